```python
import math
import jax
import jax.numpy as jnp
from jax import lax
import numpy as np

D_MODEL = 2048
BATCH = 4
SEQ = 2048
DEPTH = 2


CTX_LEN = 256
GRID_W = 64
HEAD_DIM = 128
MIX_WIDTH = D_MODEL
S5_WIDTH = MIX_WIDTH // 4
S5_GROUP = 16
S5_GROUPS = S5_WIDTH // S5_GROUP
S5_STATE = 64
RET_WIDTH = 3 * MIX_WIDTH // 8
RET_HEADS = RET_WIDTH // HEAD_DIM
MLSTM_WIDTH = MIX_WIDTH - S5_WIDTH - RET_WIDTH
MLSTM_HEADS = MLSTM_WIDTH // HEAD_DIM
IN_WIDTH = S5_WIDTH + 4 * RET_WIDTH + 4 * MLSTM_WIDTH + 4 * MLSTM_HEADS
IN_SPLITS = (S5_WIDTH, S5_WIDTH + 4 * RET_WIDTH, S5_WIDTH + 4 * RET_WIDTH + 4 * MLSTM_WIDTH)
D_FF = 4 * D_MODEL
CHUNK = 128
ROPE_BASE = 10000.0
EPS = 1e-6
S5_DT_MIN = 0.001
S5_DT_MAX = 0.1

kernel_name = 'hybrid_s5_retention_mlstm_dit_block'

F32 = jnp.float32


def _rmsnorm(x, w):
    xf = x.astype(F32)
    y = xf * lax.rsqrt(jnp.mean(xf * xf, axis=-1, keepdims=True) + EPS)
    return (y * w.astype(F32)).astype(x.dtype)


def _modulate(h, shift, scale):
    return h * (1.0 + scale) + shift


def _heads(t, n_heads):
    b, l, _ = t.shape
    return t.reshape(b, l, n_heads, -1).transpose(0, 2, 1, 3).astype(F32)


def _flip(t, rev, axis):
    return jnp.flip(t, axis=axis) if rev else t


def _chunks(t):
    b, hh, l = t.shape[:3]
    return jnp.moveaxis(t.reshape((b, hh, l // CHUNK, CHUNK) + t.shape[3:]), 2, 0)


def _unchunk(t):
    t = jnp.moveaxis(t, 0, 2)
    return t.reshape(t.shape[:2] + (t.shape[2] * t.shape[3],) + t.shape[4:])


def _rope_2d_tables(rows, cols):
    quarter = HEAD_DIM // 4
    inv = ROPE_BASE ** (-jnp.arange(quarter, dtype=F32) / quarter)
    ang = jnp.concatenate([rows[:, None] * inv, cols[:, None] * inv], axis=-1)
    return jnp.cos(ang), jnp.sin(ang)


def _apply_rope(t, cos, sin):
    half = t.shape[-1] // 2
    t1, t2 = t[..., :half], t[..., half:]
    return jnp.concatenate([t1 * cos - t2 * sin, t1 * sin + t2 * cos], axis=-1)


def _head_norm(o, w, center):
    if center:
        o = o - jnp.mean(o, axis=-1, keepdims=True)
    o = o * lax.rsqrt(jnp.mean(o * o, axis=-1, keepdims=True) + EPS)
    b, hh, l, dh = o.shape
    return o.transpose(0, 2, 1, 3).reshape(b, l, hh * dh) * w.astype(F32)


def _cmul(ar, ai, br, bi):
    return ar * br - ai * bi, ar * bi + ai * br


def _s5_discretise(lam_re, lam_im, log_step, b_re, b_im):
    lam_re = jnp.minimum(lam_re.astype(F32), -1e-4)
    lam_im = lam_im.astype(F32)
    step = jnp.exp(log_step.astype(F32))[:, None]
    mag = jnp.exp(lam_re * step)
    ab_re, ab_im = mag * jnp.cos(lam_im * step), mag * jnp.sin(lam_im * step)
    den = lam_re * lam_re + lam_im * lam_im
    nr, ni = _cmul(ab_re - 1.0, ab_im, lam_re / den, -lam_im / den)
    bb_re, bb_im = _cmul(nr[..., None], ni[..., None], b_re.astype(F32), b_im.astype(F32))
    return ab_re, ab_im, bb_re, bb_im


def _s5_scan(u, ab_re, ab_im, bb_re, bb_im, x0_re, x0_im):
    bu_re = jnp.einsum('gpn,blgn->blgp', bb_re, u)
    bu_im = jnp.einsum('gpn,blgn->blgp', bb_im, u)
    ir, ii = _cmul(ab_re, ab_im, x0_re, x0_im)
    bu_re = bu_re.at[:, 0].add(ir)
    bu_im = bu_im.at[:, 0].add(ii)
    a_re = jnp.broadcast_to(ab_re, bu_re.shape)
    a_im = jnp.broadcast_to(ab_im, bu_im.shape)

    def combine(e1, e2):
        a1r, a1i, b1r, b1i = e1
        a2r, a2i, b2r, b2i = e2
        ar, ai = _cmul(a2r, a2i, a1r, a1i)
        br, bi = _cmul(a2r, a2i, b1r, b1i)
        return ar, ai, br + b2r, bi + b2i

    _, _, xr, xi = lax.associative_scan(combine, (a_re, a_im, bu_re, bu_im), axis=1)
    return xr, xi


def _s5_readout(c_re, c_im, xr, xi):
    return jnp.einsum('gnp,blgp->blgn', c_re.astype(F32), xr) - jnp.einsum('gnp,blgp->blgn', c_im.astype(F32), xi)


def _s5_mixer(u_x, u_h, lam_re, lam_im, log_step, b_re, b_im, c_re, c_im, d_skip):
    def groups(u):
        b, l, _ = u.shape
        return u.astype(F32).reshape(b, l, S5_GROUPS, S5_GROUP)

    ux, uh = groups(u_x), groups(u_h)
    dsk = d_skip.astype(F32).reshape(S5_GROUPS, S5_GROUP)
    yx, yh = dsk * ux, dsk * uh
    zero = jnp.zeros((ux.shape[0], S5_GROUPS, S5_STATE), F32)
    for d in range(2):
        rev = d == 1
        ab_re, ab_im, bb_re, bb_im = _s5_discretise(lam_re[d], lam_im[d], log_step[d], b_re[d], b_im[d])
        sr, si = _s5_scan(_flip(uh, rev, 1), ab_re, ab_im, bb_re, bb_im, zero, zero)
        yh = yh + _flip(_s5_readout(c_re[d], c_im[d], sr, si), rev, 1)
        sr, si = _s5_scan(_flip(ux, rev, 1), ab_re, ab_im, bb_re, bb_im, sr[:, -1], si[:, -1])
        yx = yx + _flip(_s5_readout(c_re[d], c_im[d], sr, si), rev, 1)
    return yx, yh


def _s5_glu(y, w_glu, b_glu):
    b, l = y.shape[:2]
    z = jax.nn.gelu(y.reshape(b, l, S5_WIDTH)) @ w_glu.astype(F32) + b_glu.astype(F32)
    return z[..., :S5_WIDTH] * jax.nn.sigmoid(z[..., S5_WIDTH:])


def _retention_dir(q, k, v, log_gamma, r0):
    idx = jnp.arange(CHUNK, dtype=F32)
    rel = idx[:, None] - idx[None, :]
    lg = log_gamma[:, None, None]
    decay = jnp.where(rel >= 0, jnp.exp(lg * jnp.maximum(rel, 0.0)), 0.0)
    q_decay = jnp.exp(log_gamma[:, None] * (idx + 1.0))[..., None]
    k_decay = jnp.exp(log_gamma[:, None] * (CHUNK - 1.0 - idx))[..., None]
    chunk_decay = jnp.exp(log_gamma * CHUNK)[:, None, None]

    def step(r, inp):
        qi, ki, vi = inp
        s = jnp.einsum('bhid,bhjd->bhij', qi, ki) * decay
        o = jnp.einsum('bhij,bhjv->bhiv', s, vi) + jnp.einsum('bhid,bhdv->bhiv', qi * q_decay, r)
        r = chunk_decay * r + jnp.einsum('bhjd,bhjv->bhdv', ki * k_decay, vi)
        return r, o

    r, o = lax.scan(step, r0, (_chunks(q), _chunks(k), _chunks(v)))
    return _unchunk(o), r


def _retention_mixer(lat, ctx, decay_logit):
    qx, kx, vx = lat
    qh, kh, vh = ctx
    scale = HEAD_DIM ** -0.5
    kx, kh = kx * scale, kh * scale
    zero = jnp.zeros(qh.shape[:2] + (HEAD_DIM, HEAD_DIM), F32)
    out_x, out_h = 0.0, 0.0
    for d in range(2):
        rev = d == 1
        lg = jax.nn.log_sigmoid(decay_logit[d].astype(F32))
        oh, rh = _retention_dir(_flip(qh, rev, 2), _flip(kh, rev, 2), _flip(vh, rev, 2), lg, zero)
        ox, _ = _retention_dir(_flip(qx, rev, 2), _flip(kx, rev, 2), _flip(vx, rev, 2), lg, rh)
        out_h = out_h + _flip(oh, rev, 2)
        out_x = out_x + _flip(ox, rev, 2)
    return out_x, out_h


def _mlstm_dir(q, k, v, i_pre, f_pre, state):
    tril = jnp.tril(jnp.ones((CHUNK, CHUNK), dtype=bool))
    log_f = jax.nn.log_sigmoid(f_pre)

    def step(carry, inp):
        c_mem, n_mem, m_prev = carry
        qi, ki, vi, ii, lfi = inp
        b = jnp.cumsum(lfi, axis=-1)
        log_w = jnp.where(tril, b[..., :, None] - b[..., None, :] + ii[..., None, :], -jnp.inf)
        log_a = b + m_prev[..., None]
        m_t = jnp.maximum(log_a, jnp.max(log_w, axis=-1))
        w = jnp.exp(log_w - m_t[..., None])
        a = jnp.exp(log_a - m_t)
        s = jnp.einsum('bhid,bhjd->bhij', qi, ki) * w
        num = jnp.einsum('bhij,bhjv->bhiv', s, vi) + a[..., None] * jnp.einsum('bhid,bhdv->bhiv', qi, c_mem)
        den = jnp.sum(s, axis=-1) + a * jnp.einsum('bhid,bhd->bhi', qi, n_mem)
        h = num / jnp.maximum(jnp.abs(den), jnp.exp(-m_t))[..., None]
        b_end = b[..., -1:]
        log_w_end = b_end - b + ii
        m_new = jnp.maximum(b_end[..., 0] + m_prev, jnp.max(log_w_end, axis=-1))
        a_end = jnp.exp(b_end[..., 0] + m_prev - m_new)
        w_end = jnp.exp(log_w_end - m_new[..., None])
        c_mem = a_end[..., None, None] * c_mem + jnp.einsum('bhj,bhjd,bhjv->bhdv', w_end, ki, vi)
        n_mem = a_end[..., None] * n_mem + jnp.einsum('bhj,bhjd->bhd', w_end, ki)
        return (c_mem, n_mem, m_new), h

    state, h = lax.scan(step, state, (_chunks(q), _chunks(k), _chunks(v), _chunks(i_pre), _chunks(log_f)))
    return _unchunk(h), state


def _mlstm_mixer(lat, ctx, igate_b, fgate_b):
    qx, kx, vx, gx = lat
    qh, kh, vh, gh = ctx
    scale = HEAD_DIM ** -0.5
    kx, kh = kx * scale, kh * scale
    b, hh = qh.shape[:2]
    init = (jnp.zeros((b, hh, HEAD_DIM, HEAD_DIM), F32), jnp.zeros((b, hh, HEAD_DIM), F32), jnp.zeros((b, hh), F32))
    out_x, out_h = 0.0, 0.0
    for d in range(2):
        rev = d == 1
        ib = igate_b[d].astype(F32)[:, None]
        fb = fgate_b[d].astype(F32)[:, None]
        oh, st = _mlstm_dir(_flip(qh, rev, 2), _flip(kh, rev, 2), _flip(vh, rev, 2),
                            _flip(gh[d, 0] + ib, rev, 2), _flip(gh[d, 1] + fb, rev, 2), init)
        ox, _ = _mlstm_dir(_flip(qx, rev, 2), _flip(kx, rev, 2), _flip(vx, rev, 2),
                           _flip(gx[d, 0] + ib, rev, 2), _flip(gx[d, 1] + fb, rev, 2), st)
        out_h = out_h + _flip(oh, rev, 2)
        out_x = out_x + _flip(ox, rev, 2)
    return out_x, out_h


def _gates(t):
    b, l, _ = t.shape
    return t.reshape(b, l, 2, 2, MLSTM_HEADS).transpose(2, 3, 0, 4, 1).astype(F32)


def _token_mix(a_x, a_h, cos, sin, ctx_out, s5_lam_re, s5_lam_im, s5_log_step, s5_b_re, s5_b_im,
               s5_c_re, s5_c_im, s5_d, s5_w_glu, s5_b_glu, ret_decay_logit, ret_norm_w,
               mlstm_igate_b, mlstm_fgate_b, mlstm_norm_w):
    px = jnp.split(a_x, IN_SPLITS, axis=-1)
    ph = jnp.split(a_h, IN_SPLITS, axis=-1)
    s5x, s5h = _s5_mixer(px[0], ph[0], s5_lam_re, s5_lam_im, s5_log_step, s5_b_re, s5_b_im, s5_c_re, s5_c_im, s5_d)
    rx = jnp.split(px[1], 4, axis=-1)
    rh = jnp.split(ph[1], 4, axis=-1)
    lat_r = (_apply_rope(_heads(rx[0], RET_HEADS), cos, sin), _apply_rope(_heads(rx[1], RET_HEADS), cos, sin),
             _heads(rx[2], RET_HEADS))
    ctx_r = (_heads(rh[0], RET_HEADS), _heads(rh[1], RET_HEADS), _heads(rh[2], RET_HEADS))
    retx, reth = _retention_mixer(lat_r, ctx_r, ret_decay_logit)
    mx = jnp.split(px[2], 4, axis=-1)
    mh = jnp.split(ph[2], 4, axis=-1)
    lat_m = (_heads(mx[0], MLSTM_HEADS), _heads(mx[1], MLSTM_HEADS), _heads(mx[2], MLSTM_HEADS), _gates(px[3]))
    ctx_m = (_heads(mh[0], MLSTM_HEADS), _heads(mh[1], MLSTM_HEADS), _heads(mh[2], MLSTM_HEADS), _gates(ph[3]))
    mlx, mlh = _mlstm_mixer(lat_m, ctx_m, mlstm_igate_b, mlstm_fgate_b)

    def merge(s5y, rety, rgate, mly, ogate):
        return jnp.concatenate([
            _s5_glu(s5y, s5_w_glu, s5_b_glu),
            _head_norm(rety, ret_norm_w, True) * jax.nn.silu(rgate.astype(F32)),
            _head_norm(mly, mlstm_norm_w, False) * jax.nn.sigmoid(ogate.astype(F32)),
        ], axis=-1)

    y_x = merge(s5x, retx, rx[3], mlx, mx[3])
    y_h = merge(s5h, reth, rh[3], mlh, mh[3]) if ctx_out else None
    return y_x, y_h


def _sq_relu_mlp(h, w1, w2):
    return jnp.square(jax.nn.relu(h @ w1)) @ w2


def setup_inputs(seed: int = 0) -> dict:
    key = jax.random.key(seed)
    keys = iter(jax.random.split(key, 32))

    def normal(shape, scale):
        return jax.random.normal(next(keys), shape, F32) * scale

    g, p, n, h = S5_GROUPS, S5_STATE, S5_GROUP, RET_HEADS
    x = normal((BATCH, SEQ, D_MODEL), 1.0)
    c = normal((BATCH, D_MODEL), 1.0)
    ctx = normal((BATCH, CTX_LEN, D_MODEL), 1.0)
    c_ctx = normal((D_MODEL,), 1.0)
    w_mod = normal((DEPTH, D_MODEL, 6 * D_MODEL), 0.5 * D_MODEL ** -0.5)
    b_mod = normal((DEPTH, 6 * D_MODEL), 0.02)
    norm1_w = 1.0 + normal((DEPTH, D_MODEL), 0.02)
    norm2_w = 1.0 + normal((DEPTH, D_MODEL), 0.02)
    w_in = normal((DEPTH, D_MODEL, IN_WIDTH), D_MODEL ** -0.5)
    w_out = normal((DEPTH, MIX_WIDTH, D_MODEL), MIX_WIDTH ** -0.5)
    s5_lam_re = -0.5 + normal((DEPTH, 2, g, p), 0.01)
    s5_lam_im = jnp.pi * jnp.arange(p, dtype=F32) + normal((DEPTH, 2, g, p), 0.01)
    s5_log_step = jax.random.uniform(next(keys), (DEPTH, 2, g), F32, math.log(S5_DT_MIN), math.log(S5_DT_MAX))
    s5_b_re = normal((DEPTH, 2, g, p, n), (2.0 * n) ** -0.5)
    s5_b_im = normal((DEPTH, 2, g, p, n), (2.0 * n) ** -0.5)
    s5_c_re = normal((DEPTH, 2, g, n, p), p ** -0.5)
    s5_c_im = normal((DEPTH, 2, g, n, p), p ** -0.5)
    s5_d = normal((DEPTH, S5_WIDTH), 1.0)
    s5_w_glu = normal((DEPTH, S5_WIDTH, 2 * S5_WIDTH), S5_WIDTH ** -0.5)
    s5_b_glu = normal((DEPTH, 2 * S5_WIDTH), 0.02)
    expo = 5.0 + jnp.arange(h, dtype=F32)
    ret_decay_logit = jnp.log(2.0 ** expo - 1.0) + normal((DEPTH, 2, h), 0.01)
    ret_norm_w = 1.0 + normal((DEPTH, RET_WIDTH), 0.02)
    mlstm_igate_b = normal((DEPTH, 2, MLSTM_HEADS), 0.1)
    mlstm_fgate_b = jnp.linspace(3.0, 6.0, MLSTM_HEADS, dtype=F32) + normal((DEPTH, 2, MLSTM_HEADS), 0.1)
    mlstm_norm_w = 1.0 + normal((DEPTH, MLSTM_WIDTH), 0.02)
    w_ff1 = normal((DEPTH, D_MODEL, D_FF), D_MODEL ** -0.5)
    w_ff2 = normal((DEPTH, D_FF, D_MODEL), D_FF ** -0.5)
    norm_f_w = 1.0 + normal((D_MODEL,), 0.02)
    return {'x': x, 'c': c, 'ctx': ctx, 'c_ctx': c_ctx, 'w_mod': w_mod, 'b_mod': b_mod,
            'norm1_w': norm1_w, 'norm2_w': norm2_w, 'w_in': w_in, 'w_out': w_out,
            's5_lam_re': s5_lam_re, 's5_lam_im': s5_lam_im, 's5_log_step': s5_log_step,
            's5_b_re': s5_b_re, 's5_b_im': s5_b_im, 's5_c_re': s5_c_re, 's5_c_im': s5_c_im,
            's5_d': s5_d, 's5_w_glu': s5_w_glu, 's5_b_glu': s5_b_glu,
            'ret_decay_logit': ret_decay_logit, 'ret_norm_w': ret_norm_w,
            'mlstm_igate_b': mlstm_igate_b, 'mlstm_fgate_b': mlstm_fgate_b, 'mlstm_norm_w': mlstm_norm_w,
            'w_ff1': w_ff1, 'w_ff2': w_ff2, 'norm_f_w': norm_f_w}


def reference(x, c, ctx, c_ctx, w_mod, b_mod, norm1_w, norm2_w, w_in, w_out,
              s5_lam_re, s5_lam_im, s5_log_step, s5_b_re, s5_b_im, s5_c_re, s5_c_im,
              s5_d, s5_w_glu, s5_b_glu, ret_decay_logit, ret_norm_w,
              mlstm_igate_b, mlstm_fgate_b, mlstm_norm_w, w_ff1, w_ff2, norm_f_w):
    dt = x.dtype
    n_lat = x.shape[1]
    rows_count = n_lat // GRID_W
    rows = jnp.repeat(jnp.arange(rows_count, dtype=F32), GRID_W)
    cols = jnp.tile(jnp.arange(GRID_W, dtype=F32), rows_count)
    cos, sin = _rope_2d_tables(rows, cols)
    silu_c = jax.nn.silu(c)[:, None, :]
    silu_cc = jax.nn.silu(c_ctx)[None, None, :]
    h = ctx
    for l in range(DEPTH):
        last = l == DEPTH - 1
        mod_x = jnp.split(silu_c @ w_mod[l] + b_mod[l], 6, axis=-1)
        mod_h = jnp.split(silu_cc @ w_mod[l] + b_mod[l], 6, axis=-1)
        a_x = _modulate(_rmsnorm(x, norm1_w[l]), mod_x[0], mod_x[1]) @ w_in[l]
        a_h = _modulate(_rmsnorm(h, norm1_w[l]), mod_h[0], mod_h[1]) @ w_in[l]
        y_x, y_h = _token_mix(a_x, a_h, cos, sin, not last,
                              s5_lam_re[l], s5_lam_im[l], s5_log_step[l], s5_b_re[l], s5_b_im[l],
                              s5_c_re[l], s5_c_im[l], s5_d[l], s5_w_glu[l], s5_b_glu[l],
                              ret_decay_logit[l], ret_norm_w[l],
                              mlstm_igate_b[l], mlstm_fgate_b[l], mlstm_norm_w[l])
        x = x + mod_x[2] * (y_x.astype(dt) @ w_out[l])
        x = x + mod_x[5] * _sq_relu_mlp(_modulate(_rmsnorm(x, norm2_w[l]), mod_x[3], mod_x[4]), w_ff1[l], w_ff2[l])
        if not last:
            h = h + mod_h[2] * (y_h.astype(dt) @ w_out[l])
            h = h + mod_h[5] * _sq_relu_mlp(_modulate(_rmsnorm(h, norm2_w[l]), mod_h[3], mod_h[4]), w_ff1[l], w_ff2[l])
    return _rmsnorm(x, norm_f_w)
```

```python
import functools
import math

import jax
import jax.numpy as jnp
from jax import lax
from jax.experimental import pallas as pl
from jax.experimental.pallas import tpu as pltpu

F32 = jnp.float32
BF16 = jnp.bfloat16

D_MODEL = 2048
BATCH = 4
SEQ = 2048
DEPTH = 2
CTX_LEN = 256
GRID_W = 64
HEAD_DIM = 128
S5_WIDTH = 512
S5_GROUP = 16
S5_GROUPS = 32
S5_STATE = 64
RET_WIDTH = 768
MLSTM_WIDTH = 768
HEADS = 6
IN_WIDTH = S5_WIDTH + 4 * RET_WIDTH + 4 * MLSTM_WIDTH + 4 * HEADS
D_FF = 4 * D_MODEL
CHUNK = 128
ROPE_BASE = 10000.0
EPS = 1e-6

NX = BATCH * SEQ
NH = BATCH * CTX_LEN
NT = NX + NH
MOD_ROWS = 8
CTX_GROUP = BATCH

COL_RET = 0
COL_ML = 4 * RET_WIDTH
COL_S5 = COL_ML + 4 * MLSTM_WIDTH
COL_GATE = COL_S5 + S5_WIDTH
IN_PAD = 7168

X_CHUNKS = SEQ // CHUNK
H_CHUNKS = CTX_LEN // CHUNK
MIX_STEPS = X_CHUNKS + H_CHUNKS

S5_T = 16
S5_XB = SEQ // S5_T
S5_HB = CTX_LEN // S5_T
S5_NB = S5_XB + S5_HB
S5_BP = 8
S5_ROWS = S5_NB * S5_BP
S5_TN = S5_T * S5_GROUP
S5_P2 = 2 * S5_STATE

VMEM_LIMIT = 48 * 1024 * 1024


def _cparams(sem):
    return pltpu.CompilerParams(dimension_semantics=sem, vmem_limit_bytes=VMEM_LIMIT)


def _sigmoid(x):
    return 1.0 / (1.0 + jnp.exp(-x))


def _silu(x):
    return x * _sigmoid(x)


def _log_sigmoid(x):
    return jnp.minimum(x, 0.0) - jnp.log1p(jnp.exp(-jnp.abs(x)))


def _gelu_tanh(x):
    c = math.sqrt(2.0 / math.pi)
    return 0.5 * x * (1.0 + jnp.tanh(c * (x + 0.044715 * (x * x * x))))


def _row_group(i, tm):
    return jnp.minimum((i * tm) // SEQ, CTX_GROUP)


def _mod_kernel(c_ref, w_ref, b_ref, o_ref):
    c = c_ref[...]
    lhs = _silu(c).astype(BF16)
    o_ref[...] = jnp.dot(lhs, w_ref[...].astype(BF16), preferred_element_type=F32) + b_ref[...]


def _modulation(cvec, w_mod, b_mod):
    tn = 512
    n = 6 * D_MODEL
    return pl.pallas_call(
        _mod_kernel,
        out_shape=jax.ShapeDtypeStruct((DEPTH, MOD_ROWS, n), F32),
        grid=(DEPTH, n // tn),
        in_specs=[
            pl.BlockSpec((MOD_ROWS, D_MODEL), lambda l, j: (0, 0)),
            pl.BlockSpec((None, D_MODEL, tn), lambda l, j: (l, 0, j)),
            pl.BlockSpec((None, 1, tn), lambda l, j: (l, 0, j)),
        ],
        out_specs=pl.BlockSpec((None, MOD_ROWS, tn), lambda l, j: (l, 0, j)),
        compiler_params=_cparams(("arbitrary", "arbitrary")),
        name="modulation",
    )(cvec, w_mod, b_mod.reshape(DEPTH, 1, n))


def _rms_modulate(x, nw, shift, scale):
    y = x * lax.rsqrt(jnp.mean(x * x, axis=-1, keepdims=True) + EPS) * nw
    return y * (1.0 + scale) + shift


def _in_proj_kernel(x_ref, nw_ref, shift_ref, scale_ref, w_ref, o_ref, lhs_scr):
    @pl.when(pl.program_id(1) == 0)
    def _():
        lhs_scr[...] = _rms_modulate(x_ref[...], nw_ref[...], shift_ref[...], scale_ref[...]).astype(BF16)

    o_ref[...] = jnp.dot(lhs_scr[...], w_ref[...], preferred_element_type=F32)


def _in_proj(xs, norm_w, mod3, w_in_p):
    tm, tn = 1024, 512
    return pl.pallas_call(
        _in_proj_kernel,
        out_shape=jax.ShapeDtypeStruct((NT, IN_PAD), F32),
        grid=(NT // tm, IN_PAD // tn),
        in_specs=[
            pl.BlockSpec((tm, D_MODEL), lambda i, j: (i, 0)),
            pl.BlockSpec((1, D_MODEL), lambda i, j: (0, 0)),
            pl.BlockSpec((None, 1, D_MODEL), lambda i, j: (_row_group(i, tm), 0, 0)),
            pl.BlockSpec((None, 1, D_MODEL), lambda i, j: (_row_group(i, tm), 0, 1)),
            pl.BlockSpec((D_MODEL, tn), lambda i, j: (0, j)),
        ],
        out_specs=pl.BlockSpec((tm, tn), lambda i, j: (i, j)),
        scratch_shapes=[pltpu.VMEM((tm, D_MODEL), BF16)],
        compiler_params=_cparams(("arbitrary", "arbitrary")),
        name="in_proj",
    )(xs, norm_w.reshape(1, D_MODEL), mod3, mod3, w_in_p)


def _s5_prep(lam_re, lam_im, log_step, b_re, b_im, c_re, c_im, d_skip):
    hp = lax.Precision.HIGHEST
    lam_re = jnp.minimum(lam_re.astype(F32), -1e-4)
    lam_im = lam_im.astype(F32)
    step = jnp.exp(log_step.astype(F32))[..., None]
    mag = jnp.exp(lam_re * step)
    ab_re, ab_im = mag * jnp.cos(lam_im * step), mag * jnp.sin(lam_im * step)
    den = lam_re * lam_re + lam_im * lam_im
    ir, ii = lam_re / den, -lam_im / den
    nr = (ab_re - 1.0) * ir - ab_im * ii
    ni = (ab_re - 1.0) * ii + ab_im * ir
    bre, bim = b_re.astype(F32), b_im.astype(F32)
    bb_re = nr[..., None] * bre - ni[..., None] * bim
    bb_im = nr[..., None] * bim + ni[..., None] * bre
    pr, pi = [jnp.ones_like(ab_re)], [jnp.zeros_like(ab_im)]
    for _ in range(S5_T):
        pr.append(pr[-1] * ab_re - pi[-1] * ab_im)
        pi.append(pr[-2] * ab_im + pi[-1] * ab_re)
    pr, pi = jnp.stack(pr), jnp.stack(pi)
    cre, cim = c_re.astype(F32), c_im.astype(F32)
    ca_r = cre[None] * pr[:, :, :, None, :] - cim[None] * pi[:, :, :, None, :]
    ca_i = cre[None] * pi[:, :, :, None, :] + cim[None] * pr[:, :, :, None, :]
    kl = (jnp.einsum('ldgnp,dgpm->ldgnm', ca_r, bb_re, precision=hp)
          - jnp.einsum('ldgnp,dgpm->ldgnm', ca_i, bb_im, precision=hp))
    tt = jnp.arange(S5_T)
    lag = tt[None, :] - tt[:, None]
    kf = jnp.where((lag >= 0)[:, :, None, None, None], kl[jnp.clip(lag, 0, S5_T), 0], 0.0)
    kb = jnp.where((lag <= 0)[:, :, None, None, None], kl[jnp.clip(-lag, 0, S5_T), 1], 0.0)
    m = (kf + kb).transpose(2, 0, 4, 1, 3).reshape(S5_GROUPS, S5_TN, S5_TN)

    def carry_in(d, powers):
        er = pr[powers, d][..., None] * bb_re[d][None] - pi[powers, d][..., None] * bb_im[d][None]
        ei = pr[powers, d][..., None] * bb_im[d][None] + pi[powers, d][..., None] * bb_re[d][None]
        er = er.transpose(1, 0, 3, 2).reshape(S5_GROUPS, S5_TN, S5_STATE)
        ei = ei.transpose(1, 0, 3, 2).reshape(S5_GROUPS, S5_TN, S5_STATE)
        return jnp.concatenate([er, ei], -1), jnp.concatenate([ei, er], -1)

    ef, ef_sw = carry_in(0, S5_T - 1 - tt)
    eb, eb_sw = carry_in(1, tt)
    wcat = jnp.concatenate([ef, eb, ef_sw, eb_sw], axis=-1)

    def read_out(d, powers):
        fr = ca_r[powers, d].transpose(1, 3, 0, 2).reshape(S5_GROUPS, S5_STATE, S5_TN)
        fi = ca_i[powers, d].transpose(1, 3, 0, 2).reshape(S5_GROUPS, S5_STATE, S5_TN)
        return jnp.concatenate([fr, -fi], axis=1)

    mf = jnp.concatenate([m, read_out(0, tt + 1), read_out(1, S5_T - tt)], axis=1)

    def lanes(re, im):
        return jnp.concatenate([re, im], axis=-1)

    a1 = jnp.concatenate([lanes(pr[S5_T, 0], pr[S5_T, 0]), lanes(pr[S5_T, 1], pr[S5_T, 1])], -1)
    a2 = jnp.concatenate([lanes(-pi[S5_T, 0], pi[S5_T, 0]), lanes(-pi[S5_T, 1], pi[S5_T, 1])], -1)
    dsk = jnp.tile(d_skip.astype(F32).reshape(S5_GROUPS, S5_GROUP), (1, S5_T))
    aux = jnp.stack([a1, a2, dsk] + [jnp.zeros_like(a1)] * 5, axis=1)
    return wcat.astype(BF16), mf.astype(BF16), aux


def _s5_kernel(u_ref, wcat_ref, mf_ref, aux_ref, y_ref, w_scr, xs_scr):
    u = u_ref[...]
    ub = u.astype(BF16)
    w_scr[...] = jnp.dot(ub, wcat_ref[...], preferred_element_type=F32)
    aux = aux_ref[...]
    shape = (S5_BP, S5_P2)
    a1f = jnp.broadcast_to(aux[0:1, 0:S5_P2], shape)
    a1b = jnp.broadcast_to(aux[0:1, S5_P2:], shape)
    a2f = jnp.broadcast_to(aux[1:2, 0:S5_P2], shape)
    a2b = jnp.broadcast_to(aux[1:2, S5_P2:], shape)

    def body(s, carry):
        xf, xfs, xb, xbs = carry
        rf = pl.multiple_of(s * S5_BP, S5_BP)
        gb = jnp.where(s < S5_HB, S5_HB - 1 - s, S5_NB + S5_HB - 1 - s)
        rb = pl.multiple_of(gb * S5_BP, S5_BP)
        xs_scr[pl.ds(rf, S5_BP), 0:S5_P2] = xf
        xs_scr[pl.ds(rb, S5_BP), S5_P2:2 * S5_P2] = xb
        wf = w_scr[pl.ds(rf, S5_BP), 0:S5_P2]
        wb = w_scr[pl.ds(rb, S5_BP), S5_P2:2 * S5_P2]
        wfs = w_scr[pl.ds(rf, S5_BP), 2 * S5_P2:3 * S5_P2]
        wbs = w_scr[pl.ds(rb, S5_BP), 3 * S5_P2:4 * S5_P2]
        nxf = a1f * xf + a2f * xfs + wf
        nxfs = a1f * xfs - a2f * xf + wfs
        nxb = a1b * xb + a2b * xbs + wb
        nxbs = a1b * xbs - a2b * xb + wbs
        return nxf, nxfs, nxb, nxbs

    z = jnp.zeros(shape, F32)
    lax.fori_loop(0, S5_NB, body, (z, z, z, z))
    y = jnp.dot(ub, mf_ref[0:S5_TN, :], preferred_element_type=F32)
    y = y + jnp.dot(xs_scr[...].astype(BF16), mf_ref[S5_TN:, :], preferred_element_type=F32)
    y_ref[...] = y + u * aux[2:3, :]


def _s5_mix(a, wcat, mf, aux):
    u = a[:, COL_S5:COL_S5 + S5_WIDTH]

    def blocks(t, nb):
        t = t.reshape(BATCH, nb, S5_T, S5_GROUPS, S5_GROUP).transpose(3, 1, 0, 2, 4)
        t = jnp.pad(t, ((0, 0), (0, 0), (0, S5_BP - BATCH), (0, 0), (0, 0)))
        return t.reshape(S5_GROUPS, nb * S5_BP, S5_TN)

    ug = jnp.concatenate([blocks(u[NX:], S5_HB), blocks(u[:NX], S5_XB)], axis=1)
    y = pl.pallas_call(
        _s5_kernel,
        out_shape=jax.ShapeDtypeStruct((S5_GROUPS, S5_ROWS, S5_TN), F32),
        grid=(S5_GROUPS,),
        in_specs=[
            pl.BlockSpec((None, S5_ROWS, S5_TN), lambda g: (g, 0, 0)),
            pl.BlockSpec((None, S5_TN, 4 * S5_P2), lambda g: (g, 0, 0)),
            pl.BlockSpec((None, 2 * S5_TN, S5_TN), lambda g: (g, 0, 0)),
            pl.BlockSpec((None, 8, S5_TN), lambda g: (g, 0, 0)),
        ],
        out_specs=pl.BlockSpec((None, S5_ROWS, S5_TN), lambda g: (g, 0, 0)),
        scratch_shapes=[pltpu.VMEM((S5_ROWS, 4 * S5_P2), F32), pltpu.VMEM((S5_ROWS, 2 * S5_P2), F32)],
        compiler_params=_cparams(("arbitrary",)),
        name="s5_scan",
    )(ug, wcat, mf, aux)

    def unblocks(t, nb):
        t = t.reshape(S5_GROUPS, nb, S5_BP, S5_T, S5_GROUP)[:, :, :BATCH]
        return t.transpose(2, 1, 3, 0, 4).reshape(BATCH * nb * S5_T, S5_WIDTH)

    return jnp.concatenate([unblocks(y[:, S5_HB * S5_BP:], S5_XB), unblocks(y[:, :S5_HB * S5_BP], S5_HB)], axis=0)


def _glu_kernel(y_ref, w_ref, b_ref, o_ref):
    g = _gelu_tanh(y_ref[...]).astype(BF16)
    z = jnp.dot(g, w_ref[...], preferred_element_type=F32) + b_ref[...]
    o_ref[...] = z[:, :S5_WIDTH] * _sigmoid(z[:, S5_WIDTH:])


def _s5_glu(y, w_glu, b_glu):
    tm = 1024
    return pl.pallas_call(
        _glu_kernel,
        out_shape=jax.ShapeDtypeStruct((NT, S5_WIDTH), F32),
        grid=(NT // tm,),
        in_specs=[
            pl.BlockSpec((tm, S5_WIDTH), lambda i: (i, 0)),
            pl.BlockSpec((S5_WIDTH, 2 * S5_WIDTH), lambda i: (0, 0)),
            pl.BlockSpec((1, 2 * S5_WIDTH), lambda i: (0, 0)),
        ],
        out_specs=pl.BlockSpec((tm, S5_WIDTH), lambda i: (i, 0)),
        compiler_params=_cparams(("arbitrary",)),
        name="s5_glu",
    )(y, w_glu.astype(BF16), b_glu.reshape(1, 2 * S5_WIDTH))


def _chunk_index(d, s):
    is_ctx = s < H_CHUNKS
    fwd = jnp.where(is_ctx, s, s - H_CHUNKS)
    bwd = jnp.where(is_ctx, H_CHUNKS - 1 - s, MIX_STEPS - 1 - s)
    return is_ctx, jnp.where(d == 0, fwd, bwd)


def _token_block(b, d, s):
    is_ctx, c = _chunk_index(d, s)
    return jnp.where(is_ctx, NX // CHUNK + b * H_CHUNKS + c, b * X_CHUNKS + c)


def _rope_block(b, d, s):
    is_ctx, c = _chunk_index(d, s)
    return jnp.where(is_ctx, X_CHUNKS + c, c)


def _dot_nt(a, b):
    return lax.dot_general(a, b, (((1,), (1,)), ((), ())), preferred_element_type=F32)


def _dot_tn(a, b):
    return lax.dot_general(a, b, (((0,), (0,)), ((), ())), preferred_element_type=F32)


def _dot(a, b):
    return jnp.dot(a, b, preferred_element_type=F32)


def _head(ref, h):
    return ref[:, h * HEAD_DIM:(h + 1) * HEAD_DIM]


def _chunk_pos(rev):
    i = lax.broadcasted_iota(jnp.int32, (CHUNK, CHUNK), 0)
    j = lax.broadcasted_iota(jnp.int32, (CHUNK, CHUNK), 1)
    pi = jnp.where(rev, CHUNK - 1 - i, i)
    pj = jnp.where(rev, CHUNK - 1 - j, j)
    return pi, pj


def _ret_kernel(dl_ref, q_ref, k_ref, v_ref, cos_ref, sin_ref, o_ref, r_scr, dec_scr, qd_scr, kd_scr, cd_scr):
    d = pl.program_id(1)
    s = pl.program_id(2)
    rev = d == 1

    @pl.when(s == 0)
    def _():
        pi, pj = _chunk_pos(rev)
        rel = (pi - pj).astype(F32)
        pif = pi.astype(F32)
        for h in range(HEADS):
            lg = _log_sigmoid(dl_ref[h])[0:1, :]
            dec_scr[h] = jnp.where(rel >= 0, jnp.exp(lg * jnp.maximum(rel, 0.0)), 0.0)
            qd_scr[h] = jnp.exp(lg * (pif + 1.0))
            kd_scr[h] = jnp.exp(lg * (CHUNK - 1.0 - pif))
            cd_scr[h] = jnp.exp(jnp.broadcast_to(lg, (8, HEAD_DIM)) * CHUNK)
        r_scr[...] = jnp.zeros_like(r_scr)

    cos = cos_ref[...]
    sin = sin_ref[...]
    scale = HEAD_DIM ** -0.5
    for h in range(HEADS):
        q = _head(q_ref, h)
        k = _head(k_ref, h)
        v = _head(v_ref, h).astype(BF16)
        q = q * cos + pltpu.roll(q, HEAD_DIM // 2, 1) * sin
        k = (k * cos + pltpu.roll(k, HEAD_DIM // 2, 1) * sin) * scale
        r = r_scr[h]
        sm = _dot_nt(q.astype(BF16), k.astype(BF16)) * dec_scr[h]
        o = _dot(sm.astype(BF16), v) + _dot((q * qd_scr[h]).astype(BF16), r.astype(BF16))
        r_scr[h] = cd_scr[h][0:1, :] * r + _dot_tn((k * kd_scr[h]).astype(BF16), v)
        o_ref[:, h * HEAD_DIM:(h + 1) * HEAD_DIM] = o


def _retention(a, rope_cos, rope_sin, decay_logit):
    dl = jnp.broadcast_to(decay_logit.astype(F32)[:, :, None, None], (2, HEADS, 8, HEAD_DIM))
    tok = lambda col: pl.BlockSpec((CHUNK, RET_WIDTH), lambda b, d, s: (_token_block(b, d, s), col))
    rope = pl.BlockSpec((CHUNK, HEAD_DIM), lambda b, d, s: (_rope_block(b, d, s), 0))
    hh = pltpu.VMEM((HEADS, CHUNK, HEAD_DIM), F32)
    return pl.pallas_call(
        _ret_kernel,
        out_shape=jax.ShapeDtypeStruct((2, NT, RET_WIDTH), F32),
        grid=(BATCH, 2, MIX_STEPS),
        in_specs=[
            pl.BlockSpec((None, HEADS, 8, HEAD_DIM), lambda b, d, s: (d, 0, 0, 0)),
            tok(COL_RET // RET_WIDTH), tok(COL_RET // RET_WIDTH + 1), tok(COL_RET // RET_WIDTH + 2),
            rope, rope,
        ],
        out_specs=pl.BlockSpec((None, CHUNK, RET_WIDTH), lambda b, d, s: (d, _token_block(b, d, s), 0)),
        scratch_shapes=[hh, hh, hh, hh, pltpu.VMEM((HEADS, 8, HEAD_DIM), F32)],
        compiler_params=_cparams(("arbitrary", "arbitrary", "arbitrary")),
        name="retention",
    )(dl, a, a, a, rope_cos, rope_sin)


def _split3(x):
    h = x.astype(BF16)
    r = x - h.astype(F32)
    m = r.astype(BF16)
    return h, m, (r - m.astype(F32)).astype(BF16)


def _mlstm_kernel(q_ref, k_ref, v_ref, g_ref, gb_ref, o_ref, c_scr, n_scr, m_scr):
    d = pl.program_id(1)
    s = pl.program_id(2)
    rev = d == 1

    @pl.when(s == 0)
    def _():
        c_scr[...] = jnp.zeros_like(c_scr)
        n_scr[...] = jnp.zeros_like(n_scr)
        m_scr[...] = jnp.zeros_like(m_scr)

    pi, pj = _chunk_pos(rev)
    causal = pj <= pi
    tri = jnp.where(causal, 1.0, 0.0).astype(BF16)
    pre = g_ref[...] + gb_ref[...]
    pre_t = pre.T[0:32, :]
    lf_c = _log_sigmoid(pre)
    lf_r = _log_sigmoid(pre_t)
    b_cols = sum(_dot(tri, part) for part in _split3(lf_c))
    b_rows = sum(_dot_nt(part, tri) for part in _split3(lf_r))
    scale = HEAD_DIM ** -0.5
    neg_inf = -jnp.inf
    for h in range(HEADS):
        ci0, cf0, ci1, cf1 = h, HEADS + h, 2 * HEADS + h, 3 * HEADS + h
        b_col = jnp.where(rev, b_cols[:, cf1:cf1 + 1], b_cols[:, cf0:cf0 + 1])
        i_col = jnp.where(rev, pre[:, ci1:ci1 + 1], pre[:, ci0:ci0 + 1])
        b_row = jnp.where(rev, b_rows[cf1:cf1 + 1, :], b_rows[cf0:cf0 + 1, :])
        i_row = jnp.where(rev, pre_t[ci1:ci1 + 1, :], pre_t[ci0:ci0 + 1, :])
        b_end = jnp.where(rev, b_cols[0:1, cf1:cf1 + 1], b_cols[CHUNK - 1:CHUNK, cf0:cf0 + 1])
        m_prev = m_scr[h][0:1, 0:1]
        q = _head(q_ref, h)
        k = _head(k_ref, h) * scale
        v = _head(v_ref, h).astype(BF16)
        log_w = jnp.where(causal, b_col - b_row + i_row, neg_inf)
        log_a = b_col + m_prev
        m_t = jnp.maximum(log_a, jnp.max(log_w, axis=-1, keepdims=True))
        w = jnp.exp(log_w - m_t)
        a = jnp.exp(log_a - m_t)
        qb = q.astype(BF16)
        sm = _dot_nt(qb, k.astype(BF16)) * w
        c_mem = c_scr[h]
        n_mem = n_scr[h][0:1, :]
        num = _dot(sm.astype(BF16), v) + a * _dot(qb, c_mem.astype(BF16))
        den = jnp.sum(sm, axis=-1, keepdims=True) + a * jnp.sum(q * n_mem, axis=-1, keepdims=True)
        o_ref[:, h * HEAD_DIM:(h + 1) * HEAD_DIM] = num / jnp.maximum(jnp.abs(den), jnp.exp(-m_t))
        log_w_end = b_end - b_col + i_col
        m_new = jnp.maximum(b_end + m_prev, jnp.max(log_w_end, axis=0, keepdims=True))
        a_end = jnp.exp(b_end + m_prev - m_new)
        kw = k * jnp.exp(log_w_end - m_new)
        c_scr[h] = a_end * c_mem + _dot_tn(kw.astype(BF16), v)
        n_scr[h] = jnp.broadcast_to(a_end * n_mem + jnp.sum(kw, axis=0, keepdims=True), (8, HEAD_DIM))
        m_scr[h] = jnp.broadcast_to(m_new, (8, HEAD_DIM))


def _mlstm(a, igate_b, fgate_b):
    gb = jnp.stack([igate_b.astype(F32), fgate_b.astype(F32)], axis=1).reshape(1, 4 * HEADS)
    gb = jnp.pad(gb, ((0, 0), (0, HEAD_DIM - 4 * HEADS)))
    col0 = COL_ML // MLSTM_WIDTH
    tok = lambda col: pl.BlockSpec((CHUNK, MLSTM_WIDTH), lambda b, d, s: (_token_block(b, d, s), col))
    small = pltpu.VMEM((HEADS, 8, HEAD_DIM), F32)
    return pl.pallas_call(
        _mlstm_kernel,
        out_shape=jax.ShapeDtypeStruct((2, NT, MLSTM_WIDTH), F32),
        grid=(BATCH, 2, MIX_STEPS),
        in_specs=[
            tok(col0), tok(col0 + 1), tok(col0 + 2),
            pl.BlockSpec((CHUNK, HEAD_DIM), lambda b, d, s: (_token_block(b, d, s), COL_GATE // HEAD_DIM)),
            pl.BlockSpec((1, HEAD_DIM), lambda b, d, s: (0, 0)),
        ],
        out_specs=pl.BlockSpec((None, CHUNK, MLSTM_WIDTH), lambda b, d, s: (d, _token_block(b, d, s), 0)),
        scratch_shapes=[pltpu.VMEM((HEADS, CHUNK, HEAD_DIM), F32), small, small],
        compiler_params=_cparams(("arbitrary", "arbitrary", "arbitrary")),
        name="mlstm",
    )(a, a, a, a, gb)


def _head_norm(o, center):
    if center:
        o = o - jnp.mean(o, axis=-1, keepdims=True)
    return o * lax.rsqrt(jnp.mean(o * o, axis=-1, keepdims=True) + EPS)


def _out_proj_kernel(s5_ref, ro_ref, rg_ref, rw_ref, mo_ref, mg_ref, mw_ref, x_ref, gate_ref, w_ref,
                     o_ref, lhs_scr):
    @pl.when(pl.program_id(1) == 0)
    def _():
        lhs_scr[:, 0:S5_WIDTH] = s5_ref[...].astype(BF16)
        for h in range(HEADS):
            sl = slice(h * HEAD_DIM, (h + 1) * HEAD_DIM)
            r = _head_norm(ro_ref[0, :, sl] + ro_ref[1, :, sl], True) * rw_ref[:, sl] * _silu(rg_ref[:, sl])
            lhs_scr[:, S5_WIDTH + h * HEAD_DIM:S5_WIDTH + (h + 1) * HEAD_DIM] = r.astype(BF16)
            m = _head_norm(mo_ref[0, :, sl] + mo_ref[1, :, sl], False) * mw_ref[:, sl] * _sigmoid(mg_ref[:, sl])
            c0 = S5_WIDTH + RET_WIDTH + h * HEAD_DIM
            lhs_scr[:, c0:c0 + HEAD_DIM] = m.astype(BF16)

    o_ref[...] = x_ref[...] + gate_ref[...] * jnp.dot(lhs_scr[...], w_ref[...], preferred_element_type=F32)


def _out_proj(rows, s5y, ret_o, ml_o, a, ret_norm_w, mlstm_norm_w, xs, mod3, w_out):
    tm, tn = 512, 512
    return pl.pallas_call(
        _out_proj_kernel,
        out_shape=jax.ShapeDtypeStruct((rows, D_MODEL), F32),
        grid=(rows // tm, D_MODEL // tn),
        in_specs=[
            pl.BlockSpec((tm, S5_WIDTH), lambda i, j: (i, 0)),
            pl.BlockSpec((2, tm, RET_WIDTH), lambda i, j: (0, i, 0)),
            pl.BlockSpec((tm, RET_WIDTH), lambda i, j: (i, COL_RET // RET_WIDTH + 3)),
            pl.BlockSpec((1, RET_WIDTH), lambda i, j: (0, 0)),
            pl.BlockSpec((2, tm, MLSTM_WIDTH), lambda i, j: (0, i, 0)),
            pl.BlockSpec((tm, MLSTM_WIDTH), lambda i, j: (i, COL_ML // MLSTM_WIDTH + 3)),
            pl.BlockSpec((1, MLSTM_WIDTH), lambda i, j: (0, 0)),
            pl.BlockSpec((tm, tn), lambda i, j: (i, j)),
            pl.BlockSpec((None, 1, tn), lambda i, j: (_row_group(i, tm), 0, 2 * (D_MODEL // tn) + j)),
            pl.BlockSpec((D_MODEL, tn), lambda i, j: (0, j)),
        ],
        out_specs=pl.BlockSpec((tm, tn), lambda i, j: (i, j)),
        scratch_shapes=[pltpu.VMEM((tm, D_MODEL), BF16)],
        compiler_params=_cparams(("arbitrary", "arbitrary")),
        name="out_proj",
    )(s5y, ret_o, a, ret_norm_w.reshape(1, RET_WIDTH), ml_o, a, mlstm_norm_w.reshape(1, MLSTM_WIDTH),
      xs, mod3, w_out)


def _mlp_kernel(x_ref, nw_ref, shift_ref, scale_ref, gate_ref, w1_ref, w2_ref, nf_ref, o_ref,
                lhs_scr, acc_scr, *, final_norm):
    f = pl.program_id(1)

    @pl.when(f == 0)
    def _():
        lhs_scr[...] = _rms_modulate(x_ref[...], nw_ref[...], shift_ref[...], scale_ref[...]).astype(BF16)
        acc_scr[...] = jnp.zeros_like(acc_scr)

    hid = jnp.dot(lhs_scr[...], w1_ref[...], preferred_element_type=F32)
    hid = jnp.square(jnp.maximum(hid, 0.0)).astype(BF16)
    acc_scr[...] += jnp.dot(hid, w2_ref[...], preferred_element_type=F32)

    @pl.when(f == pl.num_programs(1) - 1)
    def _():
        y = x_ref[...] + gate_ref[...] * acc_scr[...]
        if final_norm:
            y = y * lax.rsqrt(jnp.mean(y * y, axis=-1, keepdims=True) + EPS) * nf_ref[...]
        o_ref[...] = y


def _mlp(rows, xs, norm_w, mod3, w1, w2, norm_f_w, final_norm):
    tm, tf = 512, 512
    modspec = lambda k: pl.BlockSpec((None, 1, D_MODEL), lambda i, f: (_row_group(i, tm), 0, k))
    return pl.pallas_call(
        functools.partial(_mlp_kernel, final_norm=final_norm),
        out_shape=jax.ShapeDtypeStruct((rows, D_MODEL), F32),
        grid=(rows // tm, D_FF // tf),
        in_specs=[
            pl.BlockSpec((tm, D_MODEL), lambda i, f: (i, 0)),
            pl.BlockSpec((1, D_MODEL), lambda i, f: (0, 0)),
            modspec(3), modspec(4), modspec(5),
            pl.BlockSpec((D_MODEL, tf), lambda i, f: (0, f)),
            pl.BlockSpec((tf, D_MODEL), lambda i, f: (f, 0)),
            pl.BlockSpec((1, D_MODEL), lambda i, f: (0, 0)),
        ],
        out_specs=pl.BlockSpec((tm, D_MODEL), lambda i, f: (i, 0)),
        scratch_shapes=[pltpu.VMEM((tm, D_MODEL), BF16), pltpu.VMEM((tm, D_MODEL), F32)],
        compiler_params=_cparams(("arbitrary", "arbitrary")),
        name="mlp",
    )(xs, norm_w.reshape(1, D_MODEL), mod3, mod3, mod3, w1, w2, norm_f_w.reshape(1, D_MODEL))


def _rope_tables():
    quarter = HEAD_DIM // 4
    rows = jnp.repeat(jnp.arange(SEQ // GRID_W, dtype=F32), GRID_W)
    cols = jnp.tile(jnp.arange(GRID_W, dtype=F32), SEQ // GRID_W)
    inv = ROPE_BASE ** (-jnp.arange(quarter, dtype=F32) / quarter)
    ang = jnp.concatenate([rows[:, None] * inv, cols[:, None] * inv], axis=-1)
    cos, sin = jnp.cos(ang), jnp.sin(ang)
    cos2 = jnp.concatenate([cos, cos], axis=-1)
    sin2 = jnp.concatenate([-sin, sin], axis=-1)
    cos2 = jnp.concatenate([cos2, jnp.ones((CTX_LEN, HEAD_DIM), F32)], axis=0)
    sin2 = jnp.concatenate([sin2, jnp.zeros((CTX_LEN, HEAD_DIM), F32)], axis=0)
    return cos2, sin2


def _permute_w_in(w):
    s5 = w[:, :S5_WIDTH]
    ret = w[:, S5_WIDTH:S5_WIDTH + 4 * RET_WIDTH]
    ml = w[:, S5_WIDTH + 4 * RET_WIDTH:S5_WIDTH + 4 * RET_WIDTH + 4 * MLSTM_WIDTH]
    gates = w[:, S5_WIDTH + 4 * RET_WIDTH + 4 * MLSTM_WIDTH:]
    pad = jnp.zeros((D_MODEL, IN_PAD - IN_WIDTH), w.dtype)
    return jnp.concatenate([ret, ml, s5, gates, pad], axis=1).astype(BF16)


def kernel(x, c, ctx, c_ctx, w_mod, b_mod, norm1_w, norm2_w, w_in, w_out, s5_lam_re, s5_lam_im, s5_log_step, s5_b_re, s5_b_im, s5_c_re, s5_c_im, s5_d, s5_w_glu, s5_b_glu, ret_decay_logit, ret_norm_w, mlstm_igate_b, mlstm_fgate_b, mlstm_norm_w, w_ff1, w_ff2, norm_f_w):
    rope_cos, rope_sin = _rope_tables()
    cvec = jnp.concatenate([c, c_ctx[None, :], jnp.zeros((MOD_ROWS - BATCH - 1, D_MODEL), F32)], axis=0)
    mod = _modulation(cvec, w_mod, b_mod)
    xs = jnp.concatenate([x.reshape(NX, D_MODEL), ctx.reshape(NH, D_MODEL)], axis=0)
    for l in range(DEPTH):
        last = l == DEPTH - 1
        mod3 = mod[l].reshape(MOD_ROWS, 1, 6 * D_MODEL)
        a = _in_proj(xs, norm1_w[l], mod3, _permute_w_in(w_in[l]))
        wcat, mf, aux = _s5_prep(s5_lam_re[l], s5_lam_im[l], s5_log_step[l], s5_b_re[l], s5_b_im[l],
                                 s5_c_re[l], s5_c_im[l], s5_d[l])
        s5y = _s5_glu(_s5_mix(a, wcat, mf, aux), s5_w_glu[l], s5_b_glu[l])
        ret_o = _retention(a, rope_cos, rope_sin, ret_decay_logit[l])
        ml_o = _mlstm(a, mlstm_igate_b[l], mlstm_fgate_b[l])
        rows = NX if last else NT
        xs = _out_proj(rows, s5y, ret_o, ml_o, a, ret_norm_w[l], mlstm_norm_w[l], xs, mod3,
                       w_out[l].astype(BF16))
        xs = _mlp(rows, xs, norm2_w[l], mod3, w_ff1[l].astype(BF16), w_ff2[l].astype(BF16), norm_f_w, last)
    return xs.reshape(BATCH, SEQ, D_MODEL)
```

```python
import functools
import math

import jax
import jax.numpy as jnp
from jax import lax
from jax.experimental import pallas as pl
from jax.experimental.pallas import tpu as pltpu

F32 = jnp.float32
BF16 = jnp.bfloat16

D_MODEL = 2048
BATCH = 4
SEQ = 2048
DEPTH = 2
CTX_LEN = 256
GRID_W = 64
HEAD_DIM = 128
S5_WIDTH = 512
S5_GROUP = 16
S5_GROUPS = 32
S5_STATE = 64
RET_WIDTH = 768
MLSTM_WIDTH = 768
HEADS = 6
IN_WIDTH = S5_WIDTH + 4 * RET_WIDTH + 4 * MLSTM_WIDTH + 4 * HEADS
D_FF = 4 * D_MODEL
CHUNK = 128
ROPE_BASE = 10000.0
EPS = 1e-6

NX = BATCH * SEQ
NH = BATCH * CTX_LEN
NT = NX + NH
MOD_ROWS = 8
CTX_GROUP = BATCH

COL_RET = 0
COL_ML = 4 * RET_WIDTH
COL_S5 = COL_ML + 4 * MLSTM_WIDTH
COL_GATE = COL_S5 + S5_WIDTH
IN_PAD = 7168

X_CHUNKS = SEQ // CHUNK
H_CHUNKS = CTX_LEN // CHUNK
MIX_STEPS = X_CHUNKS + H_CHUNKS

S5_T = 8
S5_LANES = 128
S5_LG = S5_LANES // S5_GROUP
S5_TILES = S5_WIDTH // S5_LANES
S5_K = S5_T * S5_LANES
S5_XB = SEQ // S5_T
S5_HB = CTX_LEN // S5_T
S5_NB = S5_XB + S5_HB
S5_ROWS = S5_NB * BATCH
S5_RC = 128

VMEM_LIMIT = 48 * 1024 * 1024
VMEM_LIMIT_BIG = 56 * 1024 * 1024


def _cparams(sem, limit=VMEM_LIMIT):
    return pltpu.CompilerParams(dimension_semantics=sem, vmem_limit_bytes=limit)


def _sigmoid(x):
    return 1.0 / (1.0 + jnp.exp(-x))


def _silu(x):
    return x * _sigmoid(x)


def _log_sigmoid(x):
    return jnp.minimum(x, 0.0) - jnp.log1p(jnp.exp(-jnp.abs(x)))


def _gelu_tanh(x):
    c = math.sqrt(2.0 / math.pi)
    return 0.5 * x * (1.0 + jnp.tanh(c * (x + 0.044715 * (x * x * x))))


def _row_group(i, tm):
    return jnp.minimum((i * tm) // SEQ, CTX_GROUP)


def _mod_kernel(c_ref, w_ref, b_ref, o_ref):
    c = c_ref[...]
    lhs = _silu(c).astype(BF16)
    o_ref[...] = jnp.dot(lhs, w_ref[...].astype(BF16), preferred_element_type=F32) + b_ref[...]


def _modulation(cvec, w_mod, b_mod):
    tn = 512
    n = 6 * D_MODEL
    return pl.pallas_call(
        _mod_kernel,
        out_shape=jax.ShapeDtypeStruct((DEPTH, MOD_ROWS, n), F32),
        grid=(DEPTH, n // tn),
        in_specs=[
            pl.BlockSpec((MOD_ROWS, D_MODEL), lambda l, j: (0, 0)),
            pl.BlockSpec((None, D_MODEL, tn), lambda l, j: (l, 0, j)),
            pl.BlockSpec((None, 1, tn), lambda l, j: (l, 0, j)),
        ],
        out_specs=pl.BlockSpec((None, MOD_ROWS, tn), lambda l, j: (l, 0, j)),
        compiler_params=_cparams(("arbitrary", "arbitrary")),
        name="modulation",
    )(cvec, w_mod, b_mod.reshape(DEPTH, 1, n))


def _rms_modulate(x, nw, shift, scale):
    y = x * lax.rsqrt(jnp.mean(x * x, axis=-1, keepdims=True) + EPS) * nw
    return y * (1.0 + scale) + shift


def _in_proj_kernel(x_ref, nw_ref, shift_ref, scale_ref, w_ref, o_ref, lhs_scr):
    @pl.when(pl.program_id(1) == 0)
    def _():
        lhs_scr[...] = _rms_modulate(x_ref[...], nw_ref[...], shift_ref[...], scale_ref[...]).astype(BF16)

    o_ref[...] = jnp.dot(lhs_scr[...], w_ref[...], preferred_element_type=F32)


def _in_proj(xs, norm_w, mod3, w_in_p):
    tm, tn = 1024, 1024
    return pl.pallas_call(
        _in_proj_kernel,
        out_shape=jax.ShapeDtypeStruct((NT, IN_PAD), F32),
        grid=(NT // tm, IN_PAD // tn),
        in_specs=[
            pl.BlockSpec((tm, D_MODEL), lambda i, j: (i, 0)),
            pl.BlockSpec((1, D_MODEL), lambda i, j: (0, 0)),
            pl.BlockSpec((None, 1, D_MODEL), lambda i, j: (_row_group(i, tm), 0, 0)),
            pl.BlockSpec((None, 1, D_MODEL), lambda i, j: (_row_group(i, tm), 0, 1)),
            pl.BlockSpec((D_MODEL, tn), lambda i, j: (0, j)),
        ],
        out_specs=pl.BlockSpec((tm, tn), lambda i, j: (i, j)),
        scratch_shapes=[pltpu.VMEM((tm, D_MODEL), BF16)],
        compiler_params=_cparams(("arbitrary", "arbitrary")),
        name="in_proj",
    )(xs, norm_w.reshape(1, D_MODEL), mod3, mod3, w_in_p)


def _s5_prep(lam_re, lam_im, log_step, b_re, b_im, c_re, c_im, d_skip):
    hp = lax.Precision.HIGHEST
    lam_re = jnp.minimum(lam_re.astype(F32), -1e-4)
    lam_im = lam_im.astype(F32)
    step = jnp.exp(log_step.astype(F32))[..., None]
    mag = jnp.exp(lam_re * step)
    ab_re, ab_im = mag * jnp.cos(lam_im * step), mag * jnp.sin(lam_im * step)
    den = lam_re * lam_re + lam_im * lam_im
    ir, ii = lam_re / den, -lam_im / den
    nr = (ab_re - 1.0) * ir - ab_im * ii
    ni = (ab_re - 1.0) * ii + ab_im * ir
    bre, bim = b_re.astype(F32), b_im.astype(F32)
    bb_re = nr[..., None] * bre - ni[..., None] * bim
    bb_im = nr[..., None] * bim + ni[..., None] * bre
    lpow = jnp.arange(S5_T + 1, dtype=F32)[:, None, None, None]
    pmag = jnp.exp(lpow * (lam_re * step))
    pr, pi = pmag * jnp.cos(lpow * (lam_im * step)), pmag * jnp.sin(lpow * (lam_im * step))
    cre, cim = c_re.astype(F32), c_im.astype(F32)
    ca_r = cre[None] * pr[:, :, :, None, :] - cim[None] * pi[:, :, :, None, :]
    ca_i = cre[None] * pi[:, :, :, None, :] + cim[None] * pr[:, :, :, None, :]
    kl = (jnp.einsum('ldgnp,dgpm->ldgnm', ca_r, bb_re, precision=hp)
          - jnp.einsum('ldgnp,dgpm->ldgnm', ca_i, bb_im, precision=hp))
    tt = jnp.arange(S5_T)
    lag = tt[None, :] - tt[:, None]
    kf = jnp.where((lag >= 0)[:, :, None, None, None], kl[jnp.clip(lag, 0, S5_T), 0], 0.0)
    kb = jnp.where((lag <= 0)[:, :, None, None, None], kl[jnp.clip(-lag, 0, S5_T), 1], 0.0)
    eye_g = jnp.eye(S5_LG, dtype=F32)
    eye_j = jnp.eye(S5_LG // 2, dtype=F32)
    eye_e = jnp.eye(2, dtype=F32)
    nt = S5_TILES
    mc = (kf + kb).reshape(S5_T, S5_T, nt, S5_LG, S5_GROUP, S5_GROUP).transpose(2, 0, 3, 5, 1, 4)
    m = mc[:, :, :, :, :, None, :] * eye_g[None, None, :, None, None, :, None]
    m = m.reshape(nt, S5_K, S5_K)

    def pairs(t):
        return (t[:, :, :, :, :, None, :, None, :] * eye_j[None, None, :, None, None, :, None, None, None]
                * eye_e[None, None, None, :, None, None, None, :, None])

    def carry_in(d, powers):
        er = pr[powers, d][..., None] * bb_re[d][None] - pi[powers, d][..., None] * bb_im[d][None]
        ei = pr[powers, d][..., None] * bb_im[d][None] + pi[powers, d][..., None] * bb_re[d][None]
        arrange = lambda t: t.reshape(S5_T, nt, S5_LG // 2, 2, S5_STATE, S5_GROUP).transpose(1, 0, 2, 3, 5, 4)
        es = jnp.stack([arrange(er), arrange(ei)], axis=5)
        return pairs(es).reshape(nt, S5_K, S5_K)

    def read_out(d, powers):
        arrange = lambda t: t.reshape(S5_T, nt, S5_LG // 2, 2, S5_GROUP, S5_STATE).transpose(1, 2, 3, 5, 0, 4)
        fs = jnp.stack([arrange(ca_r[powers, d]), arrange(-ca_i[powers, d])], axis=2)
        f = (fs[:, :, :, :, :, :, None, None, :] * eye_j[None, :, None, None, None, None, :, None, None]
             * eye_e[None, None, None, :, None, None, None, :, None])
        return f.reshape(nt, S5_K, S5_K)

    ef, eb = carry_in(0, S5_T - 1 - tt), carry_in(1, tt)
    ff, fb = read_out(0, tt + 1), read_out(1, S5_T - tt)
    lanes = lambda t: t.reshape(nt, S5_K // 2)
    a_r = jnp.concatenate([lanes(pr[S5_T, 0]), lanes(pr[S5_T, 1])], axis=-1)
    a_i = jnp.concatenate([lanes(pi[S5_T, 0]), lanes(pi[S5_T, 1])], axis=-1)
    dsk = jnp.pad(d_skip.astype(F32).reshape(nt, S5_LANES), ((0, 0), (0, S5_K - S5_LANES)))
    aux = jnp.stack([a_r, a_i, dsk] + [jnp.zeros_like(a_r)] * 5, axis=1)
    return tuple(t.astype(BF16) for t in (ef, eb, m, ff, fb)) + (aux,)


def _s5_kernel(a_ref, ef_ref, eb_ref, m_ref, ff_ref, fb_ref, aux_ref, y_ref, u_scr, wf_scr, wb_scr):
    for b in range(BATCH):
        for t in range(S5_T):
            u_scr.at[t][pl.ds(S5_HB * BATCH + b, S5_XB, stride=BATCH), :] = (
                a_ref[pl.ds(b * SEQ + t, S5_XB, stride=S5_T), :])
            u_scr.at[t][pl.ds(b, S5_HB, stride=BATCH), :] = a_ref[pl.ds(NX + b * CTX_LEN + t, S5_HB, stride=S5_T), :]

    def get_tiles(scr, r0):
        return jnp.concatenate([scr.at[t][pl.ds(r0, S5_RC), :] for t in range(S5_T)], axis=1).astype(BF16)

    def put_tiles(scr, r0, val):
        for t in range(S5_T):
            scr.at[t][pl.ds(r0, S5_RC), :] = val[:, t * S5_LANES:(t + 1) * S5_LANES]

    def block_rows(rc):
        r0 = pl.multiple_of(rc * S5_RC, S5_RC)
        return r0, get_tiles(u_scr, r0)

    def carry_in(rc, _):
        r0, ub = block_rows(rc)
        put_tiles(wf_scr, r0, jnp.dot(ub, ef_ref[...], preferred_element_type=F32))
        put_tiles(wb_scr, r0, jnp.dot(ub, eb_ref[...], preferred_element_type=F32))
        return 0

    lax.fori_loop(0, S5_ROWS // S5_RC, carry_in, 0)

    aux = aux_ref[...]
    npair = S5_LG // 2
    shape = (BATCH, S5_LANES)
    coef = []
    for d in range(2):
        for j in range(npair):
            lo = d * (S5_K // 2) + j * S5_LANES
            coef.append((jnp.broadcast_to(aux[0:1, lo:lo + S5_LANES], shape),
                         jnp.broadcast_to(aux[1:2, lo:lo + S5_LANES], shape)))

    def scan(s, carry):
        rf = pl.multiple_of(s * BATCH, BATCH)
        gb = jnp.where(s < S5_HB, S5_HB - 1 - s, S5_NB + S5_HB - 1 - s)
        rb = pl.multiple_of(gb * BATCH, BATCH)
        out = []
        for d, (w_scr, r) in enumerate(((wf_scr, rf), (wb_scr, rb))):
            for j in range(npair):
                xr, xi = carry[d * npair + j]
                ar, ai = coef[d * npair + j]
                w_re, w_im = w_scr.at[2 * j], w_scr.at[2 * j + 1]
                wr = w_re[pl.ds(r, BATCH), :]
                wi = w_im[pl.ds(r, BATCH), :]
                w_re[pl.ds(r, BATCH), :] = xr
                w_im[pl.ds(r, BATCH), :] = xi
                out.append((ar * xr - ai * xi + wr, ar * xi + ai * xr + wi))
        return tuple(out)

    z = jnp.zeros(shape, F32)
    lax.fori_loop(0, S5_NB, scan, tuple((z, z) for _ in range(2 * npair)))

    def read_out(rc, _):
        r0, ub = block_rows(rc)
        y = jnp.dot(ub, m_ref[...], preferred_element_type=F32)
        y = y + jnp.dot(get_tiles(wf_scr, r0), ff_ref[...], preferred_element_type=F32)
        y = y + jnp.dot(get_tiles(wb_scr, r0), fb_ref[...], preferred_element_type=F32)
        put_tiles(u_scr, r0, y)
        return 0

    lax.fori_loop(0, S5_ROWS // S5_RC, read_out, 0)

    for b in range(BATCH):
        for t in range(S5_T):
            y_ref[pl.ds(b * SEQ + t, S5_XB, stride=S5_T), :] = (
                u_scr.at[t][pl.ds(S5_HB * BATCH + b, S5_XB, stride=BATCH), :])
            y_ref[pl.ds(NX + b * CTX_LEN + t, S5_HB, stride=S5_T), :] = u_scr.at[t][pl.ds(b, S5_HB, stride=BATCH), :]
    y_ref[...] = y_ref[...] + a_ref[...] * aux[2:3, 0:S5_LANES]


def _s5_mix(a, ef, eb, m, ff, fb, aux):
    wspec = lambda: pl.BlockSpec((None, S5_K, S5_K), lambda g: (g, 0, 0), pipeline_mode=pl.Buffered(1))
    return pl.pallas_call(
        _s5_kernel,
        out_shape=jax.ShapeDtypeStruct((NT, S5_WIDTH), F32),
        grid=(S5_TILES,),
        in_specs=[
            pl.BlockSpec((NT, S5_LANES), lambda g: (0, COL_S5 // S5_LANES + g)),
            wspec(), wspec(), wspec(), wspec(), wspec(),
            pl.BlockSpec((None, 8, S5_K), lambda g: (g, 0, 0)),
        ],
        out_specs=pl.BlockSpec((NT, S5_LANES), lambda g: (0, g)),
        scratch_shapes=[pltpu.VMEM((S5_T, S5_ROWS, S5_LANES), F32)] * 3,
        compiler_params=_cparams(("arbitrary",), VMEM_LIMIT_BIG),
        name="s5_scan",
    )(a, ef, eb, m, ff, fb, aux)


def _glu_kernel(y_ref, w_ref, b_ref, o_ref):
    g = _gelu_tanh(y_ref[...]).astype(BF16)
    z = jnp.dot(g, w_ref[...], preferred_element_type=F32) + b_ref[...]
    o_ref[...] = z[:, :S5_WIDTH] * _sigmoid(z[:, S5_WIDTH:])


def _s5_glu(y, w_glu, b_glu):
    tm = 1024
    return pl.pallas_call(
        _glu_kernel,
        out_shape=jax.ShapeDtypeStruct((NT, S5_WIDTH), F32),
        grid=(NT // tm,),
        in_specs=[
            pl.BlockSpec((tm, S5_WIDTH), lambda i: (i, 0)),
            pl.BlockSpec((S5_WIDTH, 2 * S5_WIDTH), lambda i: (0, 0)),
            pl.BlockSpec((1, 2 * S5_WIDTH), lambda i: (0, 0)),
        ],
        out_specs=pl.BlockSpec((tm, S5_WIDTH), lambda i: (i, 0)),
        compiler_params=_cparams(("arbitrary",)),
        name="s5_glu",
    )(y, w_glu.astype(BF16), b_glu.reshape(1, 2 * S5_WIDTH))


def _chunk_index(d, s):
    is_ctx = s < H_CHUNKS
    fwd = jnp.where(is_ctx, s, s - H_CHUNKS)
    bwd = jnp.where(is_ctx, H_CHUNKS - 1 - s, MIX_STEPS - 1 - s)
    return is_ctx, jnp.where(d == 0, fwd, bwd)


def _token_block(b, d, s):
    is_ctx, c = _chunk_index(d, s)
    return jnp.where(is_ctx, NX // CHUNK + b * H_CHUNKS + c, b * X_CHUNKS + c)


def _rope_block(b, d, s):
    is_ctx, c = _chunk_index(d, s)
    return jnp.where(is_ctx, X_CHUNKS + c, c)


def _dot_nt(a, b):
    return lax.dot_general(a, b, (((1,), (1,)), ((), ())), preferred_element_type=F32)


def _dot_tn(a, b):
    return lax.dot_general(a, b, (((0,), (0,)), ((), ())), preferred_element_type=F32)


def _dot(a, b):
    return jnp.dot(a, b, preferred_element_type=F32)


def _head(ref, h):
    return ref[:, h * HEAD_DIM:(h + 1) * HEAD_DIM]


def _chunk_pos(rev):
    i = lax.broadcasted_iota(jnp.int32, (CHUNK, CHUNK), 0)
    j = lax.broadcasted_iota(jnp.int32, (CHUNK, CHUNK), 1)
    pi = jnp.where(rev, CHUNK - 1 - i, i)
    pj = jnp.where(rev, CHUNK - 1 - j, j)
    return pi, pj


def _ret_kernel(dl_ref, q_ref, k_ref, v_ref, cos_ref, sin_ref, o_ref, r_scr, dec_scr, qd_scr, kd_scr, cd_scr):
    d = pl.program_id(1)
    s = pl.program_id(2)
    rev = d == 1

    @pl.when(s == 0)
    def _():
        pi, pj = _chunk_pos(rev)
        rel = (pi - pj).astype(F32)
        pif = pi.astype(F32)
        for h in range(HEADS):
            lg = _log_sigmoid(dl_ref[h])[0:1, :]
            dec_scr[h] = jnp.where(rel >= 0, jnp.exp(lg * jnp.maximum(rel, 0.0)), 0.0)
            qd_scr[h] = jnp.exp(lg * (pif + 1.0))
            kd_scr[h] = jnp.exp(lg * (CHUNK - 1.0 - pif))
            cd_scr[h] = jnp.exp(jnp.broadcast_to(lg, (8, HEAD_DIM)) * CHUNK)
        r_scr[...] = jnp.zeros_like(r_scr)

    cos = cos_ref[...]
    sin = sin_ref[...]
    scale = HEAD_DIM ** -0.5
    for h in range(HEADS):
        q = _head(q_ref, h)
        k = _head(k_ref, h)
        v = _head(v_ref, h).astype(BF16)
        q = q * cos + pltpu.roll(q, HEAD_DIM // 2, 1) * sin
        k = (k * cos + pltpu.roll(k, HEAD_DIM // 2, 1) * sin) * scale
        r = r_scr[h]
        sm = _dot_nt(q.astype(BF16), k.astype(BF16)) * dec_scr[h]
        o = _dot(sm.astype(BF16), v) + _dot((q * qd_scr[h]).astype(BF16), r.astype(BF16))
        r_scr[h] = cd_scr[h][0:1, :] * r + _dot_tn((k * kd_scr[h]).astype(BF16), v)
        o_ref[:, h * HEAD_DIM:(h + 1) * HEAD_DIM] = o


def _retention(a, rope_cos, rope_sin, decay_logit):
    dl = jnp.broadcast_to(decay_logit.astype(F32)[:, :, None, None], (2, HEADS, 8, HEAD_DIM))
    tok = lambda col: pl.BlockSpec((CHUNK, RET_WIDTH), lambda b, d, s: (_token_block(b, d, s), col))
    rope = pl.BlockSpec((CHUNK, HEAD_DIM), lambda b, d, s: (_rope_block(b, d, s), 0))
    hh = pltpu.VMEM((HEADS, CHUNK, HEAD_DIM), F32)
    return pl.pallas_call(
        _ret_kernel,
        out_shape=jax.ShapeDtypeStruct((2, NT, RET_WIDTH), F32),
        grid=(BATCH, 2, MIX_STEPS),
        in_specs=[
            pl.BlockSpec((None, HEADS, 8, HEAD_DIM), lambda b, d, s: (d, 0, 0, 0)),
            tok(COL_RET // RET_WIDTH), tok(COL_RET // RET_WIDTH + 1), tok(COL_RET // RET_WIDTH + 2),
            rope, rope,
        ],
        out_specs=pl.BlockSpec((None, CHUNK, RET_WIDTH), lambda b, d, s: (d, _token_block(b, d, s), 0)),
        scratch_shapes=[hh, hh, hh, hh, pltpu.VMEM((HEADS, 8, HEAD_DIM), F32)],
        compiler_params=_cparams(("arbitrary", "arbitrary", "arbitrary")),
        name="retention",
    )(dl, a, a, a, rope_cos, rope_sin)


def _split3(x):
    h = x.astype(BF16)
    r = x - h.astype(F32)
    m = r.astype(BF16)
    return h, m, (r - m.astype(F32)).astype(BF16)


def _mlstm_kernel(q_ref, k_ref, v_ref, g_ref, gb_ref, o_ref, c_scr, n_scr, m_scr):
    d = pl.program_id(1)
    s = pl.program_id(2)
    rev = d == 1

    @pl.when(s == 0)
    def _():
        c_scr[...] = jnp.zeros_like(c_scr)
        n_scr[...] = jnp.zeros_like(n_scr)
        m_scr[...] = jnp.zeros_like(m_scr)

    pi, pj = _chunk_pos(rev)
    causal = pj <= pi
    tri = jnp.where(causal, 1.0, 0.0).astype(BF16)
    pre = g_ref[...] + gb_ref[...]
    pre_t = pre.T[0:32, :]
    lf_c = _log_sigmoid(pre)
    lf_r = _log_sigmoid(pre_t)
    b_cols = sum(_dot(tri, part) for part in _split3(lf_c))
    b_rows = sum(_dot_nt(part, tri) for part in _split3(lf_r))
    scale = HEAD_DIM ** -0.5
    neg_inf = -jnp.inf
    for h in range(HEADS):
        ci0, cf0, ci1, cf1 = h, HEADS + h, 2 * HEADS + h, 3 * HEADS + h
        b_col = jnp.where(rev, b_cols[:, cf1:cf1 + 1], b_cols[:, cf0:cf0 + 1])
        i_col = jnp.where(rev, pre[:, ci1:ci1 + 1], pre[:, ci0:ci0 + 1])
        b_row = jnp.where(rev, b_rows[cf1:cf1 + 1, :], b_rows[cf0:cf0 + 1, :])
        i_row = jnp.where(rev, pre_t[ci1:ci1 + 1, :], pre_t[ci0:ci0 + 1, :])
        b_end = jnp.where(rev, b_cols[0:1, cf1:cf1 + 1], b_cols[CHUNK - 1:CHUNK, cf0:cf0 + 1])
        m_prev = m_scr[h][0:1, 0:1]
        q = _head(q_ref, h)
        k = _head(k_ref, h) * scale
        v = _head(v_ref, h).astype(BF16)
        log_w = jnp.where(causal, b_col - b_row + i_row, neg_inf)
        log_a = b_col + m_prev
        m_t = jnp.maximum(log_a, jnp.max(log_w, axis=-1, keepdims=True))
        w = jnp.exp(log_w - m_t)
        a = jnp.exp(log_a - m_t)
        qb = q.astype(BF16)
        sm = _dot_nt(qb, k.astype(BF16)) * w
        c_mem = c_scr[h]
        n_mem = n_scr[h][0:1, :]
        num = _dot(sm.astype(BF16), v) + a * _dot(qb, c_mem.astype(BF16))
        den = jnp.sum(sm, axis=-1, keepdims=True) + a * jnp.sum(q * n_mem, axis=-1, keepdims=True)
        o_ref[:, h * HEAD_DIM:(h + 1) * HEAD_DIM] = num / jnp.maximum(jnp.abs(den), jnp.exp(-m_t))
        log_w_end = b_end - b_col + i_col
        m_new = jnp.maximum(b_end + m_prev, jnp.max(log_w_end, axis=0, keepdims=True))
        a_end = jnp.exp(b_end + m_prev - m_new)
        kw = k * jnp.exp(log_w_end - m_new)
        c_scr[h] = a_end * c_mem + _dot_tn(kw.astype(BF16), v)
        n_scr[h] = jnp.broadcast_to(a_end * n_mem + jnp.sum(kw, axis=0, keepdims=True), (8, HEAD_DIM))
        m_scr[h] = jnp.broadcast_to(m_new, (8, HEAD_DIM))


def _mlstm(a, igate_b, fgate_b):
    gb = jnp.stack([igate_b.astype(F32), fgate_b.astype(F32)], axis=1).reshape(1, 4 * HEADS)
    gb = jnp.pad(gb, ((0, 0), (0, HEAD_DIM - 4 * HEADS)))
    col0 = COL_ML // MLSTM_WIDTH
    tok = lambda col: pl.BlockSpec((CHUNK, MLSTM_WIDTH), lambda b, d, s: (_token_block(b, d, s), col))
    small = pltpu.VMEM((HEADS, 8, HEAD_DIM), F32)
    return pl.pallas_call(
        _mlstm_kernel,
        out_shape=jax.ShapeDtypeStruct((2, NT, MLSTM_WIDTH), F32),
        grid=(BATCH, 2, MIX_STEPS),
        in_specs=[
            tok(col0), tok(col0 + 1), tok(col0 + 2),
            pl.BlockSpec((CHUNK, HEAD_DIM), lambda b, d, s: (_token_block(b, d, s), COL_GATE // HEAD_DIM)),
            pl.BlockSpec((1, HEAD_DIM), lambda b, d, s: (0, 0)),
        ],
        out_specs=pl.BlockSpec((None, CHUNK, MLSTM_WIDTH), lambda b, d, s: (d, _token_block(b, d, s), 0)),
        scratch_shapes=[pltpu.VMEM((HEADS, CHUNK, HEAD_DIM), F32), small, small],
        compiler_params=_cparams(("arbitrary", "arbitrary", "arbitrary")),
        name="mlstm",
    )(a, a, a, a, gb)


def _head_norm(o, center):
    if center:
        o = o - jnp.mean(o, axis=-1, keepdims=True)
    return o * lax.rsqrt(jnp.mean(o * o, axis=-1, keepdims=True) + EPS)


def _out_proj_kernel(s5_ref, ro_ref, rg_ref, rw_ref, mo_ref, mg_ref, mw_ref, x_ref, gate_ref, w_ref, o_ref):
    pair = 2 * HEAD_DIM
    acc = jnp.dot(s5_ref[:, 0:pair].astype(BF16), w_ref[0:pair, :], preferred_element_type=F32)
    acc += jnp.dot(s5_ref[:, pair:2 * pair].astype(BF16), w_ref[pair:2 * pair, :], preferred_element_type=F32)
    for hp in range(HEADS // 2):
        rs, ms = [], []
        for h in (2 * hp, 2 * hp + 1):
            sl = slice(h * HEAD_DIM, (h + 1) * HEAD_DIM)
            r = _head_norm(ro_ref[0, :, sl] + ro_ref[1, :, sl], True) * rw_ref[:, sl] * _silu(rg_ref[:, sl])
            m = _head_norm(mo_ref[0, :, sl] + mo_ref[1, :, sl], False) * mw_ref[:, sl] * _sigmoid(mg_ref[:, sl])
            rs.append(r.astype(BF16))
            ms.append(m.astype(BF16))
        r0 = S5_WIDTH + hp * pair
        m0 = S5_WIDTH + RET_WIDTH + hp * pair
        acc += jnp.dot(jnp.concatenate(rs, axis=1), w_ref[r0:r0 + pair, :], preferred_element_type=F32)
        acc += jnp.dot(jnp.concatenate(ms, axis=1), w_ref[m0:m0 + pair, :], preferred_element_type=F32)
    o_ref[...] = x_ref[...] + gate_ref[...] * acc


def _out_proj(rows, s5y, ret_o, ml_o, a, ret_norm_w, mlstm_norm_w, xs, mod3, w_out):
    tm = 256
    return pl.pallas_call(
        _out_proj_kernel,
        out_shape=jax.ShapeDtypeStruct((rows, D_MODEL), F32),
        grid=(rows // tm,),
        in_specs=[
            pl.BlockSpec((tm, S5_WIDTH), lambda i: (i, 0)),
            pl.BlockSpec((2, tm, RET_WIDTH), lambda i: (0, i, 0)),
            pl.BlockSpec((tm, RET_WIDTH), lambda i: (i, COL_RET // RET_WIDTH + 3)),
            pl.BlockSpec((1, RET_WIDTH), lambda i: (0, 0)),
            pl.BlockSpec((2, tm, MLSTM_WIDTH), lambda i: (0, i, 0)),
            pl.BlockSpec((tm, MLSTM_WIDTH), lambda i: (i, COL_ML // MLSTM_WIDTH + 3)),
            pl.BlockSpec((1, MLSTM_WIDTH), lambda i: (0, 0)),
            pl.BlockSpec((tm, D_MODEL), lambda i: (i, 0)),
            pl.BlockSpec((None, 1, D_MODEL), lambda i: (_row_group(i, tm), 0, 2)),
            pl.BlockSpec((D_MODEL, D_MODEL), lambda i: (0, 0), pipeline_mode=pl.Buffered(1)),
        ],
        out_specs=pl.BlockSpec((tm, D_MODEL), lambda i: (i, 0)),
        compiler_params=_cparams(("arbitrary",)),
        name="out_proj",
    )(s5y, ret_o, a, ret_norm_w.reshape(1, RET_WIDTH), ml_o, a, mlstm_norm_w.reshape(1, MLSTM_WIDTH),
      xs, mod3, w_out)


def _mlp_kernel(x_ref, nw_ref, shift_ref, scale_ref, gate_ref, w1_ref, w2_ref, nf_ref, o_ref, lhs_scr,
                *, final_norm):
    f = pl.program_id(1)

    @pl.when(f == 0)
    def _():
        lhs_scr[...] = _rms_modulate(x_ref[...], nw_ref[...], shift_ref[...], scale_ref[...]).astype(BF16)
        o_ref[...] = jnp.zeros_like(o_ref)

    hid = jnp.dot(lhs_scr[...], w1_ref[...].astype(BF16), preferred_element_type=F32)
    hid = jnp.square(jnp.maximum(hid, 0.0)).astype(BF16)
    o_ref[...] += jnp.dot(hid, w2_ref[...].astype(BF16), preferred_element_type=F32)

    @pl.when(f == pl.num_programs(1) - 1)
    def _():
        y = x_ref[...] + gate_ref[...] * o_ref[...]
        if final_norm:
            y = y * lax.rsqrt(jnp.mean(y * y, axis=-1, keepdims=True) + EPS) * nf_ref[...]
        o_ref[...] = y


def _mlp(rows, xs, norm_w, mod3, w1, w2, norm_f_w, final_norm):
    tm, tf = 1024, 512
    modspec = lambda k: pl.BlockSpec((None, 1, D_MODEL), lambda i, f: (_row_group(i, tm), 0, k))
    once = pl.Buffered(1)
    return pl.pallas_call(
        functools.partial(_mlp_kernel, final_norm=final_norm),
        out_shape=jax.ShapeDtypeStruct((rows, D_MODEL), F32),
        grid=(rows // tm, D_FF // tf),
        in_specs=[
            pl.BlockSpec((tm, D_MODEL), lambda i, f: (i, 0), pipeline_mode=once),
            pl.BlockSpec((1, D_MODEL), lambda i, f: (0, 0)),
            modspec(3), modspec(4), modspec(5),
            pl.BlockSpec((D_MODEL, tf), lambda i, f: (0, f)),
            pl.BlockSpec((tf, D_MODEL), lambda i, f: (f, 0)),
            pl.BlockSpec((1, D_MODEL), lambda i, f: (0, 0)),
        ],
        out_specs=pl.BlockSpec((tm, D_MODEL), lambda i, f: (i, 0), pipeline_mode=once),
        scratch_shapes=[pltpu.VMEM((tm, D_MODEL), BF16)],
        compiler_params=_cparams(("arbitrary", "arbitrary"), VMEM_LIMIT_BIG),
        name="mlp",
    )(xs, norm_w.reshape(1, D_MODEL), mod3, mod3, mod3, w1, w2, norm_f_w.reshape(1, D_MODEL))


def _rope_tables():
    quarter = HEAD_DIM // 4
    rows = jnp.repeat(jnp.arange(SEQ // GRID_W, dtype=F32), GRID_W)
    cols = jnp.tile(jnp.arange(GRID_W, dtype=F32), SEQ // GRID_W)
    inv = ROPE_BASE ** (-jnp.arange(quarter, dtype=F32) / quarter)
    ang = jnp.concatenate([rows[:, None] * inv, cols[:, None] * inv], axis=-1)
    cos, sin = jnp.cos(ang), jnp.sin(ang)
    cos2 = jnp.concatenate([cos, cos], axis=-1)
    sin2 = jnp.concatenate([-sin, sin], axis=-1)
    cos2 = jnp.concatenate([cos2, jnp.ones((CTX_LEN, HEAD_DIM), F32)], axis=0)
    sin2 = jnp.concatenate([sin2, jnp.zeros((CTX_LEN, HEAD_DIM), F32)], axis=0)
    return cos2, sin2


def _permute_w_in(w):
    s5 = w[:, :S5_WIDTH]
    ret = w[:, S5_WIDTH:S5_WIDTH + 4 * RET_WIDTH]
    ml = w[:, S5_WIDTH + 4 * RET_WIDTH:S5_WIDTH + 4 * RET_WIDTH + 4 * MLSTM_WIDTH]
    gates = w[:, S5_WIDTH + 4 * RET_WIDTH + 4 * MLSTM_WIDTH:]
    pad = jnp.zeros((D_MODEL, IN_PAD - IN_WIDTH), w.dtype)
    return jnp.concatenate([ret, ml, s5, gates, pad], axis=1).astype(BF16)


def kernel(x, c, ctx, c_ctx, w_mod, b_mod, norm1_w, norm2_w, w_in, w_out, s5_lam_re, s5_lam_im, s5_log_step, s5_b_re, s5_b_im, s5_c_re, s5_c_im, s5_d, s5_w_glu, s5_b_glu, ret_decay_logit, ret_norm_w, mlstm_igate_b, mlstm_fgate_b, mlstm_norm_w, w_ff1, w_ff2, norm_f_w):
    rope_cos, rope_sin = _rope_tables()
    cvec = jnp.concatenate([c, c_ctx[None, :], jnp.zeros((MOD_ROWS - BATCH - 1, D_MODEL), F32)], axis=0)
    mod = _modulation(cvec, w_mod, b_mod)
    xs = jnp.concatenate([x.reshape(NX, D_MODEL), ctx.reshape(NH, D_MODEL)], axis=0)
    for l in range(DEPTH):
        last = l == DEPTH - 1
        mod3 = mod[l].reshape(MOD_ROWS, 1, 6 * D_MODEL)
        a = _in_proj(xs, norm1_w[l], mod3, _permute_w_in(w_in[l]))
        s5w = _s5_prep(s5_lam_re[l], s5_lam_im[l], s5_log_step[l], s5_b_re[l], s5_b_im[l],
                       s5_c_re[l], s5_c_im[l], s5_d[l])
        s5y = _s5_glu(_s5_mix(a, *s5w), s5_w_glu[l], s5_b_glu[l])
        ret_o = _retention(a, rope_cos, rope_sin, ret_decay_logit[l])
        ml_o = _mlstm(a, mlstm_igate_b[l], mlstm_fgate_b[l])
        rows = NX if last else NT
        xs = _out_proj(rows, s5y, ret_o, ml_o, a, ret_norm_w[l], mlstm_norm_w[l], xs, mod3,
                       w_out[l].astype(BF16))
        xs = _mlp(rows, xs, norm2_w[l], mod3, w_ff1[l], w_ff2[l], norm_f_w, last)
    return xs.reshape(BATCH, SEQ, D_MODEL)
```

```python
import functools
import math

import jax
import jax.numpy as jnp
from jax import lax
from jax.experimental import pallas as pl
from jax.experimental.pallas import tpu as pltpu

F32 = jnp.float32
BF16 = jnp.bfloat16

D_MODEL = 2048
BATCH = 4
SEQ = 2048
DEPTH = 2
CTX_LEN = 256
GRID_W = 64
HEAD_DIM = 128
S5_WIDTH = 512
S5_GROUP = 16
S5_GROUPS = 32
S5_STATE = 64
RET_WIDTH = 768
MLSTM_WIDTH = 768
HEADS = 6
IN_WIDTH = S5_WIDTH + 4 * RET_WIDTH + 4 * MLSTM_WIDTH + 4 * HEADS
D_FF = 4 * D_MODEL
CHUNK = 128
ROPE_BASE = 10000.0
EPS = 1e-6

NX = BATCH * SEQ
NH = BATCH * CTX_LEN
NT = NX + NH
MOD_ROWS = 8
CTX_GROUP = BATCH

COL_RET = 0
COL_ML = 4 * RET_WIDTH
COL_S5 = COL_ML + 4 * MLSTM_WIDTH
COL_GATE = COL_S5 + S5_WIDTH
IN_PAD = 7168

X_CHUNKS = SEQ // CHUNK
H_CHUNKS = CTX_LEN // CHUNK
MIX_STEPS = X_CHUNKS + H_CHUNKS

S5_T = 8
S5_LANES = 128
S5_LG = S5_LANES // S5_GROUP
S5_TILES = S5_WIDTH // S5_LANES
S5_K = S5_T * S5_LANES
S5_XB = SEQ // S5_T
S5_HB = CTX_LEN // S5_T
S5_NB = S5_XB + S5_HB
S5_ROWS = S5_NB * BATCH
S5_RC = 128

VMEM_LIMIT = 48 * 1024 * 1024
VMEM_LIMIT_BIG = 56 * 1024 * 1024


def _cparams(sem, limit=VMEM_LIMIT):
    return pltpu.CompilerParams(dimension_semantics=sem, vmem_limit_bytes=limit)


def _sigmoid(x):
    return 1.0 / (1.0 + jnp.exp(-x))


def _silu(x):
    return x * _sigmoid(x)


def _log_sigmoid(x):
    return jnp.minimum(x, 0.0) - jnp.log1p(jnp.exp(-jnp.abs(x)))


def _gelu_tanh(x):
    c = math.sqrt(2.0 / math.pi)
    return 0.5 * x * (1.0 + jnp.tanh(c * (x + 0.044715 * (x * x * x))))


def _row_group(i, tm):
    return jnp.minimum((i * tm) // SEQ, CTX_GROUP)


def _mod_kernel(c_ref, w_ref, b_ref, o_ref):
    c = c_ref[...]
    lhs = _silu(c).astype(BF16)
    o_ref[...] = jnp.dot(lhs, w_ref[...].astype(BF16), preferred_element_type=F32) + b_ref[...]


def _modulation(cvec, w_mod, b_mod):
    tn = 512
    n = 6 * D_MODEL
    return pl.pallas_call(
        _mod_kernel,
        out_shape=jax.ShapeDtypeStruct((DEPTH, MOD_ROWS, n), F32),
        grid=(DEPTH, n // tn),
        in_specs=[
            pl.BlockSpec((MOD_ROWS, D_MODEL), lambda l, j: (0, 0)),
            pl.BlockSpec((None, D_MODEL, tn), lambda l, j: (l, 0, j)),
            pl.BlockSpec((None, 1, tn), lambda l, j: (l, 0, j)),
        ],
        out_specs=pl.BlockSpec((None, MOD_ROWS, tn), lambda l, j: (l, 0, j)),
        compiler_params=_cparams(("arbitrary", "arbitrary")),
        name="modulation",
    )(cvec, w_mod, b_mod.reshape(DEPTH, 1, n))


def _rms_modulate(x, nw, shift, scale):
    y = x * lax.rsqrt(jnp.mean(x * x, axis=-1, keepdims=True) + EPS) * nw
    return y * (1.0 + scale) + shift


def _in_proj_kernel(x_ref, nw_ref, shift_ref, scale_ref, w_ref, o_ref, lhs_scr):
    @pl.when(pl.program_id(1) == 0)
    def _():
        lhs_scr[...] = _rms_modulate(x_ref[...], nw_ref[...], shift_ref[...], scale_ref[...]).astype(BF16)

    o_ref[...] = jnp.dot(lhs_scr[...], w_ref[...], preferred_element_type=F32)


def _in_proj(xs, norm_w, mod3, w_in_p):
    tm, tn = 1024, 1024
    return pl.pallas_call(
        _in_proj_kernel,
        out_shape=jax.ShapeDtypeStruct((NT, IN_PAD), F32),
        grid=(NT // tm, IN_PAD // tn),
        in_specs=[
            pl.BlockSpec((tm, D_MODEL), lambda i, j: (i, 0)),
            pl.BlockSpec((1, D_MODEL), lambda i, j: (0, 0)),
            pl.BlockSpec((None, 1, D_MODEL), lambda i, j: (_row_group(i, tm), 0, 0)),
            pl.BlockSpec((None, 1, D_MODEL), lambda i, j: (_row_group(i, tm), 0, 1)),
            pl.BlockSpec((D_MODEL, tn), lambda i, j: (0, j)),
        ],
        out_specs=pl.BlockSpec((tm, tn), lambda i, j: (i, j)),
        scratch_shapes=[pltpu.VMEM((tm, D_MODEL), BF16)],
        compiler_params=_cparams(("arbitrary", "arbitrary")),
        name="in_proj",
    )(xs, norm_w.reshape(1, D_MODEL), mod3, mod3, w_in_p)


def _s5_prep(lam_re, lam_im, log_step, b_re, b_im, c_re, c_im, d_skip):
    hp = lax.Precision.HIGHEST
    lam_re = jnp.minimum(lam_re.astype(F32), -1e-4)
    lam_im = lam_im.astype(F32)
    step = jnp.exp(log_step.astype(F32))[..., None]
    mag = jnp.exp(lam_re * step)
    ab_re, ab_im = mag * jnp.cos(lam_im * step), mag * jnp.sin(lam_im * step)
    den = lam_re * lam_re + lam_im * lam_im
    ir, ii = lam_re / den, -lam_im / den
    nr = (ab_re - 1.0) * ir - ab_im * ii
    ni = (ab_re - 1.0) * ii + ab_im * ir
    bre, bim = b_re.astype(F32), b_im.astype(F32)
    bb_re = nr[..., None] * bre - ni[..., None] * bim
    bb_im = nr[..., None] * bim + ni[..., None] * bre
    lpow = jnp.arange(S5_T + 1, dtype=F32)[:, None, None, None]
    pmag = jnp.exp(lpow * (lam_re * step))
    pr, pi = pmag * jnp.cos(lpow * (lam_im * step)), pmag * jnp.sin(lpow * (lam_im * step))
    cre, cim = c_re.astype(F32), c_im.astype(F32)
    ca_r = cre[None] * pr[:, :, :, None, :] - cim[None] * pi[:, :, :, None, :]
    ca_i = cre[None] * pi[:, :, :, None, :] + cim[None] * pr[:, :, :, None, :]
    kl = (jnp.einsum('ldgnp,dgpm->ldgnm', ca_r, bb_re, precision=hp)
          - jnp.einsum('ldgnp,dgpm->ldgnm', ca_i, bb_im, precision=hp))
    tt = jnp.arange(S5_T)
    lag = tt[None, :] - tt[:, None]
    kf = jnp.where((lag >= 0)[:, :, None, None, None], kl[jnp.clip(lag, 0, S5_T), 0], 0.0)
    kb = jnp.where((lag <= 0)[:, :, None, None, None], kl[jnp.clip(-lag, 0, S5_T), 1], 0.0)
    nt = S5_TILES
    row = jnp.arange(S5_K)
    col = jnp.arange(S5_K)
    lane = jnp.arange(S5_LANES)
    g_blk = (row // S5_GROUP) % S5_LG
    g_state = 2 * (row // (4 * S5_STATE)) + (row // S5_STATE) % 2
    to_blk = ((lane[:, None] // S5_GROUP == col[None, :] // S5_LANES)
              & (lane[:, None] % S5_GROUP == col[None, :] % S5_GROUP)).astype(BF16)
    to_state = ((lane[:, None] // S5_STATE == (col[None, :] // S5_LANES) % 2)
                & (lane[:, None] % S5_STATE == col[None, :] % S5_STATE)).astype(BF16)

    def expand(compact, spread, g_row, g_col):
        wide = jnp.einsum('trk,kc->trc', compact.reshape(nt, S5_K, S5_LANES).astype(BF16), spread,
                          preferred_element_type=F32)
        return jnp.where(g_row[:, None] == g_col[None, :], wide, 0.0).astype(BF16)

    mc = (kf + kb).reshape(S5_T, S5_T, nt, S5_LG, S5_GROUP, S5_GROUP).transpose(2, 0, 3, 5, 1, 4)
    m = expand(mc, to_blk, g_blk, g_blk)

    def carry_in(d, powers):
        er = pr[powers, d][..., None] * bb_re[d][None] - pi[powers, d][..., None] * bb_im[d][None]
        ei = pr[powers, d][..., None] * bb_im[d][None] + pi[powers, d][..., None] * bb_re[d][None]
        arrange = lambda t: t.reshape(S5_T, nt, S5_LG, S5_STATE, S5_GROUP).transpose(1, 0, 2, 4, 3)
        es = jnp.stack([arrange(er), arrange(ei)], axis=4)
        return expand(es, to_state, g_blk, g_state)

    def read_out(d, powers):
        arrange = lambda t: t.reshape(S5_T, nt, S5_LG // 2, 2, S5_GROUP, S5_STATE).transpose(1, 2, 3, 5, 0, 4)
        fs = jnp.stack([arrange(ca_r[powers, d]), arrange(-ca_i[powers, d])], axis=2)
        return expand(fs, to_blk, g_state, g_blk)

    ef, eb = carry_in(0, S5_T - 1 - tt), carry_in(1, tt)
    ff, fb = read_out(0, tt + 1), read_out(1, S5_T - tt)
    lanes = lambda t: t.reshape(nt, S5_K // 2)
    a_r = jnp.concatenate([lanes(pr[S5_T, 0]), lanes(pr[S5_T, 1])], axis=-1)
    a_i = jnp.concatenate([lanes(pi[S5_T, 0]), lanes(pi[S5_T, 1])], axis=-1)
    dsk = jnp.pad(d_skip.astype(F32).reshape(nt, S5_LANES), ((0, 0), (0, S5_K - S5_LANES)))
    aux = jnp.stack([a_r, a_i, dsk] + [jnp.zeros_like(a_r)] * 5, axis=1)
    return ef, eb, m, ff, fb, aux


def _s5_kernel(a_ref, ef_ref, eb_ref, m_ref, ff_ref, fb_ref, aux_ref, y_ref, u_scr, wf_scr, wb_scr):
    for b in range(BATCH):
        for t in range(S5_T):
            u_scr.at[t][pl.ds(S5_HB * BATCH + b, S5_XB, stride=BATCH), :] = (
                a_ref[pl.ds(b * SEQ + t, S5_XB, stride=S5_T), :])
            u_scr.at[t][pl.ds(b, S5_HB, stride=BATCH), :] = a_ref[pl.ds(NX + b * CTX_LEN + t, S5_HB, stride=S5_T), :]

    def get_tiles(scr, r0):
        return jnp.concatenate([scr.at[t][pl.ds(r0, S5_RC), :] for t in range(S5_T)], axis=1).astype(BF16)

    def put_tiles(scr, r0, val):
        for t in range(S5_T):
            scr.at[t][pl.ds(r0, S5_RC), :] = val[:, t * S5_LANES:(t + 1) * S5_LANES]

    def block_rows(rc):
        r0 = pl.multiple_of(rc * S5_RC, S5_RC)
        return r0, get_tiles(u_scr, r0)

    def carry_in(rc, _):
        r0, ub = block_rows(rc)
        put_tiles(wf_scr, r0, jnp.dot(ub, ef_ref[...], preferred_element_type=F32))
        put_tiles(wb_scr, r0, jnp.dot(ub, eb_ref[...], preferred_element_type=F32))
        return 0

    lax.fori_loop(0, S5_ROWS // S5_RC, carry_in, 0)

    aux = aux_ref[...]
    npair = S5_LG // 2
    shape = (BATCH, S5_LANES)
    coef = []
    for d in range(2):
        for j in range(npair):
            lo = d * (S5_K // 2) + j * S5_LANES
            coef.append((jnp.broadcast_to(aux[0:1, lo:lo + S5_LANES], shape),
                         jnp.broadcast_to(aux[1:2, lo:lo + S5_LANES], shape)))

    def scan(s, carry):
        rf = pl.multiple_of(s * BATCH, BATCH)
        gb = jnp.where(s < S5_HB, S5_HB - 1 - s, S5_NB + S5_HB - 1 - s)
        rb = pl.multiple_of(gb * BATCH, BATCH)
        out = []
        for d, (w_scr, r) in enumerate(((wf_scr, rf), (wb_scr, rb))):
            for j in range(npair):
                xr, xi = carry[d * npair + j]
                ar, ai = coef[d * npair + j]
                w_re, w_im = w_scr.at[2 * j], w_scr.at[2 * j + 1]
                wr = w_re[pl.ds(r, BATCH), :]
                wi = w_im[pl.ds(r, BATCH), :]
                w_re[pl.ds(r, BATCH), :] = xr
                w_im[pl.ds(r, BATCH), :] = xi
                out.append((ar * xr - ai * xi + wr, ar * xi + ai * xr + wi))
        return tuple(out)

    z = jnp.zeros(shape, F32)
    lax.fori_loop(0, S5_NB, scan, tuple((z, z) for _ in range(2 * npair)))

    def read_out(rc, _):
        r0, ub = block_rows(rc)
        y = jnp.dot(ub, m_ref[...], preferred_element_type=F32)
        y = y + jnp.dot(get_tiles(wf_scr, r0), ff_ref[...], preferred_element_type=F32)
        y = y + jnp.dot(get_tiles(wb_scr, r0), fb_ref[...], preferred_element_type=F32)
        put_tiles(u_scr, r0, y)
        return 0

    lax.fori_loop(0, S5_ROWS // S5_RC, read_out, 0)

    for b in range(BATCH):
        for t in range(S5_T):
            y_ref[pl.ds(b * SEQ + t, S5_XB, stride=S5_T), :] = (
                u_scr.at[t][pl.ds(S5_HB * BATCH + b, S5_XB, stride=BATCH), :])
            y_ref[pl.ds(NX + b * CTX_LEN + t, S5_HB, stride=S5_T), :] = u_scr.at[t][pl.ds(b, S5_HB, stride=BATCH), :]
    y_ref[...] = y_ref[...] + a_ref[...] * aux[2:3, 0:S5_LANES]


def _s5_mix(a, ef, eb, m, ff, fb, aux):
    wspec = lambda: pl.BlockSpec((None, S5_K, S5_K), lambda g: (g, 0, 0), pipeline_mode=pl.Buffered(1))
    return pl.pallas_call(
        _s5_kernel,
        out_shape=jax.ShapeDtypeStruct((NT, S5_WIDTH), F32),
        grid=(S5_TILES,),
        in_specs=[
            pl.BlockSpec((NT, S5_LANES), lambda g: (0, COL_S5 // S5_LANES + g)),
            wspec(), wspec(), wspec(), wspec(), wspec(),
            pl.BlockSpec((None, 8, S5_K), lambda g: (g, 0, 0)),
        ],
        out_specs=pl.BlockSpec((NT, S5_LANES), lambda g: (0, g)),
        scratch_shapes=[pltpu.VMEM((S5_T, S5_ROWS, S5_LANES), F32)] * 3,
        compiler_params=_cparams(("arbitrary",), VMEM_LIMIT_BIG),
        name="s5_scan",
    )(a, ef, eb, m, ff, fb, aux)


def _glu_kernel(y_ref, w_ref, b_ref, o_ref):
    g = _gelu_tanh(y_ref[...]).astype(BF16)
    z = jnp.dot(g, w_ref[...], preferred_element_type=F32) + b_ref[...]
    o_ref[...] = z[:, :S5_WIDTH] * _sigmoid(z[:, S5_WIDTH:])


def _s5_glu(y, w_glu, b_glu):
    tm = 1024
    return pl.pallas_call(
        _glu_kernel,
        out_shape=jax.ShapeDtypeStruct((NT, S5_WIDTH), F32),
        grid=(NT // tm,),
        in_specs=[
            pl.BlockSpec((tm, S5_WIDTH), lambda i: (i, 0)),
            pl.BlockSpec((S5_WIDTH, 2 * S5_WIDTH), lambda i: (0, 0)),
            pl.BlockSpec((1, 2 * S5_WIDTH), lambda i: (0, 0)),
        ],
        out_specs=pl.BlockSpec((tm, S5_WIDTH), lambda i: (i, 0)),
        compiler_params=_cparams(("arbitrary",)),
        name="s5_glu",
    )(y, w_glu.astype(BF16), b_glu.reshape(1, 2 * S5_WIDTH))


def _chunk_index(d, s):
    is_ctx = s < H_CHUNKS
    fwd = jnp.where(is_ctx, s, s - H_CHUNKS)
    bwd = jnp.where(is_ctx, H_CHUNKS - 1 - s, MIX_STEPS - 1 - s)
    return is_ctx, jnp.where(d == 0, fwd, bwd)


def _token_block(b, d, s):
    is_ctx, c = _chunk_index(d, s)
    return jnp.where(is_ctx, NX // CHUNK + b * H_CHUNKS + c, b * X_CHUNKS + c)


def _rope_block(b, d, s):
    is_ctx, c = _chunk_index(d, s)
    return jnp.where(is_ctx, X_CHUNKS + c, c)


def _dot_nt(a, b):
    return lax.dot_general(a, b, (((1,), (1,)), ((), ())), preferred_element_type=F32)


def _dot_tn(a, b):
    return lax.dot_general(a, b, (((0,), (0,)), ((), ())), preferred_element_type=F32)


def _dot(a, b):
    return jnp.dot(a, b, preferred_element_type=F32)


def _head(ref, h):
    return ref[:, h * HEAD_DIM:(h + 1) * HEAD_DIM]


def _chunk_pos(rev):
    i = lax.broadcasted_iota(jnp.int32, (CHUNK, CHUNK), 0)
    j = lax.broadcasted_iota(jnp.int32, (CHUNK, CHUNK), 1)
    pi = jnp.where(rev, CHUNK - 1 - i, i)
    pj = jnp.where(rev, CHUNK - 1 - j, j)
    return pi, pj


def _ret_kernel(dl_ref, q_ref, k_ref, v_ref, cos_ref, sin_ref, o_ref, r_scr, dec_scr, qd_scr, kd_scr, cd_scr):
    d = pl.program_id(1)
    s = pl.program_id(2)
    rev = d == 1

    @pl.when(s == 0)
    def _():
        pi, pj = _chunk_pos(rev)
        rel = (pi - pj).astype(F32)
        pif = pi.astype(F32)
        for h in range(HEADS):
            lg = _log_sigmoid(dl_ref[h])[0:1, :]
            dec_scr[h] = jnp.where(rel >= 0, jnp.exp(lg * jnp.maximum(rel, 0.0)), 0.0)
            qd_scr[h] = jnp.exp(lg * (pif + 1.0))
            kd_scr[h] = jnp.exp(lg * (CHUNK - 1.0 - pif))
            cd_scr[h] = jnp.exp(jnp.broadcast_to(lg, (8, HEAD_DIM)) * CHUNK)
        r_scr[...] = jnp.zeros_like(r_scr)

    cos = cos_ref[...]
    sin = sin_ref[...]
    scale = HEAD_DIM ** -0.5
    for h in range(HEADS):
        q = _head(q_ref, h)
        k = _head(k_ref, h)
        v = _head(v_ref, h).astype(BF16)
        q = q * cos + pltpu.roll(q, HEAD_DIM // 2, 1) * sin
        k = (k * cos + pltpu.roll(k, HEAD_DIM // 2, 1) * sin) * scale
        r = r_scr[h]
        sm = _dot_nt(q.astype(BF16), k.astype(BF16)) * dec_scr[h]
        o = _dot(sm.astype(BF16), v) + _dot((q * qd_scr[h]).astype(BF16), r.astype(BF16))
        r_scr[h] = cd_scr[h][0:1, :] * r + _dot_tn((k * kd_scr[h]).astype(BF16), v)
        o_ref[:, h * HEAD_DIM:(h + 1) * HEAD_DIM] = o


def _retention(a, rope_cos, rope_sin, decay_logit):
    dl = jnp.broadcast_to(decay_logit.astype(F32)[:, :, None, None], (2, HEADS, 8, HEAD_DIM))
    tok = lambda col: pl.BlockSpec((CHUNK, RET_WIDTH), lambda b, d, s: (_token_block(b, d, s), col))
    rope = pl.BlockSpec((CHUNK, HEAD_DIM), lambda b, d, s: (_rope_block(b, d, s), 0))
    hh = pltpu.VMEM((HEADS, CHUNK, HEAD_DIM), F32)
    return pl.pallas_call(
        _ret_kernel,
        out_shape=jax.ShapeDtypeStruct((2, NT, RET_WIDTH), F32),
        grid=(BATCH, 2, MIX_STEPS),
        in_specs=[
            pl.BlockSpec((None, HEADS, 8, HEAD_DIM), lambda b, d, s: (d, 0, 0, 0)),
            tok(COL_RET // RET_WIDTH), tok(COL_RET // RET_WIDTH + 1), tok(COL_RET // RET_WIDTH + 2),
            rope, rope,
        ],
        out_specs=pl.BlockSpec((None, CHUNK, RET_WIDTH), lambda b, d, s: (d, _token_block(b, d, s), 0)),
        scratch_shapes=[hh, hh, hh, hh, pltpu.VMEM((HEADS, 8, HEAD_DIM), F32)],
        compiler_params=_cparams(("arbitrary", "arbitrary", "arbitrary")),
        name="retention",
    )(dl, a, a, a, rope_cos, rope_sin)


def _split3(x):
    h = x.astype(BF16)
    r = x - h.astype(F32)
    m = r.astype(BF16)
    return h, m, (r - m.astype(F32)).astype(BF16)


def _mlstm_kernel(q_ref, k_ref, v_ref, g_ref, gb_ref, o_ref, c_scr, n_scr, m_scr):
    d = pl.program_id(1)
    s = pl.program_id(2)
    rev = d == 1

    @pl.when(s == 0)
    def _():
        c_scr[...] = jnp.zeros_like(c_scr)
        n_scr[...] = jnp.zeros_like(n_scr)
        m_scr[...] = jnp.zeros_like(m_scr)

    pi, pj = _chunk_pos(rev)
    causal = pj <= pi
    tri = jnp.where(causal, 1.0, 0.0).astype(BF16)
    pre = g_ref[...] + gb_ref[...]
    pre_t = pre.T[0:32, :]
    lf_c = _log_sigmoid(pre)
    lf_r = _log_sigmoid(pre_t)
    b_cols = sum(_dot(tri, part) for part in _split3(lf_c))
    b_rows = sum(_dot_nt(part, tri) for part in _split3(lf_r))
    scale = HEAD_DIM ** -0.5
    neg_inf = -jnp.inf
    for h in range(HEADS):
        ci0, cf0, ci1, cf1 = h, HEADS + h, 2 * HEADS + h, 3 * HEADS + h
        b_col = jnp.where(rev, b_cols[:, cf1:cf1 + 1], b_cols[:, cf0:cf0 + 1])
        i_col = jnp.where(rev, pre[:, ci1:ci1 + 1], pre[:, ci0:ci0 + 1])
        b_row = jnp.where(rev, b_rows[cf1:cf1 + 1, :], b_rows[cf0:cf0 + 1, :])
        i_row = jnp.where(rev, pre_t[ci1:ci1 + 1, :], pre_t[ci0:ci0 + 1, :])
        b_end = jnp.where(rev, b_cols[0:1, cf1:cf1 + 1], b_cols[CHUNK - 1:CHUNK, cf0:cf0 + 1])
        m_prev = m_scr[h][0:1, 0:1]
        q = _head(q_ref, h)
        k = _head(k_ref, h) * scale
        v = _head(v_ref, h).astype(BF16)
        log_w = jnp.where(causal, b_col - b_row + i_row, neg_inf)
        log_a = b_col + m_prev
        m_t = jnp.maximum(log_a, jnp.max(log_w, axis=-1, keepdims=True))
        w = jnp.exp(log_w - m_t)
        a = jnp.exp(log_a - m_t)
        qb = q.astype(BF16)
        sm = _dot_nt(qb, k.astype(BF16)) * w
        c_mem = c_scr[h]
        n_mem = n_scr[h][0:1, :]
        num = _dot(sm.astype(BF16), v) + a * _dot(qb, c_mem.astype(BF16))
        den = jnp.sum(sm, axis=-1, keepdims=True) + a * jnp.sum(q * n_mem, axis=-1, keepdims=True)
        o_ref[:, h * HEAD_DIM:(h + 1) * HEAD_DIM] = num / jnp.maximum(jnp.abs(den), jnp.exp(-m_t))
        log_w_end = b_end - b_col + i_col
        m_new = jnp.maximum(b_end + m_prev, jnp.max(log_w_end, axis=0, keepdims=True))
        a_end = jnp.exp(b_end + m_prev - m_new)
        kw = k * jnp.exp(log_w_end - m_new)
        c_scr[h] = a_end * c_mem + _dot_tn(kw.astype(BF16), v)
        n_scr[h] = jnp.broadcast_to(a_end * n_mem + jnp.sum(kw, axis=0, keepdims=True), (8, HEAD_DIM))
        m_scr[h] = jnp.broadcast_to(m_new, (8, HEAD_DIM))


def _mlstm(a, igate_b, fgate_b):
    gb = jnp.stack([igate_b.astype(F32), fgate_b.astype(F32)], axis=1).reshape(1, 4 * HEADS)
    gb = jnp.pad(gb, ((0, 0), (0, HEAD_DIM - 4 * HEADS)))
    col0 = COL_ML // MLSTM_WIDTH
    tok = lambda col: pl.BlockSpec((CHUNK, MLSTM_WIDTH), lambda b, d, s: (_token_block(b, d, s), col))
    small = pltpu.VMEM((HEADS, 8, HEAD_DIM), F32)
    return pl.pallas_call(
        _mlstm_kernel,
        out_shape=jax.ShapeDtypeStruct((2, NT, MLSTM_WIDTH), F32),
        grid=(BATCH, 2, MIX_STEPS),
        in_specs=[
            tok(col0), tok(col0 + 1), tok(col0 + 2),
            pl.BlockSpec((CHUNK, HEAD_DIM), lambda b, d, s: (_token_block(b, d, s), COL_GATE // HEAD_DIM)),
            pl.BlockSpec((1, HEAD_DIM), lambda b, d, s: (0, 0)),
        ],
        out_specs=pl.BlockSpec((None, CHUNK, MLSTM_WIDTH), lambda b, d, s: (d, _token_block(b, d, s), 0)),
        scratch_shapes=[pltpu.VMEM((HEADS, CHUNK, HEAD_DIM), F32), small, small],
        compiler_params=_cparams(("arbitrary", "arbitrary", "arbitrary")),
        name="mlstm",
    )(a, a, a, a, gb)


def _head_norm(o, center):
    if center:
        o = o - jnp.mean(o, axis=-1, keepdims=True)
    return o * lax.rsqrt(jnp.mean(o * o, axis=-1, keepdims=True) + EPS)


def _out_proj_kernel(s5_ref, ro_ref, rg_ref, rw_ref, mo_ref, mg_ref, mw_ref, x_ref, gate_ref, w_ref, o_ref):
    pair = 2 * HEAD_DIM
    acc = jnp.dot(s5_ref[:, 0:pair].astype(BF16), w_ref[0:pair, :], preferred_element_type=F32)
    acc += jnp.dot(s5_ref[:, pair:2 * pair].astype(BF16), w_ref[pair:2 * pair, :], preferred_element_type=F32)
    for hp in range(HEADS // 2):
        rs, ms = [], []
        for h in (2 * hp, 2 * hp + 1):
            sl = slice(h * HEAD_DIM, (h + 1) * HEAD_DIM)
            r = _head_norm(ro_ref[0, :, sl] + ro_ref[1, :, sl], True) * rw_ref[:, sl] * _silu(rg_ref[:, sl])
            m = _head_norm(mo_ref[0, :, sl] + mo_ref[1, :, sl], False) * mw_ref[:, sl] * _sigmoid(mg_ref[:, sl])
            rs.append(r.astype(BF16))
            ms.append(m.astype(BF16))
        r0 = S5_WIDTH + hp * pair
        m0 = S5_WIDTH + RET_WIDTH + hp * pair
        acc += jnp.dot(jnp.concatenate(rs, axis=1), w_ref[r0:r0 + pair, :], preferred_element_type=F32)
        acc += jnp.dot(jnp.concatenate(ms, axis=1), w_ref[m0:m0 + pair, :], preferred_element_type=F32)
    o_ref[...] = x_ref[...] + gate_ref[...] * acc


def _out_proj(rows, s5y, ret_o, ml_o, a, ret_norm_w, mlstm_norm_w, xs, mod3, w_out):
    tm = 256
    return pl.pallas_call(
        _out_proj_kernel,
        out_shape=jax.ShapeDtypeStruct((rows, D_MODEL), F32),
        grid=(rows // tm,),
        in_specs=[
            pl.BlockSpec((tm, S5_WIDTH), lambda i: (i, 0)),
            pl.BlockSpec((2, tm, RET_WIDTH), lambda i: (0, i, 0)),
            pl.BlockSpec((tm, RET_WIDTH), lambda i: (i, COL_RET // RET_WIDTH + 3)),
            pl.BlockSpec((1, RET_WIDTH), lambda i: (0, 0)),
            pl.BlockSpec((2, tm, MLSTM_WIDTH), lambda i: (0, i, 0)),
            pl.BlockSpec((tm, MLSTM_WIDTH), lambda i: (i, COL_ML // MLSTM_WIDTH + 3)),
            pl.BlockSpec((1, MLSTM_WIDTH), lambda i: (0, 0)),
            pl.BlockSpec((tm, D_MODEL), lambda i: (i, 0)),
            pl.BlockSpec((None, 1, D_MODEL), lambda i: (_row_group(i, tm), 0, 2)),
            pl.BlockSpec((D_MODEL, D_MODEL), lambda i: (0, 0), pipeline_mode=pl.Buffered(1)),
        ],
        out_specs=pl.BlockSpec((tm, D_MODEL), lambda i: (i, 0)),
        compiler_params=_cparams(("arbitrary",)),
        name="out_proj",
    )(s5y, ret_o, a, ret_norm_w.reshape(1, RET_WIDTH), ml_o, a, mlstm_norm_w.reshape(1, MLSTM_WIDTH),
      xs, mod3, w_out)


def _mlp_kernel(x_ref, nw_ref, shift_ref, scale_ref, gate_ref, w1_ref, w2_ref, nf_ref, o_ref, lhs_scr,
                *, final_norm):
    f = pl.program_id(1)

    @pl.when(f == 0)
    def _():
        lhs_scr[...] = _rms_modulate(x_ref[...], nw_ref[...], shift_ref[...], scale_ref[...]).astype(BF16)
        o_ref[...] = jnp.zeros_like(o_ref)

    hid = jnp.dot(lhs_scr[...], w1_ref[...].astype(BF16), preferred_element_type=F32)
    hid = jnp.square(jnp.maximum(hid, 0.0)).astype(BF16)
    o_ref[...] += jnp.dot(hid, w2_ref[...].astype(BF16), preferred_element_type=F32)

    @pl.when(f == pl.num_programs(1) - 1)
    def _():
        y = x_ref[...] + gate_ref[...] * o_ref[...]
        if final_norm:
            y = y * lax.rsqrt(jnp.mean(y * y, axis=-1, keepdims=True) + EPS) * nf_ref[...]
        o_ref[...] = y


def _mlp(rows, xs, norm_w, mod3, w1, w2, norm_f_w, final_norm):
    tm, tf = 1024, 512
    modspec = lambda k: pl.BlockSpec((None, 1, D_MODEL), lambda i, f: (_row_group(i, tm), 0, k))
    once = pl.Buffered(1)
    return pl.pallas_call(
        functools.partial(_mlp_kernel, final_norm=final_norm),
        out_shape=jax.ShapeDtypeStruct((rows, D_MODEL), F32),
        grid=(rows // tm, D_FF // tf),
        in_specs=[
            pl.BlockSpec((tm, D_MODEL), lambda i, f: (i, 0), pipeline_mode=once),
            pl.BlockSpec((1, D_MODEL), lambda i, f: (0, 0)),
            modspec(3), modspec(4), modspec(5),
            pl.BlockSpec((D_MODEL, tf), lambda i, f: (0, f)),
            pl.BlockSpec((tf, D_MODEL), lambda i, f: (f, 0)),
            pl.BlockSpec((1, D_MODEL), lambda i, f: (0, 0)),
        ],
        out_specs=pl.BlockSpec((tm, D_MODEL), lambda i, f: (i, 0), pipeline_mode=once),
        scratch_shapes=[pltpu.VMEM((tm, D_MODEL), BF16)],
        compiler_params=_cparams(("arbitrary", "arbitrary"), VMEM_LIMIT_BIG),
        name="mlp",
    )(xs, norm_w.reshape(1, D_MODEL), mod3, mod3, mod3, w1, w2, norm_f_w.reshape(1, D_MODEL))


def _rope_tables():
    quarter = HEAD_DIM // 4
    rows = jnp.repeat(jnp.arange(SEQ // GRID_W, dtype=F32), GRID_W)
    cols = jnp.tile(jnp.arange(GRID_W, dtype=F32), SEQ // GRID_W)
    inv = ROPE_BASE ** (-jnp.arange(quarter, dtype=F32) / quarter)
    ang = jnp.concatenate([rows[:, None] * inv, cols[:, None] * inv], axis=-1)
    cos, sin = jnp.cos(ang), jnp.sin(ang)
    cos2 = jnp.concatenate([cos, cos], axis=-1)
    sin2 = jnp.concatenate([-sin, sin], axis=-1)
    cos2 = jnp.concatenate([cos2, jnp.ones((CTX_LEN, HEAD_DIM), F32)], axis=0)
    sin2 = jnp.concatenate([sin2, jnp.zeros((CTX_LEN, HEAD_DIM), F32)], axis=0)
    return cos2, sin2


def _permute_w_in(w):
    s5 = w[:, :S5_WIDTH]
    ret = w[:, S5_WIDTH:S5_WIDTH + 4 * RET_WIDTH]
    ml = w[:, S5_WIDTH + 4 * RET_WIDTH:S5_WIDTH + 4 * RET_WIDTH + 4 * MLSTM_WIDTH]
    gates = w[:, S5_WIDTH + 4 * RET_WIDTH + 4 * MLSTM_WIDTH:]
    pad = jnp.zeros((D_MODEL, IN_PAD - IN_WIDTH), w.dtype)
    return jnp.concatenate([ret, ml, s5, gates, pad], axis=1).astype(BF16)


def kernel(x, c, ctx, c_ctx, w_mod, b_mod, norm1_w, norm2_w, w_in, w_out, s5_lam_re, s5_lam_im, s5_log_step, s5_b_re, s5_b_im, s5_c_re, s5_c_im, s5_d, s5_w_glu, s5_b_glu, ret_decay_logit, ret_norm_w, mlstm_igate_b, mlstm_fgate_b, mlstm_norm_w, w_ff1, w_ff2, norm_f_w):
    rope_cos, rope_sin = _rope_tables()
    cvec = jnp.concatenate([c, c_ctx[None, :], jnp.zeros((MOD_ROWS - BATCH - 1, D_MODEL), F32)], axis=0)
    mod = _modulation(cvec, w_mod, b_mod)
    xs = jnp.concatenate([x.reshape(NX, D_MODEL), ctx.reshape(NH, D_MODEL)], axis=0)
    for l in range(DEPTH):
        last = l == DEPTH - 1
        mod3 = mod[l].reshape(MOD_ROWS, 1, 6 * D_MODEL)
        a = _in_proj(xs, norm1_w[l], mod3, _permute_w_in(w_in[l]))
        s5w = _s5_prep(s5_lam_re[l], s5_lam_im[l], s5_log_step[l], s5_b_re[l], s5_b_im[l],
                       s5_c_re[l], s5_c_im[l], s5_d[l])
        s5y = _s5_glu(_s5_mix(a, *s5w), s5_w_glu[l], s5_b_glu[l])
        ret_o = _retention(a, rope_cos, rope_sin, ret_decay_logit[l])
        ml_o = _mlstm(a, mlstm_igate_b[l], mlstm_fgate_b[l])
        rows = NX if last else NT
        xs = _out_proj(rows, s5y, ret_o, ml_o, a, ret_norm_w[l], mlstm_norm_w[l], xs, mod3,
                       w_out[l].astype(BF16))
        xs = _mlp(rows, xs, norm2_w[l], mod3, w_ff1[l], w_ff2[l], norm_f_w, last)
    return xs.reshape(BATCH, SEQ, D_MODEL)
```

```python
import functools
import math

import jax
import jax.numpy as jnp
from jax import lax
from jax.experimental import pallas as pl
from jax.experimental.pallas import tpu as pltpu

F32 = jnp.float32
BF16 = jnp.bfloat16

D_MODEL = 2048
BATCH = 4
SEQ = 2048
DEPTH = 2
CTX_LEN = 256
GRID_W = 64
HEAD_DIM = 128
S5_WIDTH = 512
S5_GROUP = 16
S5_GROUPS = 32
S5_STATE = 64
RET_WIDTH = 768
MLSTM_WIDTH = 768
HEADS = 6
IN_WIDTH = S5_WIDTH + 4 * RET_WIDTH + 4 * MLSTM_WIDTH + 4 * HEADS
D_FF = 4 * D_MODEL
CHUNK = 128
ROPE_BASE = 10000.0
EPS = 1e-6

NX = BATCH * SEQ
NH = BATCH * CTX_LEN
NT = NX + NH
MOD_ROWS = 8
CTX_GROUP = BATCH

COL_RET = 0
COL_ML = 4 * RET_WIDTH
COL_S5 = COL_ML + 4 * MLSTM_WIDTH
COL_GATE = COL_S5 + S5_WIDTH
IN_PAD = 7168

X_CHUNKS = SEQ // CHUNK
H_CHUNKS = CTX_LEN // CHUNK
MIX_STEPS = X_CHUNKS + H_CHUNKS

S5_T = 8
S5_LANES = 128
S5_LG = S5_LANES // S5_GROUP
S5_TILES = S5_WIDTH // S5_LANES
S5_K = S5_T * S5_LANES
S5_XB = SEQ // S5_T
S5_HB = CTX_LEN // S5_T
S5_NB = S5_XB + S5_HB
S5_ROWS = S5_NB * BATCH
S5_RC = 128

VMEM_LIMIT = 48 * 1024 * 1024
VMEM_LIMIT_BIG = 56 * 1024 * 1024


def _cparams(sem, limit=VMEM_LIMIT):
    return pltpu.CompilerParams(dimension_semantics=sem, vmem_limit_bytes=limit)


def _sigmoid(x):
    return 1.0 / (1.0 + jnp.exp(-x))


def _silu(x):
    return x * _sigmoid(x)


def _log_sigmoid(x):
    return jnp.minimum(x, 0.0) - jnp.log1p(jnp.exp(-jnp.abs(x)))


def _gelu_tanh(x):
    c = math.sqrt(2.0 / math.pi)
    return 0.5 * x * (1.0 + jnp.tanh(c * (x + 0.044715 * (x * x * x))))


def _row_group(i, tm):
    return jnp.minimum((i * tm) // SEQ, CTX_GROUP)


def _mod_kernel(c_ref, w_ref, b_ref, o_ref):
    c = c_ref[...]
    lhs = _silu(c).astype(BF16)
    o_ref[...] = jnp.dot(lhs, w_ref[...].astype(BF16), preferred_element_type=F32) + b_ref[...]


def _modulation(cvec, w_mod, b_mod):
    tn = 512
    n = 6 * D_MODEL
    return pl.pallas_call(
        _mod_kernel,
        out_shape=jax.ShapeDtypeStruct((DEPTH, MOD_ROWS, n), F32),
        grid=(DEPTH, n // tn),
        in_specs=[
            pl.BlockSpec((MOD_ROWS, D_MODEL), lambda l, j: (0, 0)),
            pl.BlockSpec((None, D_MODEL, tn), lambda l, j: (l, 0, j)),
            pl.BlockSpec((None, 1, tn), lambda l, j: (l, 0, j)),
        ],
        out_specs=pl.BlockSpec((None, MOD_ROWS, tn), lambda l, j: (l, 0, j)),
        compiler_params=_cparams(("arbitrary", "arbitrary")),
        name="modulation",
    )(cvec, w_mod, b_mod.reshape(DEPTH, 1, n))


def _rms_modulate(x, nw, shift, scale):
    y = x * lax.rsqrt(jnp.mean(x * x, axis=-1, keepdims=True) + EPS) * nw
    return y * (1.0 + scale) + shift


def _in_proj_kernel(x_ref, nw_ref, shift_ref, scale_ref, w_ref, o_ref, lhs_scr):
    @pl.when(pl.program_id(1) == 0)
    def _():
        lhs_scr[...] = _rms_modulate(x_ref[...], nw_ref[...], shift_ref[...], scale_ref[...]).astype(BF16)

    o_ref[...] = jnp.dot(lhs_scr[...], w_ref[...], preferred_element_type=F32)


def _in_proj(xs, norm_w, mod3, w_in_p):
    tm, tn = 1024, 1024
    return pl.pallas_call(
        _in_proj_kernel,
        out_shape=jax.ShapeDtypeStruct((NT, IN_PAD), F32),
        grid=(NT // tm, IN_PAD // tn),
        in_specs=[
            pl.BlockSpec((tm, D_MODEL), lambda i, j: (i, 0)),
            pl.BlockSpec((1, D_MODEL), lambda i, j: (0, 0)),
            pl.BlockSpec((None, 1, D_MODEL), lambda i, j: (_row_group(i, tm), 0, 0)),
            pl.BlockSpec((None, 1, D_MODEL), lambda i, j: (_row_group(i, tm), 0, 1)),
            pl.BlockSpec((D_MODEL, tn), lambda i, j: (0, j)),
        ],
        out_specs=pl.BlockSpec((tm, tn), lambda i, j: (i, j)),
        scratch_shapes=[pltpu.VMEM((tm, D_MODEL), BF16)],
        compiler_params=_cparams(("arbitrary", "arbitrary")),
        name="in_proj",
    )(xs, norm_w.reshape(1, D_MODEL), mod3, mod3, w_in_p)


def _s5_prep(lam_re, lam_im, log_step, b_re, b_im, c_re, c_im, d_skip):
    hp = lax.Precision.HIGHEST
    lam_re = jnp.minimum(lam_re.astype(F32), -1e-4)
    lam_im = lam_im.astype(F32)
    step = jnp.exp(log_step.astype(F32))[..., None]
    mag = jnp.exp(lam_re * step)
    ab_re, ab_im = mag * jnp.cos(lam_im * step), mag * jnp.sin(lam_im * step)
    den = lam_re * lam_re + lam_im * lam_im
    ir, ii = lam_re / den, -lam_im / den
    nr = (ab_re - 1.0) * ir - ab_im * ii
    ni = (ab_re - 1.0) * ii + ab_im * ir
    bre, bim = b_re.astype(F32), b_im.astype(F32)
    bb_re = nr[..., None] * bre - ni[..., None] * bim
    bb_im = nr[..., None] * bim + ni[..., None] * bre
    lpow = jnp.arange(S5_T + 1, dtype=F32)[:, None, None, None]
    pmag = jnp.exp(lpow * (lam_re * step))
    pr, pi = pmag * jnp.cos(lpow * (lam_im * step)), pmag * jnp.sin(lpow * (lam_im * step))
    cre, cim = c_re.astype(F32), c_im.astype(F32)
    ca_r = cre[None] * pr[:, :, :, None, :] - cim[None] * pi[:, :, :, None, :]
    ca_i = cre[None] * pi[:, :, :, None, :] + cim[None] * pr[:, :, :, None, :]
    kl = (jnp.einsum('ldgnp,dgpm->ldgnm', ca_r, bb_re, precision=hp)
          - jnp.einsum('ldgnp,dgpm->ldgnm', ca_i, bb_im, precision=hp))
    tt = jnp.arange(S5_T)
    lag = tt[None, :] - tt[:, None]
    kf = jnp.where((lag >= 0)[:, :, None, None, None], kl[jnp.clip(lag, 0, S5_T), 0], 0.0)
    kb = jnp.where((lag <= 0)[:, :, None, None, None], kl[jnp.clip(-lag, 0, S5_T), 1], 0.0)
    nt = S5_TILES
    row = jnp.arange(S5_K)
    col = jnp.arange(S5_K)
    lane = jnp.arange(S5_LANES)
    g_blk = (row // S5_GROUP) % S5_LG
    g_state = 2 * (row // (4 * S5_STATE)) + (row // S5_STATE) % 2
    to_blk = ((lane[:, None] // S5_GROUP == col[None, :] // S5_LANES)
              & (lane[:, None] % S5_GROUP == col[None, :] % S5_GROUP)).astype(BF16)
    to_state = ((lane[:, None] // S5_STATE == (col[None, :] // S5_LANES) % 2)
                & (lane[:, None] % S5_STATE == col[None, :] % S5_STATE)).astype(BF16)

    def expand(compact, spread, g_row, g_col):
        wide = jnp.einsum('trk,kc->trc', compact.reshape(nt, S5_K, S5_LANES).astype(BF16), spread,
                          preferred_element_type=F32)
        return jnp.where(g_row[:, None] == g_col[None, :], wide, 0.0).astype(BF16)

    mc = (kf + kb).reshape(S5_T, S5_T, nt, S5_LG, S5_GROUP, S5_GROUP).transpose(2, 0, 3, 5, 1, 4)
    m = expand(mc, to_blk, g_blk, g_blk)

    def carry_in(d, powers):
        er = pr[powers, d][..., None] * bb_re[d][None] - pi[powers, d][..., None] * bb_im[d][None]
        ei = pr[powers, d][..., None] * bb_im[d][None] + pi[powers, d][..., None] * bb_re[d][None]
        arrange = lambda t: t.reshape(S5_T, nt, S5_LG, S5_STATE, S5_GROUP).transpose(1, 0, 2, 4, 3)
        es = jnp.stack([arrange(er), arrange(ei)], axis=4)
        return expand(es, to_state, g_blk, g_state)

    def read_out(d, powers):
        arrange = lambda t: t.reshape(S5_T, nt, S5_LG // 2, 2, S5_GROUP, S5_STATE).transpose(1, 2, 3, 5, 0, 4)
        fs = jnp.stack([arrange(ca_r[powers, d]), arrange(-ca_i[powers, d])], axis=2)
        return expand(fs, to_blk, g_state, g_blk)

    ef, eb = carry_in(0, S5_T - 1 - tt), carry_in(1, tt)
    ff, fb = read_out(0, tt + 1), read_out(1, S5_T - tt)
    lanes = lambda t: t.reshape(nt, S5_K // 2)
    a_r = jnp.concatenate([lanes(pr[S5_T, 0]), lanes(pr[S5_T, 1])], axis=-1)
    a_i = jnp.concatenate([lanes(pi[S5_T, 0]), lanes(pi[S5_T, 1])], axis=-1)
    dsk = jnp.pad(d_skip.astype(F32).reshape(nt, S5_LANES), ((0, 0), (0, S5_K - S5_LANES)))
    aux = jnp.stack([a_r, a_i, dsk] + [jnp.zeros_like(a_r)] * 5, axis=1)
    return ef, eb, m, ff, fb, aux


def _s5_kernel(a_ref, ef_ref, eb_ref, m_ref, ff_ref, fb_ref, aux_ref, y_ref, u_scr, wf_scr, wb_scr):
    for b in range(BATCH):
        for t in range(S5_T):
            u_scr.at[t][pl.ds(S5_HB * BATCH + b, S5_XB, stride=BATCH), :] = (
                a_ref[pl.ds(b * SEQ + t, S5_XB, stride=S5_T), :])
            u_scr.at[t][pl.ds(b, S5_HB, stride=BATCH), :] = a_ref[pl.ds(NX + b * CTX_LEN + t, S5_HB, stride=S5_T), :]

    def get_tiles(scr, r0):
        return jnp.concatenate([scr.at[t][pl.ds(r0, S5_RC), :] for t in range(S5_T)], axis=1).astype(BF16)

    def put_tiles(scr, r0, val):
        for t in range(S5_T):
            scr.at[t][pl.ds(r0, S5_RC), :] = val[:, t * S5_LANES:(t + 1) * S5_LANES]

    def block_rows(rc):
        r0 = pl.multiple_of(rc * S5_RC, S5_RC)
        return r0, get_tiles(u_scr, r0)

    def carry_in(rc, _):
        r0, ub = block_rows(rc)
        put_tiles(wf_scr, r0, jnp.dot(ub, ef_ref[...], preferred_element_type=F32))
        put_tiles(wb_scr, r0, jnp.dot(ub, eb_ref[...], preferred_element_type=F32))
        return 0

    lax.fori_loop(0, S5_ROWS // S5_RC, carry_in, 0)

    aux = aux_ref[...]
    npair = S5_LG // 2
    shape = (BATCH, S5_LANES)
    coef = []
    for d in range(2):
        for j in range(npair):
            lo = d * (S5_K // 2) + j * S5_LANES
            coef.append((jnp.broadcast_to(aux[0:1, lo:lo + S5_LANES], shape),
                         jnp.broadcast_to(aux[1:2, lo:lo + S5_LANES], shape)))

    def scan(s, carry):
        rf = pl.multiple_of(s * BATCH, BATCH)
        gb = jnp.where(s < S5_HB, S5_HB - 1 - s, S5_NB + S5_HB - 1 - s)
        rb = pl.multiple_of(gb * BATCH, BATCH)
        out = []
        for d, (w_scr, r) in enumerate(((wf_scr, rf), (wb_scr, rb))):
            for j in range(npair):
                xr, xi = carry[d * npair + j]
                ar, ai = coef[d * npair + j]
                w_re, w_im = w_scr.at[2 * j], w_scr.at[2 * j + 1]
                wr = w_re[pl.ds(r, BATCH), :]
                wi = w_im[pl.ds(r, BATCH), :]
                w_re[pl.ds(r, BATCH), :] = xr
                w_im[pl.ds(r, BATCH), :] = xi
                out.append((ar * xr - ai * xi + wr, ar * xi + ai * xr + wi))
        return tuple(out)

    z = jnp.zeros(shape, F32)
    lax.fori_loop(0, S5_NB, scan, tuple((z, z) for _ in range(2 * npair)))

    def read_out(rc, _):
        r0, ub = block_rows(rc)
        y = jnp.dot(ub, m_ref[...], preferred_element_type=F32)
        y = y + jnp.dot(get_tiles(wf_scr, r0), ff_ref[...], preferred_element_type=F32)
        y = y + jnp.dot(get_tiles(wb_scr, r0), fb_ref[...], preferred_element_type=F32)
        put_tiles(u_scr, r0, y)
        return 0

    lax.fori_loop(0, S5_ROWS // S5_RC, read_out, 0)

    for b in range(BATCH):
        for t in range(S5_T):
            y_ref[pl.ds(b * SEQ + t, S5_XB, stride=S5_T), :] = (
                u_scr.at[t][pl.ds(S5_HB * BATCH + b, S5_XB, stride=BATCH), :])
            y_ref[pl.ds(NX + b * CTX_LEN + t, S5_HB, stride=S5_T), :] = u_scr.at[t][pl.ds(b, S5_HB, stride=BATCH), :]
    y_ref[...] = y_ref[...] + a_ref[...] * aux[2:3, 0:S5_LANES]


def _s5_mix(a, ef, eb, m, ff, fb, aux):
    wspec = lambda: pl.BlockSpec((None, S5_K, S5_K), lambda g: (g, 0, 0), pipeline_mode=pl.Buffered(1))
    return pl.pallas_call(
        _s5_kernel,
        out_shape=jax.ShapeDtypeStruct((NT, S5_WIDTH), F32),
        grid=(S5_TILES,),
        in_specs=[
            pl.BlockSpec((NT, S5_LANES), lambda g: (0, COL_S5 // S5_LANES + g)),
            wspec(), wspec(), wspec(), wspec(), wspec(),
            pl.BlockSpec((None, 8, S5_K), lambda g: (g, 0, 0)),
        ],
        out_specs=pl.BlockSpec((NT, S5_LANES), lambda g: (0, g)),
        scratch_shapes=[pltpu.VMEM((S5_T, S5_ROWS, S5_LANES), F32)] * 3,
        compiler_params=_cparams(("arbitrary",), VMEM_LIMIT_BIG),
        name="s5_scan",
    )(a, ef, eb, m, ff, fb, aux)


def _glu_kernel(y_ref, w_ref, b_ref, o_ref):
    g = _gelu_tanh(y_ref[...]).astype(BF16)
    z = jnp.dot(g, w_ref[...], preferred_element_type=F32) + b_ref[...]
    o_ref[...] = z[:, :S5_WIDTH] * _sigmoid(z[:, S5_WIDTH:])


def _s5_glu(y, w_glu, b_glu):
    tm = 1024
    return pl.pallas_call(
        _glu_kernel,
        out_shape=jax.ShapeDtypeStruct((NT, S5_WIDTH), F32),
        grid=(NT // tm,),
        in_specs=[
            pl.BlockSpec((tm, S5_WIDTH), lambda i: (i, 0)),
            pl.BlockSpec((S5_WIDTH, 2 * S5_WIDTH), lambda i: (0, 0)),
            pl.BlockSpec((1, 2 * S5_WIDTH), lambda i: (0, 0)),
        ],
        out_specs=pl.BlockSpec((tm, S5_WIDTH), lambda i: (i, 0)),
        compiler_params=_cparams(("arbitrary",)),
        name="s5_glu",
    )(y, w_glu.astype(BF16), b_glu.reshape(1, 2 * S5_WIDTH))


def _chunk_index(d, s):
    is_ctx = s < H_CHUNKS
    fwd = jnp.where(is_ctx, s, s - H_CHUNKS)
    bwd = jnp.where(is_ctx, H_CHUNKS - 1 - s, MIX_STEPS - 1 - s)
    return is_ctx, jnp.where(d == 0, fwd, bwd)


def _token_block(b, d, s):
    is_ctx, c = _chunk_index(d, s)
    return jnp.where(is_ctx, NX // CHUNK + b * H_CHUNKS + c, b * X_CHUNKS + c)


def _rope_block(b, d, s):
    is_ctx, c = _chunk_index(d, s)
    return jnp.where(is_ctx, X_CHUNKS + c, c)


def _dot_nt(a, b):
    return lax.dot_general(a, b, (((1,), (1,)), ((), ())), preferred_element_type=F32)


def _dot_tn(a, b):
    return lax.dot_general(a, b, (((0,), (0,)), ((), ())), preferred_element_type=F32)


def _dot(a, b):
    return jnp.dot(a, b, preferred_element_type=F32)


def _head(ref, h):
    return ref[:, h * HEAD_DIM:(h + 1) * HEAD_DIM]


def _chunk_pos(rev):
    i = lax.broadcasted_iota(jnp.int32, (CHUNK, CHUNK), 0)
    j = lax.broadcasted_iota(jnp.int32, (CHUNK, CHUNK), 1)
    pi = jnp.where(rev, CHUNK - 1 - i, i)
    pj = jnp.where(rev, CHUNK - 1 - j, j)
    return pi, pj


def _ret_kernel(dl_ref, q_ref, k_ref, v_ref, cos_ref, sin_ref, o_ref, r_scr, dec_scr, qd_scr, kd_scr, cd_scr):
    d = pl.program_id(1)
    s = pl.program_id(2)
    rev = d == 1

    @pl.when(s == 0)
    def _():
        pi, pj = _chunk_pos(rev)
        rel = (pi - pj).astype(F32)
        pif = pi.astype(F32)
        for h in range(HEADS):
            lg = _log_sigmoid(dl_ref[h])[0:1, :]
            dec_scr[h] = jnp.where(rel >= 0, jnp.exp(lg * jnp.maximum(rel, 0.0)), 0.0)
            qd_scr[h] = jnp.exp(lg * (pif + 1.0))
            kd_scr[h] = jnp.exp(lg * (CHUNK - 1.0 - pif))
            cd_scr[h] = jnp.exp(jnp.broadcast_to(lg, (8, HEAD_DIM)) * CHUNK)
        r_scr[...] = jnp.zeros_like(r_scr)

    cos = cos_ref[...]
    sin = sin_ref[...]
    scale = HEAD_DIM ** -0.5
    hs = range(HEADS)
    q = [_head(q_ref, h) for h in hs]
    k = [_head(k_ref, h) for h in hs]
    v = [_head(v_ref, h).astype(BF16) for h in hs]
    q = [q[h] * cos + pltpu.roll(q[h], HEAD_DIM // 2, 1) * sin for h in hs]
    k = [(k[h] * cos + pltpu.roll(k[h], HEAD_DIM // 2, 1) * sin) * scale for h in hs]
    r = [r_scr[h] for h in hs]
    sm = [_dot_nt(q[h].astype(BF16), k[h].astype(BF16)) * dec_scr[h] for h in hs]
    inter = [_dot((q[h] * qd_scr[h]).astype(BF16), r[h].astype(BF16)) for h in hs]
    kv = [_dot_tn((k[h] * kd_scr[h]).astype(BF16), v[h]) for h in hs]
    for h in hs:
        o_ref[:, h * HEAD_DIM:(h + 1) * HEAD_DIM] = _dot(sm[h].astype(BF16), v[h]) + inter[h]
        r_scr[h] = cd_scr[h][0:1, :] * r[h] + kv[h]


def _retention(a, rope_cos, rope_sin, decay_logit):
    dl = jnp.broadcast_to(decay_logit.astype(F32)[:, :, None, None], (2, HEADS, 8, HEAD_DIM))
    tok = lambda col: pl.BlockSpec((CHUNK, RET_WIDTH), lambda b, d, s: (_token_block(b, d, s), col))
    rope = pl.BlockSpec((CHUNK, HEAD_DIM), lambda b, d, s: (_rope_block(b, d, s), 0))
    hh = pltpu.VMEM((HEADS, CHUNK, HEAD_DIM), F32)
    return pl.pallas_call(
        _ret_kernel,
        out_shape=jax.ShapeDtypeStruct((2, NT, RET_WIDTH), F32),
        grid=(BATCH, 2, MIX_STEPS),
        in_specs=[
            pl.BlockSpec((None, HEADS, 8, HEAD_DIM), lambda b, d, s: (d, 0, 0, 0)),
            tok(COL_RET // RET_WIDTH), tok(COL_RET // RET_WIDTH + 1), tok(COL_RET // RET_WIDTH + 2),
            rope, rope,
        ],
        out_specs=pl.BlockSpec((None, CHUNK, RET_WIDTH), lambda b, d, s: (d, _token_block(b, d, s), 0)),
        scratch_shapes=[hh, hh, hh, hh, pltpu.VMEM((HEADS, 8, HEAD_DIM), F32)],
        compiler_params=_cparams(("arbitrary", "arbitrary", "arbitrary")),
        name="retention",
    )(dl, a, a, a, rope_cos, rope_sin)


def _split3(x):
    h = x.astype(BF16)
    r = x - h.astype(F32)
    m = r.astype(BF16)
    return h, m, (r - m.astype(F32)).astype(BF16)


def _mlstm_kernel(q_ref, k_ref, v_ref, g_ref, gb_ref, o_ref, c_scr, n_scr, m_scr):
    d = pl.program_id(1)
    s = pl.program_id(2)
    rev = d == 1

    @pl.when(s == 0)
    def _():
        c_scr[...] = jnp.zeros_like(c_scr)
        n_scr[...] = jnp.zeros_like(n_scr)
        m_scr[...] = jnp.zeros_like(m_scr)

    pi, pj = _chunk_pos(rev)
    causal = pj <= pi
    tri = jnp.where(causal, 1.0, 0.0).astype(BF16)
    pre = g_ref[...] + gb_ref[...]
    pre_t = pre.T[0:32, :]
    lf_c = _log_sigmoid(pre)
    lf_r = _log_sigmoid(pre_t)
    b_cols = sum(_dot(tri, part) for part in _split3(lf_c))
    b_rows = sum(_dot_nt(part, tri) for part in _split3(lf_r))
    scale = HEAD_DIM ** -0.5
    neg_inf = -jnp.inf
    hs = range(HEADS)
    pick = lambda t, kind, h, rows: jnp.where(rev, rows(t, (2 + kind) * HEADS + h), rows(t, kind * HEADS + h))
    as_col = lambda t, c: t[:, c:c + 1]
    as_row = lambda t, c: t[c:c + 1, :]
    i_col = [pick(pre, 0, h, as_col) for h in hs]
    b_col = [pick(b_cols, 1, h, as_col) for h in hs]
    i_row = [pick(pre_t, 0, h, as_row) for h in hs]
    b_row = [pick(b_rows, 1, h, as_row) for h in hs]
    b_end = [jnp.where(rev, b_cols[0:1, 3 * HEADS + h:3 * HEADS + h + 1],
                       b_cols[CHUNK - 1:CHUNK, HEADS + h:HEADS + h + 1]) for h in hs]
    m_prev = [m_scr[h][0:1, 0:1] for h in hs]
    q = [_head(q_ref, h) for h in hs]
    k = [_head(k_ref, h) * scale for h in hs]
    v = [_head(v_ref, h).astype(BF16) for h in hs]
    qb = [q[h].astype(BF16) for h in hs]
    c_mem = [c_scr[h] for h in hs]
    n_mem = [n_scr[h][0:1, :] for h in hs]
    log_w = [jnp.where(causal, b_col[h] - b_row[h] + i_row[h], neg_inf) for h in hs]
    log_a = [b_col[h] + m_prev[h] for h in hs]
    m_t = [jnp.maximum(log_a[h], jnp.max(log_w[h], axis=-1, keepdims=True)) for h in hs]
    w = [jnp.exp(log_w[h] - m_t[h]) for h in hs]
    a = [jnp.exp(log_a[h] - m_t[h]) for h in hs]
    sm = [_dot_nt(qb[h], k[h].astype(BF16)) * w[h] for h in hs]
    inter = [_dot(qb[h], c_mem[h].astype(BF16)) for h in hs]
    num = [_dot(sm[h].astype(BF16), v[h]) + a[h] * inter[h] for h in hs]
    den = [jnp.sum(sm[h] + a[h] * (q[h] * n_mem[h]), axis=-1, keepdims=True) for h in hs]
    for h in hs:
        o_ref[:, h * HEAD_DIM:(h + 1) * HEAD_DIM] = num[h] / jnp.maximum(jnp.abs(den[h]), jnp.exp(-m_t[h]))
    log_w_end = [b_end[h] - b_col[h] + i_col[h] for h in hs]
    m_new = [jnp.maximum(b_end[h] + m_prev[h], jnp.max(log_w_end[h], axis=0, keepdims=True)) for h in hs]
    a_end = [jnp.exp(b_end[h] + m_prev[h] - m_new[h]) for h in hs]
    kw = [k[h] * jnp.exp(log_w_end[h] - m_new[h]) for h in hs]
    kv = [_dot_tn(kw[h].astype(BF16), v[h]) for h in hs]
    for h in hs:
        c_scr[h] = a_end[h] * c_mem[h] + kv[h]
        n_scr[h] = jnp.broadcast_to(a_end[h] * n_mem[h] + jnp.sum(kw[h], axis=0, keepdims=True), (8, HEAD_DIM))
        m_scr[h] = jnp.broadcast_to(m_new[h], (8, HEAD_DIM))


def _mlstm(a, igate_b, fgate_b):
    gb = jnp.stack([igate_b.astype(F32), fgate_b.astype(F32)], axis=1).reshape(1, 4 * HEADS)
    gb = jnp.pad(gb, ((0, 0), (0, HEAD_DIM - 4 * HEADS)))
    col0 = COL_ML // MLSTM_WIDTH
    tok = lambda col: pl.BlockSpec((CHUNK, MLSTM_WIDTH), lambda b, d, s: (_token_block(b, d, s), col))
    small = pltpu.VMEM((HEADS, 8, HEAD_DIM), F32)
    return pl.pallas_call(
        _mlstm_kernel,
        out_shape=jax.ShapeDtypeStruct((2, NT, MLSTM_WIDTH), F32),
        grid=(BATCH, 2, MIX_STEPS),
        in_specs=[
            tok(col0), tok(col0 + 1), tok(col0 + 2),
            pl.BlockSpec((CHUNK, HEAD_DIM), lambda b, d, s: (_token_block(b, d, s), COL_GATE // HEAD_DIM)),
            pl.BlockSpec((1, HEAD_DIM), lambda b, d, s: (0, 0)),
        ],
        out_specs=pl.BlockSpec((None, CHUNK, MLSTM_WIDTH), lambda b, d, s: (d, _token_block(b, d, s), 0)),
        scratch_shapes=[pltpu.VMEM((HEADS, CHUNK, HEAD_DIM), F32), small, small],
        compiler_params=_cparams(("arbitrary", "arbitrary", "arbitrary")),
        name="mlstm",
    )(a, a, a, a, gb)


def _head_norm(o, center):
    if center:
        o = o - jnp.mean(o, axis=-1, keepdims=True)
    return o * lax.rsqrt(jnp.mean(o * o, axis=-1, keepdims=True) + EPS)


def _out_proj_kernel(s5_ref, ro_ref, rg_ref, rw_ref, mo_ref, mg_ref, mw_ref, x_ref, gate_ref, w_ref, o_ref):
    pair = 2 * HEAD_DIM
    acc = jnp.dot(s5_ref[:, 0:pair].astype(BF16), w_ref[0:pair, :], preferred_element_type=F32)
    acc += jnp.dot(s5_ref[:, pair:2 * pair].astype(BF16), w_ref[pair:2 * pair, :], preferred_element_type=F32)
    for hp in range(HEADS // 2):
        rs, ms = [], []
        for h in (2 * hp, 2 * hp + 1):
            sl = slice(h * HEAD_DIM, (h + 1) * HEAD_DIM)
            r = _head_norm(ro_ref[0, :, sl] + ro_ref[1, :, sl], True) * rw_ref[:, sl] * _silu(rg_ref[:, sl])
            m = _head_norm(mo_ref[0, :, sl] + mo_ref[1, :, sl], False) * mw_ref[:, sl] * _sigmoid(mg_ref[:, sl])
            rs.append(r.astype(BF16))
            ms.append(m.astype(BF16))
        r0 = S5_WIDTH + hp * pair
        m0 = S5_WIDTH + RET_WIDTH + hp * pair
        acc += jnp.dot(jnp.concatenate(rs, axis=1), w_ref[r0:r0 + pair, :], preferred_element_type=F32)
        acc += jnp.dot(jnp.concatenate(ms, axis=1), w_ref[m0:m0 + pair, :], preferred_element_type=F32)
    o_ref[...] = x_ref[...] + gate_ref[...] * acc


def _out_proj(rows, s5y, ret_o, ml_o, a, ret_norm_w, mlstm_norm_w, xs, mod3, w_out):
    tm = 256
    return pl.pallas_call(
        _out_proj_kernel,
        out_shape=jax.ShapeDtypeStruct((rows, D_MODEL), F32),
        grid=(rows // tm,),
        in_specs=[
            pl.BlockSpec((tm, S5_WIDTH), lambda i: (i, 0)),
            pl.BlockSpec((2, tm, RET_WIDTH), lambda i: (0, i, 0)),
            pl.BlockSpec((tm, RET_WIDTH), lambda i: (i, COL_RET // RET_WIDTH + 3)),
            pl.BlockSpec((1, RET_WIDTH), lambda i: (0, 0)),
            pl.BlockSpec((2, tm, MLSTM_WIDTH), lambda i: (0, i, 0)),
            pl.BlockSpec((tm, MLSTM_WIDTH), lambda i: (i, COL_ML // MLSTM_WIDTH + 3)),
            pl.BlockSpec((1, MLSTM_WIDTH), lambda i: (0, 0)),
            pl.BlockSpec((tm, D_MODEL), lambda i: (i, 0)),
            pl.BlockSpec((None, 1, D_MODEL), lambda i: (_row_group(i, tm), 0, 2)),
            pl.BlockSpec((D_MODEL, D_MODEL), lambda i: (0, 0), pipeline_mode=pl.Buffered(1)),
        ],
        out_specs=pl.BlockSpec((tm, D_MODEL), lambda i: (i, 0)),
        compiler_params=_cparams(("arbitrary",)),
        name="out_proj",
    )(s5y, ret_o, a, ret_norm_w.reshape(1, RET_WIDTH), ml_o, a, mlstm_norm_w.reshape(1, MLSTM_WIDTH),
      xs, mod3, w_out)


def _mlp_kernel(x_ref, nw_ref, shift_ref, scale_ref, gate_ref, w1_ref, w2_ref, nf_ref, o_ref, lhs_scr,
                *, final_norm):
    f = pl.program_id(1)

    @pl.when(f == 0)
    def _():
        lhs_scr[...] = _rms_modulate(x_ref[...], nw_ref[...], shift_ref[...], scale_ref[...]).astype(BF16)
        o_ref[...] = jnp.zeros_like(o_ref)

    hid = jnp.dot(lhs_scr[...], w1_ref[...].astype(BF16), preferred_element_type=F32)
    hid = jnp.square(jnp.maximum(hid, 0.0)).astype(BF16)
    o_ref[...] += jnp.dot(hid, w2_ref[...].astype(BF16), preferred_element_type=F32)

    @pl.when(f == pl.num_programs(1) - 1)
    def _():
        y = x_ref[...] + gate_ref[...] * o_ref[...]
        if final_norm:
            y = y * lax.rsqrt(jnp.mean(y * y, axis=-1, keepdims=True) + EPS) * nf_ref[...]
        o_ref[...] = y


def _mlp(rows, xs, norm_w, mod3, w1, w2, layer, norm_f_w, final_norm):
    tm, tf = 1024, 512
    modspec = lambda k: pl.BlockSpec((None, 1, D_MODEL), lambda i, f: (_row_group(i, tm), 0, k))
    once = pl.Buffered(1)
    return pl.pallas_call(
        functools.partial(_mlp_kernel, final_norm=final_norm),
        out_shape=jax.ShapeDtypeStruct((rows, D_MODEL), F32),
        grid=(rows // tm, D_FF // tf),
        in_specs=[
            pl.BlockSpec((tm, D_MODEL), lambda i, f: (i, 0), pipeline_mode=once),
            pl.BlockSpec((1, D_MODEL), lambda i, f: (0, 0)),
            modspec(3), modspec(4), modspec(5),
            pl.BlockSpec((None, D_MODEL, tf), lambda i, f: (layer, 0, f)),
            pl.BlockSpec((None, tf, D_MODEL), lambda i, f: (layer, f, 0)),
            pl.BlockSpec((1, D_MODEL), lambda i, f: (0, 0)),
        ],
        out_specs=pl.BlockSpec((tm, D_MODEL), lambda i, f: (i, 0), pipeline_mode=once),
        scratch_shapes=[pltpu.VMEM((tm, D_MODEL), BF16)],
        compiler_params=_cparams(("arbitrary", "arbitrary"), VMEM_LIMIT_BIG),
        name="mlp",
    )(xs, norm_w.reshape(1, D_MODEL), mod3, mod3, mod3, w1, w2, norm_f_w.reshape(1, D_MODEL))


def _rope_tables():
    quarter = HEAD_DIM // 4
    rows = jnp.repeat(jnp.arange(SEQ // GRID_W, dtype=F32), GRID_W)
    cols = jnp.tile(jnp.arange(GRID_W, dtype=F32), SEQ // GRID_W)
    inv = ROPE_BASE ** (-jnp.arange(quarter, dtype=F32) / quarter)
    ang = jnp.concatenate([rows[:, None] * inv, cols[:, None] * inv], axis=-1)
    cos, sin = jnp.cos(ang), jnp.sin(ang)
    cos2 = jnp.concatenate([cos, cos], axis=-1)
    sin2 = jnp.concatenate([-sin, sin], axis=-1)
    cos2 = jnp.concatenate([cos2, jnp.ones((CTX_LEN, HEAD_DIM), F32)], axis=0)
    sin2 = jnp.concatenate([sin2, jnp.zeros((CTX_LEN, HEAD_DIM), F32)], axis=0)
    return cos2, sin2


def _permute_w_in(w):
    s5 = w[:, :S5_WIDTH]
    ret = w[:, S5_WIDTH:S5_WIDTH + 4 * RET_WIDTH]
    ml = w[:, S5_WIDTH + 4 * RET_WIDTH:S5_WIDTH + 4 * RET_WIDTH + 4 * MLSTM_WIDTH]
    gates = w[:, S5_WIDTH + 4 * RET_WIDTH + 4 * MLSTM_WIDTH:]
    pad = jnp.zeros((D_MODEL, IN_PAD - IN_WIDTH), w.dtype)
    return jnp.concatenate([ret, ml, s5, gates, pad], axis=1).astype(BF16)


def kernel(x, c, ctx, c_ctx, w_mod, b_mod, norm1_w, norm2_w, w_in, w_out, s5_lam_re, s5_lam_im, s5_log_step, s5_b_re, s5_b_im, s5_c_re, s5_c_im, s5_d, s5_w_glu, s5_b_glu, ret_decay_logit, ret_norm_w, mlstm_igate_b, mlstm_fgate_b, mlstm_norm_w, w_ff1, w_ff2, norm_f_w):
    rope_cos, rope_sin = _rope_tables()
    cvec = jnp.concatenate([c, c_ctx[None, :], jnp.zeros((MOD_ROWS - BATCH - 1, D_MODEL), F32)], axis=0)
    mod = _modulation(cvec, w_mod, b_mod)
    xs = jnp.concatenate([x.reshape(NX, D_MODEL), ctx.reshape(NH, D_MODEL)], axis=0)
    for l in range(DEPTH):
        last = l == DEPTH - 1
        mod3 = mod[l].reshape(MOD_ROWS, 1, 6 * D_MODEL)
        a = _in_proj(xs, norm1_w[l], mod3, _permute_w_in(w_in[l]))
        s5w = _s5_prep(s5_lam_re[l], s5_lam_im[l], s5_log_step[l], s5_b_re[l], s5_b_im[l],
                       s5_c_re[l], s5_c_im[l], s5_d[l])
        s5y = _s5_glu(_s5_mix(a, *s5w), s5_w_glu[l], s5_b_glu[l])
        ret_o = _retention(a, rope_cos, rope_sin, ret_decay_logit[l])
        ml_o = _mlstm(a, mlstm_igate_b[l], mlstm_fgate_b[l])
        rows = NX if last else NT
        xs = _out_proj(rows, s5y, ret_o, ml_o, a, ret_norm_w[l], mlstm_norm_w[l], xs, mod3,
                       w_out[l].astype(BF16))
        xs = _mlp(rows, xs, norm2_w[l], mod3, w_ff1, w_ff2, l, norm_f_w, last)
    return xs.reshape(BATCH, SEQ, D_MODEL)
```

```python
import functools
import math

import jax
import jax.numpy as jnp
from jax import lax
from jax.experimental import pallas as pl
from jax.experimental.pallas import tpu as pltpu

F32 = jnp.float32
BF16 = jnp.bfloat16

D_MODEL = 2048
BATCH = 4
SEQ = 2048
DEPTH = 2
CTX_LEN = 256
GRID_W = 64
HEAD_DIM = 128
S5_WIDTH = 512
S5_GROUP = 16
S5_GROUPS = 32
S5_STATE = 64
RET_WIDTH = 768
MLSTM_WIDTH = 768
HEADS = 6
IN_WIDTH = S5_WIDTH + 4 * RET_WIDTH + 4 * MLSTM_WIDTH + 4 * HEADS
D_FF = 4 * D_MODEL
CHUNK = 128
ROPE_BASE = 10000.0
EPS = 1e-6

NX = BATCH * SEQ
NH = BATCH * CTX_LEN
NT = NX + NH
MOD_ROWS = 8
CTX_GROUP = BATCH

COL_RET = 0
COL_ML = 4 * RET_WIDTH
COL_S5 = COL_ML + 4 * MLSTM_WIDTH
COL_GATE = COL_S5 + S5_WIDTH
IN_PAD = 7168

X_CHUNKS = SEQ // CHUNK
H_CHUNKS = CTX_LEN // CHUNK
MIX_STEPS = X_CHUNKS + H_CHUNKS

S5_T = 8
S5_LANES = 128
S5_LG = S5_LANES // S5_GROUP
S5_TILES = S5_WIDTH // S5_LANES
S5_K = S5_T * S5_LANES
S5_XB = SEQ // S5_T
S5_HB = CTX_LEN // S5_T
S5_NB = S5_XB + S5_HB
S5_ROWS = S5_NB * BATCH
S5_RC = 128

VMEM_LIMIT = 48 * 1024 * 1024
VMEM_LIMIT_BIG = 56 * 1024 * 1024


def _cparams(sem, limit=VMEM_LIMIT):
    return pltpu.CompilerParams(dimension_semantics=sem, vmem_limit_bytes=limit)


def _sigmoid(x):
    return 1.0 / (1.0 + jnp.exp(-x))


def _silu(x):
    return x * _sigmoid(x)


def _log_sigmoid(x):
    return jnp.minimum(x, 0.0) - jnp.log1p(jnp.exp(-jnp.abs(x)))


def _gelu_tanh(x):
    c = math.sqrt(2.0 / math.pi)
    return 0.5 * x * (1.0 + jnp.tanh(c * (x + 0.044715 * (x * x * x))))


def _row_group(i, tm):
    return jnp.minimum((i * tm) // SEQ, CTX_GROUP)


def _mod_kernel(c_ref, w_ref, b_ref, o_ref):
    c = c_ref[...]
    lhs = _silu(c).astype(BF16)
    o_ref[...] = jnp.dot(lhs, w_ref[...].astype(BF16), preferred_element_type=F32) + b_ref[...]


def _modulation(cvec, w_mod, b_mod):
    tn = 512
    n = 6 * D_MODEL
    return pl.pallas_call(
        _mod_kernel,
        out_shape=jax.ShapeDtypeStruct((DEPTH, MOD_ROWS, n), F32),
        grid=(DEPTH, n // tn),
        in_specs=[
            pl.BlockSpec((MOD_ROWS, D_MODEL), lambda l, j: (0, 0)),
            pl.BlockSpec((None, D_MODEL, tn), lambda l, j: (l, 0, j)),
            pl.BlockSpec((None, 1, tn), lambda l, j: (l, 0, j)),
        ],
        out_specs=pl.BlockSpec((None, MOD_ROWS, tn), lambda l, j: (l, 0, j)),
        compiler_params=_cparams(("arbitrary", "arbitrary")),
        name="modulation",
    )(cvec, w_mod, b_mod.reshape(DEPTH, 1, n))


def _rms_modulate(x, nw, shift, scale):
    y = x * lax.rsqrt(jnp.mean(x * x, axis=-1, keepdims=True) + EPS) * nw
    return y * (1.0 + scale) + shift


def _in_proj_kernel(x_ref, nw_ref, shift_ref, scale_ref, w_ref, o_ref, lhs_scr):
    @pl.when(pl.program_id(1) == 0)
    def _():
        lhs_scr[...] = _rms_modulate(x_ref[...], nw_ref[...], shift_ref[...], scale_ref[...]).astype(BF16)

    o_ref[...] = jnp.dot(lhs_scr[...], w_ref[...], preferred_element_type=F32)


def _in_proj(xs, norm_w, mod3, w_in_p):
    tm, tn = 1024, 1024
    return pl.pallas_call(
        _in_proj_kernel,
        out_shape=jax.ShapeDtypeStruct((NT, IN_PAD), F32),
        grid=(NT // tm, IN_PAD // tn),
        in_specs=[
            pl.BlockSpec((tm, D_MODEL), lambda i, j: (i, 0)),
            pl.BlockSpec((1, D_MODEL), lambda i, j: (0, 0)),
            pl.BlockSpec((None, 1, D_MODEL), lambda i, j: (_row_group(i, tm), 0, 0)),
            pl.BlockSpec((None, 1, D_MODEL), lambda i, j: (_row_group(i, tm), 0, 1)),
            pl.BlockSpec((D_MODEL, tn), lambda i, j: (0, j)),
        ],
        out_specs=pl.BlockSpec((tm, tn), lambda i, j: (i, j)),
        scratch_shapes=[pltpu.VMEM((tm, D_MODEL), BF16)],
        compiler_params=_cparams(("arbitrary", "arbitrary")),
        name="in_proj",
    )(xs, norm_w.reshape(1, D_MODEL), mod3, mod3, w_in_p)


def _s5_prep(lam_re, lam_im, log_step, b_re, b_im, c_re, c_im, d_skip):
    hp = lax.Precision.HIGHEST
    lam_re = jnp.minimum(lam_re.astype(F32), -1e-4)
    lam_im = lam_im.astype(F32)
    step = jnp.exp(log_step.astype(F32))[..., None]
    mag = jnp.exp(lam_re * step)
    ab_re, ab_im = mag * jnp.cos(lam_im * step), mag * jnp.sin(lam_im * step)
    den = lam_re * lam_re + lam_im * lam_im
    ir, ii = lam_re / den, -lam_im / den
    nr = (ab_re - 1.0) * ir - ab_im * ii
    ni = (ab_re - 1.0) * ii + ab_im * ir
    bre, bim = b_re.astype(F32), b_im.astype(F32)
    bb_re = nr[..., None] * bre - ni[..., None] * bim
    bb_im = nr[..., None] * bim + ni[..., None] * bre
    lpow = jnp.arange(S5_T + 1, dtype=F32)[:, None, None, None]
    pmag = jnp.exp(lpow * (lam_re * step))
    pr, pi = pmag * jnp.cos(lpow * (lam_im * step)), pmag * jnp.sin(lpow * (lam_im * step))
    cre, cim = c_re.astype(F32), c_im.astype(F32)
    ca_r = cre[None] * pr[:, :, :, None, :] - cim[None] * pi[:, :, :, None, :]
    ca_i = cre[None] * pi[:, :, :, None, :] + cim[None] * pr[:, :, :, None, :]
    kl = (jnp.einsum('ldgnp,dgpm->ldgnm', ca_r, bb_re, precision=hp)
          - jnp.einsum('ldgnp,dgpm->ldgnm', ca_i, bb_im, precision=hp))
    tt = jnp.arange(S5_T)
    lag = tt[None, :] - tt[:, None]
    kf = jnp.where((lag >= 0)[:, :, None, None, None], kl[jnp.clip(lag, 0, S5_T), 0], 0.0)
    kb = jnp.where((lag <= 0)[:, :, None, None, None], kl[jnp.clip(-lag, 0, S5_T), 1], 0.0)
    nt = S5_TILES
    row = jnp.arange(S5_K)
    col = jnp.arange(S5_K)
    lane = jnp.arange(S5_LANES)
    g_blk = (row // S5_GROUP) % S5_LG
    g_state = 2 * (row // (4 * S5_STATE)) + (row // S5_STATE) % 2
    to_blk = ((lane[:, None] // S5_GROUP == col[None, :] // S5_LANES)
              & (lane[:, None] % S5_GROUP == col[None, :] % S5_GROUP)).astype(BF16)
    to_state = ((lane[:, None] // S5_STATE == (col[None, :] // S5_LANES) % 2)
                & (lane[:, None] % S5_STATE == col[None, :] % S5_STATE)).astype(BF16)

    def expand(compact, spread, g_row, g_col):
        wide = jnp.einsum('trk,kc->trc', compact.reshape(nt, S5_K, S5_LANES).astype(BF16), spread,
                          preferred_element_type=F32)
        return jnp.where(g_row[:, None] == g_col[None, :], wide, 0.0).astype(BF16)

    mc = (kf + kb).reshape(S5_T, S5_T, nt, S5_LG, S5_GROUP, S5_GROUP).transpose(2, 0, 3, 5, 1, 4)
    m = expand(mc, to_blk, g_blk, g_blk)

    def carry_in(d, powers):
        er = pr[powers, d][..., None] * bb_re[d][None] - pi[powers, d][..., None] * bb_im[d][None]
        ei = pr[powers, d][..., None] * bb_im[d][None] + pi[powers, d][..., None] * bb_re[d][None]
        arrange = lambda t: t.reshape(S5_T, nt, S5_LG, S5_STATE, S5_GROUP).transpose(1, 0, 2, 4, 3)
        es = jnp.stack([arrange(er), arrange(ei)], axis=4)
        return expand(es, to_state, g_blk, g_state)

    def read_out(d, powers):
        arrange = lambda t: t.reshape(S5_T, nt, S5_LG // 2, 2, S5_GROUP, S5_STATE).transpose(1, 2, 3, 5, 0, 4)
        fs = jnp.stack([arrange(ca_r[powers, d]), arrange(-ca_i[powers, d])], axis=2)
        return expand(fs, to_blk, g_state, g_blk)

    ef, eb = carry_in(0, S5_T - 1 - tt), carry_in(1, tt)
    ff, fb = read_out(0, tt + 1), read_out(1, S5_T - tt)
    lanes = lambda t: t.reshape(nt, S5_K // 2)
    a_r = jnp.concatenate([lanes(pr[S5_T, 0]), lanes(pr[S5_T, 1])], axis=-1)
    a_i = jnp.concatenate([lanes(pi[S5_T, 0]), lanes(pi[S5_T, 1])], axis=-1)
    dsk = jnp.pad(d_skip.astype(F32).reshape(nt, S5_LANES), ((0, 0), (0, S5_K - S5_LANES)))
    aux = jnp.stack([a_r, a_i, dsk] + [jnp.zeros_like(a_r)] * 5, axis=1)
    return ef, eb, m, ff, fb, aux


def _s5_kernel(a_ref, ef_ref, eb_ref, m_ref, ff_ref, fb_ref, aux_ref, y_ref, u_scr, wf_scr, wb_scr):
    for b in range(BATCH):
        for t in range(S5_T):
            u_scr.at[t][pl.ds(S5_HB * BATCH + b, S5_XB, stride=BATCH), :] = (
                a_ref[pl.ds(b * SEQ + t, S5_XB, stride=S5_T), :])
            u_scr.at[t][pl.ds(b, S5_HB, stride=BATCH), :] = a_ref[pl.ds(NX + b * CTX_LEN + t, S5_HB, stride=S5_T), :]

    def get_tiles(scr, r0):
        return jnp.concatenate([scr.at[t][pl.ds(r0, S5_RC), :] for t in range(S5_T)], axis=1).astype(BF16)

    def put_tiles(scr, r0, val):
        for t in range(S5_T):
            scr.at[t][pl.ds(r0, S5_RC), :] = val[:, t * S5_LANES:(t + 1) * S5_LANES]

    def block_rows(rc):
        r0 = pl.multiple_of(rc * S5_RC, S5_RC)
        return r0, get_tiles(u_scr, r0)

    def carry_in(rc, _):
        r0, ub = block_rows(rc)
        put_tiles(wf_scr, r0, jnp.dot(ub, ef_ref[...], preferred_element_type=F32))
        put_tiles(wb_scr, r0, jnp.dot(ub, eb_ref[...], preferred_element_type=F32))
        return 0

    lax.fori_loop(0, S5_ROWS // S5_RC, carry_in, 0)

    aux = aux_ref[...]
    npair = S5_LG // 2
    shape = (BATCH, S5_LANES)
    coef = []
    for d in range(2):
        for j in range(npair):
            lo = d * (S5_K // 2) + j * S5_LANES
            coef.append((jnp.broadcast_to(aux[0:1, lo:lo + S5_LANES], shape),
                         jnp.broadcast_to(aux[1:2, lo:lo + S5_LANES], shape)))

    def scan(s, carry):
        rf = pl.multiple_of(s * BATCH, BATCH)
        gb = jnp.where(s < S5_HB, S5_HB - 1 - s, S5_NB + S5_HB - 1 - s)
        rb = pl.multiple_of(gb * BATCH, BATCH)
        out = []
        for d, (w_scr, r) in enumerate(((wf_scr, rf), (wb_scr, rb))):
            for j in range(npair):
                xr, xi = carry[d * npair + j]
                ar, ai = coef[d * npair + j]
                w_re, w_im = w_scr.at[2 * j], w_scr.at[2 * j + 1]
                wr = w_re[pl.ds(r, BATCH), :]
                wi = w_im[pl.ds(r, BATCH), :]
                w_re[pl.ds(r, BATCH), :] = xr
                w_im[pl.ds(r, BATCH), :] = xi
                out.append((ar * xr - ai * xi + wr, ar * xi + ai * xr + wi))
        return tuple(out)

    z = jnp.zeros(shape, F32)
    lax.fori_loop(0, S5_NB, scan, tuple((z, z) for _ in range(2 * npair)))

    def read_out(rc, _):
        r0, ub = block_rows(rc)
        y = jnp.dot(ub, m_ref[...], preferred_element_type=F32)
        y = y + jnp.dot(get_tiles(wf_scr, r0), ff_ref[...], preferred_element_type=F32)
        y = y + jnp.dot(get_tiles(wb_scr, r0), fb_ref[...], preferred_element_type=F32)
        put_tiles(u_scr, r0, y)
        return 0

    lax.fori_loop(0, S5_ROWS // S5_RC, read_out, 0)

    for b in range(BATCH):
        for t in range(S5_T):
            y_ref[pl.ds(b * SEQ + t, S5_XB, stride=S5_T), :] = (
                u_scr.at[t][pl.ds(S5_HB * BATCH + b, S5_XB, stride=BATCH), :])
            y_ref[pl.ds(NX + b * CTX_LEN + t, S5_HB, stride=S5_T), :] = u_scr.at[t][pl.ds(b, S5_HB, stride=BATCH), :]
    y_ref[...] = y_ref[...] + a_ref[...] * aux[2:3, 0:S5_LANES]


def _s5_mix(a, layer, ef, eb, m, ff, fb, aux):
    wspec = lambda: pl.BlockSpec((None, None, S5_K, S5_K), lambda g: (layer, g, 0, 0),
                                 pipeline_mode=pl.Buffered(1))
    return pl.pallas_call(
        _s5_kernel,
        out_shape=jax.ShapeDtypeStruct((NT, S5_WIDTH), F32),
        grid=(S5_TILES,),
        in_specs=[
            pl.BlockSpec((NT, S5_LANES), lambda g: (0, COL_S5 // S5_LANES + g)),
            wspec(), wspec(), wspec(), wspec(), wspec(),
            pl.BlockSpec((None, None, 8, S5_K), lambda g: (layer, g, 0, 0)),
        ],
        out_specs=pl.BlockSpec((NT, S5_LANES), lambda g: (0, g)),
        scratch_shapes=[pltpu.VMEM((S5_T, S5_ROWS, S5_LANES), F32)] * 3,
        compiler_params=_cparams(("arbitrary",), VMEM_LIMIT_BIG),
        name="s5_scan",
    )(a, ef, eb, m, ff, fb, aux)


def _glu_kernel(y_ref, w_ref, b_ref, o_ref):
    g = _gelu_tanh(y_ref[...]).astype(BF16)
    z = jnp.dot(g, w_ref[...], preferred_element_type=F32) + b_ref[...]
    o_ref[...] = z[:, :S5_WIDTH] * _sigmoid(z[:, S5_WIDTH:])


def _s5_glu(y, w_glu, b_glu):
    tm = 1024
    return pl.pallas_call(
        _glu_kernel,
        out_shape=jax.ShapeDtypeStruct((NT, S5_WIDTH), F32),
        grid=(NT // tm,),
        in_specs=[
            pl.BlockSpec((tm, S5_WIDTH), lambda i: (i, 0)),
            pl.BlockSpec((S5_WIDTH, 2 * S5_WIDTH), lambda i: (0, 0)),
            pl.BlockSpec((1, 2 * S5_WIDTH), lambda i: (0, 0)),
        ],
        out_specs=pl.BlockSpec((tm, S5_WIDTH), lambda i: (i, 0)),
        compiler_params=_cparams(("arbitrary",)),
        name="s5_glu",
    )(y, w_glu.astype(BF16), b_glu.reshape(1, 2 * S5_WIDTH))


def _chunk_index(d, s):
    is_ctx = s < H_CHUNKS
    fwd = jnp.where(is_ctx, s, s - H_CHUNKS)
    bwd = jnp.where(is_ctx, H_CHUNKS - 1 - s, MIX_STEPS - 1 - s)
    return is_ctx, jnp.where(d == 0, fwd, bwd)


def _token_block(b, d, s):
    is_ctx, c = _chunk_index(d, s)
    return jnp.where(is_ctx, NX // CHUNK + b * H_CHUNKS + c, b * X_CHUNKS + c)


def _rope_block(b, d, s):
    is_ctx, c = _chunk_index(d, s)
    return jnp.where(is_ctx, X_CHUNKS + c, c)


def _dot_nt(a, b):
    return lax.dot_general(a, b, (((1,), (1,)), ((), ())), preferred_element_type=F32)


def _dot_tn(a, b):
    return lax.dot_general(a, b, (((0,), (0,)), ((), ())), preferred_element_type=F32)


def _dot(a, b):
    return jnp.dot(a, b, preferred_element_type=F32)


def _head(ref, h):
    return ref[:, h * HEAD_DIM:(h + 1) * HEAD_DIM]


def _chunk_pos(rev):
    i = lax.broadcasted_iota(jnp.int32, (CHUNK, CHUNK), 0)
    j = lax.broadcasted_iota(jnp.int32, (CHUNK, CHUNK), 1)
    pi = jnp.where(rev, CHUNK - 1 - i, i)
    pj = jnp.where(rev, CHUNK - 1 - j, j)
    return pi, pj


def _ret_kernel(dl_ref, q_ref, k_ref, v_ref, cos_ref, sin_ref, o_ref, r_scr, dec_scr, qd_scr, kd_scr, cd_scr):
    d = pl.program_id(1)
    s = pl.program_id(2)
    rev = d == 1

    @pl.when(s == 0)
    def _():
        pi, pj = _chunk_pos(rev)
        rel = (pi - pj).astype(F32)
        pif = pi.astype(F32)
        for h in range(HEADS):
            lg = _log_sigmoid(dl_ref[h])[0:1, :]
            dec_scr[h] = jnp.where(rel >= 0, jnp.exp(lg * jnp.maximum(rel, 0.0)), 0.0)
            qd_scr[h] = jnp.exp(lg * (pif + 1.0))
            kd_scr[h] = jnp.exp(lg * (CHUNK - 1.0 - pif))
            cd_scr[h] = jnp.exp(jnp.broadcast_to(lg, (8, HEAD_DIM)) * CHUNK)
        r_scr[...] = jnp.zeros_like(r_scr)

    cos = cos_ref[...]
    sin = sin_ref[...]
    scale = HEAD_DIM ** -0.5
    hs = range(HEADS)
    q = [_head(q_ref, h) for h in hs]
    k = [_head(k_ref, h) for h in hs]
    v = [_head(v_ref, h).astype(BF16) for h in hs]
    q = [q[h] * cos + pltpu.roll(q[h], HEAD_DIM // 2, 1) * sin for h in hs]
    k = [(k[h] * cos + pltpu.roll(k[h], HEAD_DIM // 2, 1) * sin) * scale for h in hs]
    r = [r_scr[h] for h in hs]
    sm = [_dot_nt(q[h].astype(BF16), k[h].astype(BF16)) * dec_scr[h] for h in hs]
    inter = [_dot((q[h] * qd_scr[h]).astype(BF16), r[h].astype(BF16)) for h in hs]
    kv = [_dot_tn((k[h] * kd_scr[h]).astype(BF16), v[h]) for h in hs]
    for h in hs:
        o_ref[:, h * HEAD_DIM:(h + 1) * HEAD_DIM] = _dot(sm[h].astype(BF16), v[h]) + inter[h]
        r_scr[h] = cd_scr[h][0:1, :] * r[h] + kv[h]


def _retention(a, rope_cos, rope_sin, decay_logit):
    dl = jnp.broadcast_to(decay_logit.astype(F32)[:, :, None, None], (2, HEADS, 8, HEAD_DIM))
    tok = lambda col: pl.BlockSpec((CHUNK, RET_WIDTH), lambda b, d, s: (_token_block(b, d, s), col))
    rope = pl.BlockSpec((CHUNK, HEAD_DIM), lambda b, d, s: (_rope_block(b, d, s), 0))
    hh = pltpu.VMEM((HEADS, CHUNK, HEAD_DIM), F32)
    return pl.pallas_call(
        _ret_kernel,
        out_shape=jax.ShapeDtypeStruct((2, NT, RET_WIDTH), F32),
        grid=(BATCH, 2, MIX_STEPS),
        in_specs=[
            pl.BlockSpec((None, HEADS, 8, HEAD_DIM), lambda b, d, s: (d, 0, 0, 0)),
            tok(COL_RET // RET_WIDTH), tok(COL_RET // RET_WIDTH + 1), tok(COL_RET // RET_WIDTH + 2),
            rope, rope,
        ],
        out_specs=pl.BlockSpec((None, CHUNK, RET_WIDTH), lambda b, d, s: (d, _token_block(b, d, s), 0)),
        scratch_shapes=[hh, hh, hh, hh, pltpu.VMEM((HEADS, 8, HEAD_DIM), F32)],
        compiler_params=_cparams(("arbitrary", "arbitrary", "arbitrary")),
        name="retention",
    )(dl, a, a, a, rope_cos, rope_sin)


def _split3(x):
    h = x.astype(BF16)
    r = x - h.astype(F32)
    m = r.astype(BF16)
    return h, m, (r - m.astype(F32)).astype(BF16)


def _mlstm_kernel(q_ref, k_ref, v_ref, g_ref, gb_ref, o_ref, c_scr, n_scr, m_scr):
    d = pl.program_id(1)
    s = pl.program_id(2)
    rev = d == 1

    @pl.when(s == 0)
    def _():
        c_scr[...] = jnp.zeros_like(c_scr)
        n_scr[...] = jnp.zeros_like(n_scr)
        m_scr[...] = jnp.zeros_like(m_scr)

    pi, pj = _chunk_pos(rev)
    causal = pj <= pi
    tri = jnp.where(causal, 1.0, 0.0).astype(BF16)
    pre = g_ref[...] + gb_ref[...]
    pre_t = pre.T[0:32, :]
    lf_c = _log_sigmoid(pre)
    lf_r = _log_sigmoid(pre_t)
    b_cols = sum(_dot(tri, part) for part in _split3(lf_c))
    b_rows = sum(_dot_nt(part, tri) for part in _split3(lf_r))
    scale = HEAD_DIM ** -0.5
    neg_inf = -jnp.inf
    hs = range(HEADS)
    pick = lambda t, kind, h, rows: jnp.where(rev, rows(t, (2 + kind) * HEADS + h), rows(t, kind * HEADS + h))
    as_col = lambda t, c: t[:, c:c + 1]
    as_row = lambda t, c: t[c:c + 1, :]
    i_col = [pick(pre, 0, h, as_col) for h in hs]
    b_col = [pick(b_cols, 1, h, as_col) for h in hs]
    i_row = [pick(pre_t, 0, h, as_row) for h in hs]
    b_row = [pick(b_rows, 1, h, as_row) for h in hs]
    b_end = [jnp.where(rev, b_cols[0:1, 3 * HEADS + h:3 * HEADS + h + 1],
                       b_cols[CHUNK - 1:CHUNK, HEADS + h:HEADS + h + 1]) for h in hs]
    m_prev = [m_scr[h][0:1, 0:1] for h in hs]
    q = [_head(q_ref, h) for h in hs]
    k = [_head(k_ref, h) * scale for h in hs]
    v = [_head(v_ref, h).astype(BF16) for h in hs]
    qb = [q[h].astype(BF16) for h in hs]
    c_mem = [c_scr[h] for h in hs]
    n_mem = [n_scr[h][0:1, :] for h in hs]
    log_w = [jnp.where(causal, b_col[h] - b_row[h] + i_row[h], neg_inf) for h in hs]
    log_a = [b_col[h] + m_prev[h] for h in hs]
    m_t = [jnp.maximum(log_a[h], jnp.max(log_w[h], axis=-1, keepdims=True)) for h in hs]
    w = [jnp.exp(log_w[h] - m_t[h]) for h in hs]
    a = [jnp.exp(log_a[h] - m_t[h]) for h in hs]
    sm = [_dot_nt(qb[h], k[h].astype(BF16)) * w[h] for h in hs]
    inter = [_dot(qb[h], c_mem[h].astype(BF16)) for h in hs]
    num = [_dot(sm[h].astype(BF16), v[h]) + a[h] * inter[h] for h in hs]
    den = [jnp.sum(sm[h] + a[h] * (q[h] * n_mem[h]), axis=-1, keepdims=True) for h in hs]
    for h in hs:
        o_ref[:, h * HEAD_DIM:(h + 1) * HEAD_DIM] = num[h] / jnp.maximum(jnp.abs(den[h]), jnp.exp(-m_t[h]))
    log_w_end = [b_end[h] - b_col[h] + i_col[h] for h in hs]
    m_new = [jnp.maximum(b_end[h] + m_prev[h], jnp.max(log_w_end[h], axis=0, keepdims=True)) for h in hs]
    a_end = [jnp.exp(b_end[h] + m_prev[h] - m_new[h]) for h in hs]
    kw = [k[h] * jnp.exp(log_w_end[h] - m_new[h]) for h in hs]
    kv = [_dot_tn(kw[h].astype(BF16), v[h]) for h in hs]
    for h in hs:
        c_scr[h] = a_end[h] * c_mem[h] + kv[h]
        n_scr[h] = jnp.broadcast_to(a_end[h] * n_mem[h] + jnp.sum(kw[h], axis=0, keepdims=True), (8, HEAD_DIM))
        m_scr[h] = jnp.broadcast_to(m_new[h], (8, HEAD_DIM))


def _mlstm(a, igate_b, fgate_b):
    gb = jnp.stack([igate_b.astype(F32), fgate_b.astype(F32)], axis=1).reshape(1, 4 * HEADS)
    gb = jnp.pad(gb, ((0, 0), (0, HEAD_DIM - 4 * HEADS)))
    col0 = COL_ML // MLSTM_WIDTH
    tok = lambda col: pl.BlockSpec((CHUNK, MLSTM_WIDTH), lambda b, d, s: (_token_block(b, d, s), col))
    small = pltpu.VMEM((HEADS, 8, HEAD_DIM), F32)
    return pl.pallas_call(
        _mlstm_kernel,
        out_shape=jax.ShapeDtypeStruct((2, NT, MLSTM_WIDTH), F32),
        grid=(BATCH, 2, MIX_STEPS),
        in_specs=[
            tok(col0), tok(col0 + 1), tok(col0 + 2),
            pl.BlockSpec((CHUNK, HEAD_DIM), lambda b, d, s: (_token_block(b, d, s), COL_GATE // HEAD_DIM)),
            pl.BlockSpec((1, HEAD_DIM), lambda b, d, s: (0, 0)),
        ],
        out_specs=pl.BlockSpec((None, CHUNK, MLSTM_WIDTH), lambda b, d, s: (d, _token_block(b, d, s), 0)),
        scratch_shapes=[pltpu.VMEM((HEADS, CHUNK, HEAD_DIM), F32), small, small],
        compiler_params=_cparams(("arbitrary", "arbitrary", "arbitrary")),
        name="mlstm",
    )(a, a, a, a, gb)


def _head_norm(o, center):
    if center:
        o = o - jnp.mean(o, axis=-1, keepdims=True)
    return o * lax.rsqrt(jnp.mean(o * o, axis=-1, keepdims=True) + EPS)


def _out_proj_kernel(s5_ref, ro_ref, rg_ref, rw_ref, mo_ref, mg_ref, mw_ref, x_ref, gate_ref, w_ref, o_ref):
    pair = 2 * HEAD_DIM
    acc = jnp.dot(s5_ref[:, 0:pair].astype(BF16), w_ref[0:pair, :], preferred_element_type=F32)
    acc += jnp.dot(s5_ref[:, pair:2 * pair].astype(BF16), w_ref[pair:2 * pair, :], preferred_element_type=F32)
    for hp in range(HEADS // 2):
        rs, ms = [], []
        for h in (2 * hp, 2 * hp + 1):
            sl = slice(h * HEAD_DIM, (h + 1) * HEAD_DIM)
            r = _head_norm(ro_ref[0, :, sl] + ro_ref[1, :, sl], True) * rw_ref[:, sl] * _silu(rg_ref[:, sl])
            m = _head_norm(mo_ref[0, :, sl] + mo_ref[1, :, sl], False) * mw_ref[:, sl] * _sigmoid(mg_ref[:, sl])
            rs.append(r.astype(BF16))
            ms.append(m.astype(BF16))
        r0 = S5_WIDTH + hp * pair
        m0 = S5_WIDTH + RET_WIDTH + hp * pair
        acc += jnp.dot(jnp.concatenate(rs, axis=1), w_ref[r0:r0 + pair, :], preferred_element_type=F32)
        acc += jnp.dot(jnp.concatenate(ms, axis=1), w_ref[m0:m0 + pair, :], preferred_element_type=F32)
    o_ref[...] = x_ref[...] + gate_ref[...] * acc


def _out_proj(rows, s5y, ret_o, ml_o, a, ret_norm_w, mlstm_norm_w, xs, mod3, w_out):
    tm = 256
    return pl.pallas_call(
        _out_proj_kernel,
        out_shape=jax.ShapeDtypeStruct((rows, D_MODEL), F32),
        grid=(rows // tm,),
        in_specs=[
            pl.BlockSpec((tm, S5_WIDTH), lambda i: (i, 0)),
            pl.BlockSpec((2, tm, RET_WIDTH), lambda i: (0, i, 0)),
            pl.BlockSpec((tm, RET_WIDTH), lambda i: (i, COL_RET // RET_WIDTH + 3)),
            pl.BlockSpec((1, RET_WIDTH), lambda i: (0, 0)),
            pl.BlockSpec((2, tm, MLSTM_WIDTH), lambda i: (0, i, 0)),
            pl.BlockSpec((tm, MLSTM_WIDTH), lambda i: (i, COL_ML // MLSTM_WIDTH + 3)),
            pl.BlockSpec((1, MLSTM_WIDTH), lambda i: (0, 0)),
            pl.BlockSpec((tm, D_MODEL), lambda i: (i, 0)),
            pl.BlockSpec((None, 1, D_MODEL), lambda i: (_row_group(i, tm), 0, 2)),
            pl.BlockSpec((D_MODEL, D_MODEL), lambda i: (0, 0), pipeline_mode=pl.Buffered(1)),
        ],
        out_specs=pl.BlockSpec((tm, D_MODEL), lambda i: (i, 0)),
        compiler_params=_cparams(("arbitrary",)),
        name="out_proj",
    )(s5y, ret_o, a, ret_norm_w.reshape(1, RET_WIDTH), ml_o, a, mlstm_norm_w.reshape(1, MLSTM_WIDTH),
      xs, mod3, w_out)


def _mlp_kernel(x_ref, nw_ref, shift_ref, scale_ref, gate_ref, w1_ref, w2_ref, nf_ref, o_ref, lhs_scr,
                *, final_norm):
    f = pl.program_id(1)

    @pl.when(f == 0)
    def _():
        lhs_scr[...] = _rms_modulate(x_ref[...], nw_ref[...], shift_ref[...], scale_ref[...]).astype(BF16)
        o_ref[...] = jnp.zeros_like(o_ref)

    hid = jnp.dot(lhs_scr[...], w1_ref[...].astype(BF16), preferred_element_type=F32)
    hid = jnp.square(jnp.maximum(hid, 0.0)).astype(BF16)
    o_ref[...] += jnp.dot(hid, w2_ref[...].astype(BF16), preferred_element_type=F32)

    @pl.when(f == pl.num_programs(1) - 1)
    def _():
        y = x_ref[...] + gate_ref[...] * o_ref[...]
        if final_norm:
            y = y * lax.rsqrt(jnp.mean(y * y, axis=-1, keepdims=True) + EPS) * nf_ref[...]
        o_ref[...] = y


def _mlp(rows, xs, norm_w, mod3, w1, w2, layer, norm_f_w, final_norm):
    tm, tf = 1024, 512
    modspec = lambda k: pl.BlockSpec((None, 1, D_MODEL), lambda i, f: (_row_group(i, tm), 0, k))
    once = pl.Buffered(1)
    return pl.pallas_call(
        functools.partial(_mlp_kernel, final_norm=final_norm),
        out_shape=jax.ShapeDtypeStruct((rows, D_MODEL), F32),
        grid=(rows // tm, D_FF // tf),
        in_specs=[
            pl.BlockSpec((tm, D_MODEL), lambda i, f: (i, 0)),
            pl.BlockSpec((1, D_MODEL), lambda i, f: (0, 0)),
            modspec(3), modspec(4), modspec(5),
            pl.BlockSpec((None, D_MODEL, tf), lambda i, f: (layer, 0, f)),
            pl.BlockSpec((None, tf, D_MODEL), lambda i, f: (layer, f, 0)),
            pl.BlockSpec((1, D_MODEL), lambda i, f: (0, 0)),
        ],
        out_specs=pl.BlockSpec((tm, D_MODEL), lambda i, f: (i, 0), pipeline_mode=once),
        scratch_shapes=[pltpu.VMEM((tm, D_MODEL), BF16)],
        compiler_params=_cparams(("arbitrary", "arbitrary"), VMEM_LIMIT_BIG),
        name="mlp",
    )(xs, norm_w.reshape(1, D_MODEL), mod3, mod3, mod3, w1, w2, norm_f_w.reshape(1, D_MODEL))


def _rope_tables():
    quarter = HEAD_DIM // 4
    rows = jnp.repeat(jnp.arange(SEQ // GRID_W, dtype=F32), GRID_W)
    cols = jnp.tile(jnp.arange(GRID_W, dtype=F32), SEQ // GRID_W)
    inv = ROPE_BASE ** (-jnp.arange(quarter, dtype=F32) / quarter)
    ang = jnp.concatenate([rows[:, None] * inv, cols[:, None] * inv], axis=-1)
    cos, sin = jnp.cos(ang), jnp.sin(ang)
    cos2 = jnp.concatenate([cos, cos], axis=-1)
    sin2 = jnp.concatenate([-sin, sin], axis=-1)
    cos2 = jnp.concatenate([cos2, jnp.ones((CTX_LEN, HEAD_DIM), F32)], axis=0)
    sin2 = jnp.concatenate([sin2, jnp.zeros((CTX_LEN, HEAD_DIM), F32)], axis=0)
    return cos2, sin2


def _permute_w_in(w):
    s5 = w[:, :S5_WIDTH]
    ret = w[:, S5_WIDTH:S5_WIDTH + 4 * RET_WIDTH]
    ml = w[:, S5_WIDTH + 4 * RET_WIDTH:S5_WIDTH + 4 * RET_WIDTH + 4 * MLSTM_WIDTH]
    gates = w[:, S5_WIDTH + 4 * RET_WIDTH + 4 * MLSTM_WIDTH:]
    pad = jnp.zeros((D_MODEL, IN_PAD - IN_WIDTH), w.dtype)
    return jnp.concatenate([ret, ml, s5, gates, pad], axis=1).astype(BF16)


def kernel(x, c, ctx, c_ctx, w_mod, b_mod, norm1_w, norm2_w, w_in, w_out, s5_lam_re, s5_lam_im, s5_log_step, s5_b_re, s5_b_im, s5_c_re, s5_c_im, s5_d, s5_w_glu, s5_b_glu, ret_decay_logit, ret_norm_w, mlstm_igate_b, mlstm_fgate_b, mlstm_norm_w, w_ff1, w_ff2, norm_f_w):
    rope_cos, rope_sin = _rope_tables()
    cvec = jnp.concatenate([c, c_ctx[None, :], jnp.zeros((MOD_ROWS - BATCH - 1, D_MODEL), F32)], axis=0)
    mod = _modulation(cvec, w_mod, b_mod)
    xs = jnp.concatenate([x.reshape(NX, D_MODEL), ctx.reshape(NH, D_MODEL)], axis=0)
    s5w = jax.vmap(_s5_prep)(s5_lam_re, s5_lam_im, s5_log_step, s5_b_re, s5_b_im, s5_c_re, s5_c_im, s5_d)
    for l in range(DEPTH):
        last = l == DEPTH - 1
        mod3 = mod[l].reshape(MOD_ROWS, 1, 6 * D_MODEL)
        a = _in_proj(xs, norm1_w[l], mod3, _permute_w_in(w_in[l]))
        s5y = _s5_glu(_s5_mix(a, l, *s5w), s5_w_glu[l], s5_b_glu[l])
        ret_o = _retention(a, rope_cos, rope_sin, ret_decay_logit[l])
        ml_o = _mlstm(a, mlstm_igate_b[l], mlstm_fgate_b[l])
        rows = NX if last else NT
        xs = _out_proj(rows, s5y, ret_o, ml_o, a, ret_norm_w[l], mlstm_norm_w[l], xs, mod3,
                       w_out[l].astype(BF16))
        xs = _mlp(rows, xs, norm2_w[l], mod3, w_ff1, w_ff2, l, norm_f_w, last)
    return xs.reshape(BATCH, SEQ, D_MODEL)
```

```python
import functools
import math

import jax
import jax.numpy as jnp
from jax import lax
from jax.experimental import pallas as pl
from jax.experimental.pallas import tpu as pltpu

F32 = jnp.float32
BF16 = jnp.bfloat16

D_MODEL = 2048
BATCH = 4
SEQ = 2048
DEPTH = 2
CTX_LEN = 256
GRID_W = 64
HEAD_DIM = 128
S5_WIDTH = 512
S5_GROUP = 16
S5_GROUPS = 32
S5_STATE = 64
RET_WIDTH = 768
MLSTM_WIDTH = 768
HEADS = 6
IN_WIDTH = S5_WIDTH + 4 * RET_WIDTH + 4 * MLSTM_WIDTH + 4 * HEADS
D_FF = 4 * D_MODEL
CHUNK = 128
ROPE_BASE = 10000.0
EPS = 1e-6

NX = BATCH * SEQ
NH = BATCH * CTX_LEN
NT = NX + NH
MOD_ROWS = 8
CTX_GROUP = BATCH

COL_RET = 0
COL_ML = 4 * RET_WIDTH
COL_S5 = COL_ML + 4 * MLSTM_WIDTH
COL_GATE = COL_S5 + S5_WIDTH
IN_PAD = 7168

X_CHUNKS = SEQ // CHUNK
H_CHUNKS = CTX_LEN // CHUNK
MIX_STEPS = X_CHUNKS + H_CHUNKS

S5_T = 8
S5_LANES = 128
S5_LG = S5_LANES // S5_GROUP
S5_TILES = S5_WIDTH // S5_LANES
S5_K = S5_T * S5_LANES
S5_XB = SEQ // S5_T
S5_HB = CTX_LEN // S5_T
S5_NB = S5_XB + S5_HB
S5_ROWS = S5_NB * BATCH
S5_RC = 128

VMEM_LIMIT = 48 * 1024 * 1024
VMEM_LIMIT_BIG = 56 * 1024 * 1024


def _cparams(sem, limit=VMEM_LIMIT):
    return pltpu.CompilerParams(dimension_semantics=sem, vmem_limit_bytes=limit)


def _sigmoid(x):
    return 1.0 / (1.0 + jnp.exp(-x))


def _silu(x):
    return x * _sigmoid(x)


def _log_sigmoid(x):
    return jnp.minimum(x, 0.0) - jnp.log1p(jnp.exp(-jnp.abs(x)))


def _gelu_tanh(x):
    c = math.sqrt(2.0 / math.pi)
    return 0.5 * x * (1.0 + jnp.tanh(c * (x + 0.044715 * (x * x * x))))


def _row_group(i, tm):
    return jnp.minimum((i * tm) // SEQ, CTX_GROUP)


def _mod_kernel(c_ref, w_ref, b_ref, o_ref):
    c = c_ref[...]
    lhs = _silu(c).astype(BF16)
    o_ref[...] = jnp.dot(lhs, w_ref[...].astype(BF16), preferred_element_type=F32) + b_ref[...]


def _modulation(cvec, w_mod, b_mod):
    tn = 512
    n = 6 * D_MODEL
    return pl.pallas_call(
        _mod_kernel,
        out_shape=jax.ShapeDtypeStruct((DEPTH, MOD_ROWS, n), F32),
        grid=(DEPTH, n // tn),
        in_specs=[
            pl.BlockSpec((MOD_ROWS, D_MODEL), lambda l, j: (0, 0)),
            pl.BlockSpec((None, D_MODEL, tn), lambda l, j: (l, 0, j)),
            pl.BlockSpec((None, 1, tn), lambda l, j: (l, 0, j)),
        ],
        out_specs=pl.BlockSpec((None, MOD_ROWS, tn), lambda l, j: (l, 0, j)),
        compiler_params=_cparams(("arbitrary", "arbitrary")),
        name="modulation",
    )(cvec, w_mod, b_mod.reshape(DEPTH, 1, n))


def _rms_modulate(x, nw, shift, scale):
    y = x * lax.rsqrt(jnp.mean(x * x, axis=-1, keepdims=True) + EPS) * nw
    return y * (1.0 + scale) + shift


def _in_proj_kernel(x_ref, nw_ref, shift_ref, scale_ref, w_ref, o_ref, lhs_scr):
    @pl.when(pl.program_id(1) == 0)
    def _():
        lhs_scr[...] = _rms_modulate(x_ref[...], nw_ref[...], shift_ref[...], scale_ref[...]).astype(BF16)

    o_ref[...] = jnp.dot(lhs_scr[...], w_ref[...], preferred_element_type=F32)


def _in_proj(xs, norm_w, mod3, w_in_p):
    tm, tn = 1024, 1024
    return pl.pallas_call(
        _in_proj_kernel,
        out_shape=jax.ShapeDtypeStruct((NT, IN_PAD), F32),
        grid=(NT // tm, IN_PAD // tn),
        in_specs=[
            pl.BlockSpec((tm, D_MODEL), lambda i, j: (i, 0)),
            pl.BlockSpec((1, D_MODEL), lambda i, j: (0, 0)),
            pl.BlockSpec((None, 1, D_MODEL), lambda i, j: (_row_group(i, tm), 0, 0)),
            pl.BlockSpec((None, 1, D_MODEL), lambda i, j: (_row_group(i, tm), 0, 1)),
            pl.BlockSpec((D_MODEL, tn), lambda i, j: (0, j)),
        ],
        out_specs=pl.BlockSpec((tm, tn), lambda i, j: (i, j)),
        scratch_shapes=[pltpu.VMEM((tm, D_MODEL), BF16)],
        compiler_params=_cparams(("arbitrary", "arbitrary")),
        name="in_proj",
    )(xs, norm_w.reshape(1, D_MODEL), mod3, mod3, w_in_p)


def _s5_prep(lam_re, lam_im, log_step, b_re, b_im, c_re, c_im, d_skip):
    hp = lax.Precision.HIGHEST
    lam_re = jnp.minimum(lam_re.astype(F32), -1e-4)
    lam_im = lam_im.astype(F32)
    step = jnp.exp(log_step.astype(F32))[..., None]
    mag = jnp.exp(lam_re * step)
    ab_re, ab_im = mag * jnp.cos(lam_im * step), mag * jnp.sin(lam_im * step)
    den = lam_re * lam_re + lam_im * lam_im
    ir, ii = lam_re / den, -lam_im / den
    nr = (ab_re - 1.0) * ir - ab_im * ii
    ni = (ab_re - 1.0) * ii + ab_im * ir
    bre, bim = b_re.astype(F32), b_im.astype(F32)
    bb_re = nr[..., None] * bre - ni[..., None] * bim
    bb_im = nr[..., None] * bim + ni[..., None] * bre
    lpow = jnp.arange(S5_T + 1, dtype=F32)[:, None, None, None]
    pmag = jnp.exp(lpow * (lam_re * step))
    pr, pi = pmag * jnp.cos(lpow * (lam_im * step)), pmag * jnp.sin(lpow * (lam_im * step))
    cre, cim = c_re.astype(F32), c_im.astype(F32)
    ca_r = cre[None] * pr[:, :, :, None, :] - cim[None] * pi[:, :, :, None, :]
    ca_i = cre[None] * pi[:, :, :, None, :] + cim[None] * pr[:, :, :, None, :]
    kl = (jnp.einsum('ldgnp,dgpm->ldgnm', ca_r, bb_re, precision=hp)
          - jnp.einsum('ldgnp,dgpm->ldgnm', ca_i, bb_im, precision=hp))
    tt = jnp.arange(S5_T)
    lag = tt[None, :] - tt[:, None]
    kf = jnp.where((lag >= 0)[:, :, None, None, None], kl[jnp.clip(lag, 0, S5_T), 0], 0.0)
    kb = jnp.where((lag <= 0)[:, :, None, None, None], kl[jnp.clip(-lag, 0, S5_T), 1], 0.0)
    nt = S5_TILES
    row = jnp.arange(S5_K)
    col = jnp.arange(S5_K)
    lane = jnp.arange(S5_LANES)
    g_blk = (row // S5_GROUP) % S5_LG
    g_state = 2 * (row // (4 * S5_STATE)) + (row // S5_STATE) % 2
    to_blk = ((lane[:, None] // S5_GROUP == col[None, :] // S5_LANES)
              & (lane[:, None] % S5_GROUP == col[None, :] % S5_GROUP)).astype(BF16)
    to_state = ((lane[:, None] // S5_STATE == (col[None, :] // S5_LANES) % 2)
                & (lane[:, None] % S5_STATE == col[None, :] % S5_STATE)).astype(BF16)

    def expand(compact, spread, g_row, g_col):
        wide = jnp.einsum('trk,kc->trc', compact.reshape(nt, S5_K, S5_LANES).astype(BF16), spread,
                          preferred_element_type=BF16)
        return jnp.where(g_row[:, None] == g_col[None, :], wide, jnp.zeros((), BF16))

    mc = (kf + kb).reshape(S5_T, S5_T, nt, S5_LG, S5_GROUP, S5_GROUP).transpose(2, 0, 3, 5, 1, 4)
    m = expand(mc, to_blk, g_blk, g_blk)

    def carry_in(d, powers):
        er = pr[powers, d][..., None] * bb_re[d][None] - pi[powers, d][..., None] * bb_im[d][None]
        ei = pr[powers, d][..., None] * bb_im[d][None] + pi[powers, d][..., None] * bb_re[d][None]
        arrange = lambda t: t.reshape(S5_T, nt, S5_LG, S5_STATE, S5_GROUP).transpose(1, 0, 2, 4, 3)
        es = jnp.stack([arrange(er), arrange(ei)], axis=4)
        return expand(es, to_state, g_blk, g_state)

    def read_out(d, powers):
        arrange = lambda t: t.reshape(S5_T, nt, S5_LG // 2, 2, S5_GROUP, S5_STATE).transpose(1, 2, 3, 5, 0, 4)
        fs = jnp.stack([arrange(ca_r[powers, d]), arrange(-ca_i[powers, d])], axis=2)
        return expand(fs, to_blk, g_state, g_blk)

    ef, eb = carry_in(0, S5_T - 1 - tt), carry_in(1, tt)
    ff, fb = read_out(0, tt + 1), read_out(1, S5_T - tt)
    lanes = lambda t: t.reshape(nt, S5_K // 2)
    a_r = jnp.concatenate([lanes(pr[S5_T, 0]), lanes(pr[S5_T, 1])], axis=-1)
    a_i = jnp.concatenate([lanes(pi[S5_T, 0]), lanes(pi[S5_T, 1])], axis=-1)
    dsk = jnp.pad(d_skip.astype(F32).reshape(nt, S5_LANES), ((0, 0), (0, S5_K - S5_LANES)))
    aux = jnp.stack([a_r, a_i, dsk] + [jnp.zeros_like(a_r)] * 5, axis=1)
    return ef, eb, m, ff, fb, aux


def _s5_kernel(a_ref, ef_ref, eb_ref, m_ref, ff_ref, fb_ref, aux_ref, y_ref, u_scr, wf_scr, wb_scr):
    for b in range(BATCH):
        for t in range(S5_T):
            u_scr.at[t][pl.ds(S5_HB * BATCH + b, S5_XB, stride=BATCH), :] = (
                a_ref[pl.ds(b * SEQ + t, S5_XB, stride=S5_T), :])
            u_scr.at[t][pl.ds(b, S5_HB, stride=BATCH), :] = a_ref[pl.ds(NX + b * CTX_LEN + t, S5_HB, stride=S5_T), :]

    def get_tiles(scr, r0):
        return jnp.concatenate([scr.at[t][pl.ds(r0, S5_RC), :] for t in range(S5_T)], axis=1).astype(BF16)

    def put_tiles(scr, r0, val):
        for t in range(S5_T):
            scr.at[t][pl.ds(r0, S5_RC), :] = val[:, t * S5_LANES:(t + 1) * S5_LANES]

    def block_rows(rc):
        r0 = pl.multiple_of(rc * S5_RC, S5_RC)
        return r0, get_tiles(u_scr, r0)

    def carry_in(rc, _):
        r0, ub = block_rows(rc)
        put_tiles(wf_scr, r0, jnp.dot(ub, ef_ref[...], preferred_element_type=F32))
        put_tiles(wb_scr, r0, jnp.dot(ub, eb_ref[...], preferred_element_type=F32))
        return 0

    lax.fori_loop(0, S5_ROWS // S5_RC, carry_in, 0)

    aux = aux_ref[...]
    npair = S5_LG // 2
    shape = (BATCH, S5_LANES)
    coef = []
    for d in range(2):
        for j in range(npair):
            lo = d * (S5_K // 2) + j * S5_LANES
            coef.append((jnp.broadcast_to(aux[0:1, lo:lo + S5_LANES], shape),
                         jnp.broadcast_to(aux[1:2, lo:lo + S5_LANES], shape)))

    def scan(s, carry):
        rf = pl.multiple_of(s * BATCH, BATCH)
        gb = jnp.where(s < S5_HB, S5_HB - 1 - s, S5_NB + S5_HB - 1 - s)
        rb = pl.multiple_of(gb * BATCH, BATCH)
        out = []
        for d, (w_scr, r) in enumerate(((wf_scr, rf), (wb_scr, rb))):
            for j in range(npair):
                xr, xi = carry[d * npair + j]
                ar, ai = coef[d * npair + j]
                w_re, w_im = w_scr.at[2 * j], w_scr.at[2 * j + 1]
                wr = w_re[pl.ds(r, BATCH), :]
                wi = w_im[pl.ds(r, BATCH), :]
                w_re[pl.ds(r, BATCH), :] = xr
                w_im[pl.ds(r, BATCH), :] = xi
                out.append((ar * xr - ai * xi + wr, ar * xi + ai * xr + wi))
        return tuple(out)

    z = jnp.zeros(shape, F32)
    lax.fori_loop(0, S5_NB, scan, tuple((z, z) for _ in range(2 * npair)))

    def read_out(rc, _):
        r0, ub = block_rows(rc)
        y = jnp.dot(ub, m_ref[...], preferred_element_type=F32)
        y = y + jnp.dot(get_tiles(wf_scr, r0), ff_ref[...], preferred_element_type=F32)
        y = y + jnp.dot(get_tiles(wb_scr, r0), fb_ref[...], preferred_element_type=F32)
        put_tiles(u_scr, r0, y)
        return 0

    lax.fori_loop(0, S5_ROWS // S5_RC, read_out, 0)

    for b in range(BATCH):
        for t in range(S5_T):
            y_ref[pl.ds(b * SEQ + t, S5_XB, stride=S5_T), :] = (
                u_scr.at[t][pl.ds(S5_HB * BATCH + b, S5_XB, stride=BATCH), :])
            y_ref[pl.ds(NX + b * CTX_LEN + t, S5_HB, stride=S5_T), :] = u_scr.at[t][pl.ds(b, S5_HB, stride=BATCH), :]
    y_ref[...] = y_ref[...] + a_ref[...] * aux[2:3, 0:S5_LANES]


def _s5_mix(a, layer, ef, eb, m, ff, fb, aux):
    wspec = lambda: pl.BlockSpec((None, None, S5_K, S5_K), lambda g: (layer, g, 0, 0))
    return pl.pallas_call(
        _s5_kernel,
        out_shape=jax.ShapeDtypeStruct((NT, S5_WIDTH), F32),
        grid=(S5_TILES,),
        in_specs=[
            pl.BlockSpec((NT, S5_LANES), lambda g: (0, COL_S5 // S5_LANES + g)),
            wspec(), wspec(), wspec(), wspec(), wspec(),
            pl.BlockSpec((None, None, 8, S5_K), lambda g: (layer, g, 0, 0)),
        ],
        out_specs=pl.BlockSpec((NT, S5_LANES), lambda g: (0, g)),
        scratch_shapes=[pltpu.VMEM((S5_T, S5_ROWS, S5_LANES), F32)] * 3,
        compiler_params=_cparams(("arbitrary",), VMEM_LIMIT_BIG),
        name="s5_scan",
    )(a, ef, eb, m, ff, fb, aux)


def _glu_kernel(y_ref, w_ref, b_ref, o_ref):
    g = _gelu_tanh(y_ref[...]).astype(BF16)
    z = jnp.dot(g, w_ref[...], preferred_element_type=F32) + b_ref[...]
    o_ref[...] = z[:, :S5_WIDTH] * _sigmoid(z[:, S5_WIDTH:])


def _s5_glu(y, w_glu, b_glu):
    tm = 1024
    return pl.pallas_call(
        _glu_kernel,
        out_shape=jax.ShapeDtypeStruct((NT, S5_WIDTH), F32),
        grid=(NT // tm,),
        in_specs=[
            pl.BlockSpec((tm, S5_WIDTH), lambda i: (i, 0)),
            pl.BlockSpec((S5_WIDTH, 2 * S5_WIDTH), lambda i: (0, 0)),
            pl.BlockSpec((1, 2 * S5_WIDTH), lambda i: (0, 0)),
        ],
        out_specs=pl.BlockSpec((tm, S5_WIDTH), lambda i: (i, 0)),
        compiler_params=_cparams(("arbitrary",)),
        name="s5_glu",
    )(y, w_glu.astype(BF16), b_glu.reshape(1, 2 * S5_WIDTH))


def _chunk_index(d, s):
    is_ctx = s < H_CHUNKS
    fwd = jnp.where(is_ctx, s, s - H_CHUNKS)
    bwd = jnp.where(is_ctx, H_CHUNKS - 1 - s, MIX_STEPS - 1 - s)
    return is_ctx, jnp.where(d == 0, fwd, bwd)


def _token_block(b, d, s):
    is_ctx, c = _chunk_index(d, s)
    return jnp.where(is_ctx, NX // CHUNK + b * H_CHUNKS + c, b * X_CHUNKS + c)


def _rope_block(b, d, s):
    is_ctx, c = _chunk_index(d, s)
    return jnp.where(is_ctx, X_CHUNKS + c, c)


def _dot_nt(a, b):
    return lax.dot_general(a, b, (((1,), (1,)), ((), ())), preferred_element_type=F32)


def _dot_tn(a, b):
    return lax.dot_general(a, b, (((0,), (0,)), ((), ())), preferred_element_type=F32)


def _dot(a, b):
    return jnp.dot(a, b, preferred_element_type=F32)


def _head(ref, h):
    return ref[:, h * HEAD_DIM:(h + 1) * HEAD_DIM]


def _chunk_pos(rev):
    i = lax.broadcasted_iota(jnp.int32, (CHUNK, CHUNK), 0)
    j = lax.broadcasted_iota(jnp.int32, (CHUNK, CHUNK), 1)
    pi = jnp.where(rev, CHUNK - 1 - i, i)
    pj = jnp.where(rev, CHUNK - 1 - j, j)
    return pi, pj


def _ret_kernel(dl_ref, q_ref, k_ref, v_ref, cos_ref, sin_ref, o_ref, r_scr, dec_scr, qd_scr, kd_scr, cd_scr):
    d = pl.program_id(1)
    s = pl.program_id(2)
    rev = d == 1

    @pl.when(s == 0)
    def _():
        pi, pj = _chunk_pos(rev)
        rel = (pi - pj).astype(F32)
        pif = pi.astype(F32)
        for h in range(HEADS):
            lg = _log_sigmoid(dl_ref[h])[0:1, :]
            dec_scr[h] = jnp.where(rel >= 0, jnp.exp(lg * jnp.maximum(rel, 0.0)), 0.0)
            qd_scr[h] = jnp.exp(lg * (pif + 1.0))
            kd_scr[h] = jnp.exp(lg * (CHUNK - 1.0 - pif))
            cd_scr[h] = jnp.exp(jnp.broadcast_to(lg, (8, HEAD_DIM)) * CHUNK)
        r_scr[...] = jnp.zeros_like(r_scr)

    cos = cos_ref[...]
    sin = sin_ref[...]
    scale = HEAD_DIM ** -0.5
    hs = range(HEADS)
    q = [_head(q_ref, h) for h in hs]
    k = [_head(k_ref, h) for h in hs]
    v = [_head(v_ref, h).astype(BF16) for h in hs]
    q = [q[h] * cos + pltpu.roll(q[h], HEAD_DIM // 2, 1) * sin for h in hs]
    k = [(k[h] * cos + pltpu.roll(k[h], HEAD_DIM // 2, 1) * sin) * scale for h in hs]
    r = [r_scr[h] for h in hs]
    sm = [_dot_nt(q[h].astype(BF16), k[h].astype(BF16)) * dec_scr[h] for h in hs]
    inter = [_dot((q[h] * qd_scr[h]).astype(BF16), r[h].astype(BF16)) for h in hs]
    kv = [_dot_tn((k[h] * kd_scr[h]).astype(BF16), v[h]) for h in hs]
    for h in hs:
        o_ref[:, h * HEAD_DIM:(h + 1) * HEAD_DIM] = _dot(sm[h].astype(BF16), v[h]) + inter[h]
        r_scr[h] = cd_scr[h][0:1, :] * r[h] + kv[h]


def _retention(a, rope_cos, rope_sin, decay_logit):
    dl = jnp.broadcast_to(decay_logit.astype(F32)[:, :, None, None], (2, HEADS, 8, HEAD_DIM))
    tok = lambda col: pl.BlockSpec((CHUNK, RET_WIDTH), lambda b, d, s: (_token_block(b, d, s), col))
    rope = pl.BlockSpec((CHUNK, HEAD_DIM), lambda b, d, s: (_rope_block(b, d, s), 0))
    hh = pltpu.VMEM((HEADS, CHUNK, HEAD_DIM), F32)
    return pl.pallas_call(
        _ret_kernel,
        out_shape=jax.ShapeDtypeStruct((2, NT, RET_WIDTH), F32),
        grid=(BATCH, 2, MIX_STEPS),
        in_specs=[
            pl.BlockSpec((None, HEADS, 8, HEAD_DIM), lambda b, d, s: (d, 0, 0, 0)),
            tok(COL_RET // RET_WIDTH), tok(COL_RET // RET_WIDTH + 1), tok(COL_RET // RET_WIDTH + 2),
            rope, rope,
        ],
        out_specs=pl.BlockSpec((None, CHUNK, RET_WIDTH), lambda b, d, s: (d, _token_block(b, d, s), 0)),
        scratch_shapes=[hh, hh, hh, hh, pltpu.VMEM((HEADS, 8, HEAD_DIM), F32)],
        compiler_params=_cparams(("arbitrary", "arbitrary", "arbitrary")),
        name="retention",
    )(dl, a, a, a, rope_cos, rope_sin)


def _split3(x):
    h = x.astype(BF16)
    r = x - h.astype(F32)
    m = r.astype(BF16)
    return h, m, (r - m.astype(F32)).astype(BF16)


def _mlstm_kernel(q_ref, k_ref, v_ref, g_ref, gb_ref, o_ref, c_scr, n_scr, m_scr):
    d = pl.program_id(1)
    s = pl.program_id(2)
    rev = d == 1

    @pl.when(s == 0)
    def _():
        c_scr[...] = jnp.zeros_like(c_scr)
        n_scr[...] = jnp.zeros_like(n_scr)
        m_scr[...] = jnp.zeros_like(m_scr)

    pi, pj = _chunk_pos(rev)
    causal = pj <= pi
    tri = jnp.where(causal, 1.0, 0.0).astype(BF16)
    pre = g_ref[...] + gb_ref[...]
    pre_t = pre.T[0:32, :]
    lf_c = _log_sigmoid(pre)
    lf_r = _log_sigmoid(pre_t)
    b_cols = sum(_dot(tri, part) for part in _split3(lf_c))
    b_rows = sum(_dot_nt(part, tri) for part in _split3(lf_r))
    scale = HEAD_DIM ** -0.5
    neg_inf = -jnp.inf
    hs = range(HEADS)
    pick = lambda t, kind, h, rows: jnp.where(rev, rows(t, (2 + kind) * HEADS + h), rows(t, kind * HEADS + h))
    as_col = lambda t, c: t[:, c:c + 1]
    as_row = lambda t, c: t[c:c + 1, :]
    i_col = [pick(pre, 0, h, as_col) for h in hs]
    b_col = [pick(b_cols, 1, h, as_col) for h in hs]
    i_row = [pick(pre_t, 0, h, as_row) for h in hs]
    b_row = [pick(b_rows, 1, h, as_row) for h in hs]
    b_end = [jnp.where(rev, b_cols[0:1, 3 * HEADS + h:3 * HEADS + h + 1],
                       b_cols[CHUNK - 1:CHUNK, HEADS + h:HEADS + h + 1]) for h in hs]
    m_prev = [m_scr[h][0:1, 0:1] for h in hs]
    q = [_head(q_ref, h) for h in hs]
    k = [_head(k_ref, h) * scale for h in hs]
    v = [_head(v_ref, h).astype(BF16) for h in hs]
    qb = [q[h].astype(BF16) for h in hs]
    c_mem = [c_scr[h] for h in hs]
    n_mem = [n_scr[h][0:1, :] for h in hs]
    log_w = [jnp.where(causal, b_col[h] - b_row[h] + i_row[h], neg_inf) for h in hs]
    log_a = [b_col[h] + m_prev[h] for h in hs]
    m_t = [jnp.maximum(log_a[h], jnp.max(log_w[h], axis=-1, keepdims=True)) for h in hs]
    w = [jnp.exp(log_w[h] - m_t[h]) for h in hs]
    a = [jnp.exp(log_a[h] - m_t[h]) for h in hs]
    sm = [_dot_nt(qb[h], k[h].astype(BF16)) * w[h] for h in hs]
    inter = [_dot(qb[h], c_mem[h].astype(BF16)) for h in hs]
    num = [_dot(sm[h].astype(BF16), v[h]) + a[h] * inter[h] for h in hs]
    den = [jnp.sum(sm[h] + a[h] * (q[h] * n_mem[h]), axis=-1, keepdims=True) for h in hs]
    for h in hs:
        o_ref[:, h * HEAD_DIM:(h + 1) * HEAD_DIM] = num[h] / jnp.maximum(jnp.abs(den[h]), jnp.exp(-m_t[h]))
    log_w_end = [b_end[h] - b_col[h] + i_col[h] for h in hs]
    m_new = [jnp.maximum(b_end[h] + m_prev[h], jnp.max(log_w_end[h], axis=0, keepdims=True)) for h in hs]
    a_end = [jnp.exp(b_end[h] + m_prev[h] - m_new[h]) for h in hs]
    kw = [k[h] * jnp.exp(log_w_end[h] - m_new[h]) for h in hs]
    kv = [_dot_tn(kw[h].astype(BF16), v[h]) for h in hs]
    for h in hs:
        c_scr[h] = a_end[h] * c_mem[h] + kv[h]
        n_scr[h] = jnp.broadcast_to(a_end[h] * n_mem[h] + jnp.sum(kw[h], axis=0, keepdims=True), (8, HEAD_DIM))
        m_scr[h] = jnp.broadcast_to(m_new[h], (8, HEAD_DIM))


def _mlstm(a, igate_b, fgate_b):
    gb = jnp.stack([igate_b.astype(F32), fgate_b.astype(F32)], axis=1).reshape(1, 4 * HEADS)
    gb = jnp.pad(gb, ((0, 0), (0, HEAD_DIM - 4 * HEADS)))
    col0 = COL_ML // MLSTM_WIDTH
    tok = lambda col: pl.BlockSpec((CHUNK, MLSTM_WIDTH), lambda b, d, s: (_token_block(b, d, s), col))
    small = pltpu.VMEM((HEADS, 8, HEAD_DIM), F32)
    return pl.pallas_call(
        _mlstm_kernel,
        out_shape=jax.ShapeDtypeStruct((2, NT, MLSTM_WIDTH), F32),
        grid=(BATCH, 2, MIX_STEPS),
        in_specs=[
            tok(col0), tok(col0 + 1), tok(col0 + 2),
            pl.BlockSpec((CHUNK, HEAD_DIM), lambda b, d, s: (_token_block(b, d, s), COL_GATE // HEAD_DIM)),
            pl.BlockSpec((1, HEAD_DIM), lambda b, d, s: (0, 0)),
        ],
        out_specs=pl.BlockSpec((None, CHUNK, MLSTM_WIDTH), lambda b, d, s: (d, _token_block(b, d, s), 0)),
        scratch_shapes=[pltpu.VMEM((HEADS, CHUNK, HEAD_DIM), F32), small, small],
        compiler_params=_cparams(("arbitrary", "arbitrary", "arbitrary")),
        name="mlstm",
    )(a, a, a, a, gb)


def _head_norm(o, center):
    if center:
        o = o - jnp.mean(o, axis=-1, keepdims=True)
    return o * lax.rsqrt(jnp.mean(o * o, axis=-1, keepdims=True) + EPS)


def _out_proj_kernel(s5_ref, ro_ref, rg_ref, rw_ref, mo_ref, mg_ref, mw_ref, x_ref, gate_ref, w_ref, o_ref):
    pair = 2 * HEAD_DIM
    acc = jnp.dot(s5_ref[:, 0:pair].astype(BF16), w_ref[0:pair, :], preferred_element_type=F32)
    acc += jnp.dot(s5_ref[:, pair:2 * pair].astype(BF16), w_ref[pair:2 * pair, :], preferred_element_type=F32)
    for hp in range(HEADS // 2):
        rs, ms = [], []
        for h in (2 * hp, 2 * hp + 1):
            sl = slice(h * HEAD_DIM, (h + 1) * HEAD_DIM)
            r = _head_norm(ro_ref[0, :, sl] + ro_ref[1, :, sl], True) * rw_ref[:, sl] * _silu(rg_ref[:, sl])
            m = _head_norm(mo_ref[0, :, sl] + mo_ref[1, :, sl], False) * mw_ref[:, sl] * _sigmoid(mg_ref[:, sl])
            rs.append(r.astype(BF16))
            ms.append(m.astype(BF16))
        r0 = S5_WIDTH + hp * pair
        m0 = S5_WIDTH + RET_WIDTH + hp * pair
        acc += jnp.dot(jnp.concatenate(rs, axis=1), w_ref[r0:r0 + pair, :], preferred_element_type=F32)
        acc += jnp.dot(jnp.concatenate(ms, axis=1), w_ref[m0:m0 + pair, :], preferred_element_type=F32)
    o_ref[...] = x_ref[...] + gate_ref[...] * acc


def _out_proj(rows, s5y, ret_o, ml_o, a, ret_norm_w, mlstm_norm_w, xs, mod3, w_out):
    tm = 256
    return pl.pallas_call(
        _out_proj_kernel,
        out_shape=jax.ShapeDtypeStruct((rows, D_MODEL), F32),
        grid=(rows // tm,),
        in_specs=[
            pl.BlockSpec((tm, S5_WIDTH), lambda i: (i, 0)),
            pl.BlockSpec((2, tm, RET_WIDTH), lambda i: (0, i, 0)),
            pl.BlockSpec((tm, RET_WIDTH), lambda i: (i, COL_RET // RET_WIDTH + 3)),
            pl.BlockSpec((1, RET_WIDTH), lambda i: (0, 0)),
            pl.BlockSpec((2, tm, MLSTM_WIDTH), lambda i: (0, i, 0)),
            pl.BlockSpec((tm, MLSTM_WIDTH), lambda i: (i, COL_ML // MLSTM_WIDTH + 3)),
            pl.BlockSpec((1, MLSTM_WIDTH), lambda i: (0, 0)),
            pl.BlockSpec((tm, D_MODEL), lambda i: (i, 0)),
            pl.BlockSpec((None, 1, D_MODEL), lambda i: (_row_group(i, tm), 0, 2)),
            pl.BlockSpec((D_MODEL, D_MODEL), lambda i: (0, 0), pipeline_mode=pl.Buffered(1)),
        ],
        out_specs=pl.BlockSpec((tm, D_MODEL), lambda i: (i, 0)),
        compiler_params=_cparams(("arbitrary",)),
        name="out_proj",
    )(s5y, ret_o, a, ret_norm_w.reshape(1, RET_WIDTH), ml_o, a, mlstm_norm_w.reshape(1, MLSTM_WIDTH),
      xs, mod3, w_out)


def _mlp_kernel(x_ref, nw_ref, shift_ref, scale_ref, gate_ref, w1_ref, w2_ref, nf_ref, o_ref, lhs_scr,
                *, final_norm):
    f = pl.program_id(1)

    @pl.when(f == 0)
    def _():
        lhs_scr[...] = _rms_modulate(x_ref[...], nw_ref[...], shift_ref[...], scale_ref[...]).astype(BF16)
        o_ref[...] = jnp.zeros_like(o_ref)

    hid = jnp.dot(lhs_scr[...], w1_ref[...].astype(BF16), preferred_element_type=F32)
    hid = jnp.square(jnp.maximum(hid, 0.0)).astype(BF16)
    o_ref[...] += jnp.dot(hid, w2_ref[...].astype(BF16), preferred_element_type=F32)

    @pl.when(f == pl.num_programs(1) - 1)
    def _():
        y = x_ref[...] + gate_ref[...] * o_ref[...]
        if final_norm:
            y = y * lax.rsqrt(jnp.mean(y * y, axis=-1, keepdims=True) + EPS) * nf_ref[...]
        o_ref[...] = y


def _mlp(rows, xs, norm_w, mod3, w1, w2, layer, norm_f_w, final_norm):
    tm, tf = 1024, 512
    modspec = lambda k: pl.BlockSpec((None, 1, D_MODEL), lambda i, f: (_row_group(i, tm), 0, k))
    once = pl.Buffered(1)
    return pl.pallas_call(
        functools.partial(_mlp_kernel, final_norm=final_norm),
        out_shape=jax.ShapeDtypeStruct((rows, D_MODEL), F32),
        grid=(rows // tm, D_FF // tf),
        in_specs=[
            pl.BlockSpec((tm, D_MODEL), lambda i, f: (i, 0)),
            pl.BlockSpec((1, D_MODEL), lambda i, f: (0, 0)),
            modspec(3), modspec(4), modspec(5),
            pl.BlockSpec((None, D_MODEL, tf), lambda i, f: (layer, 0, f)),
            pl.BlockSpec((None, tf, D_MODEL), lambda i, f: (layer, f, 0)),
            pl.BlockSpec((1, D_MODEL), lambda i, f: (0, 0)),
        ],
        out_specs=pl.BlockSpec((tm, D_MODEL), lambda i, f: (i, 0), pipeline_mode=once),
        scratch_shapes=[pltpu.VMEM((tm, D_MODEL), BF16)],
        compiler_params=_cparams(("arbitrary", "arbitrary"), VMEM_LIMIT_BIG),
        name="mlp",
    )(xs, norm_w.reshape(1, D_MODEL), mod3, mod3, mod3, w1, w2, norm_f_w.reshape(1, D_MODEL))


def _rope_tables():
    quarter = HEAD_DIM // 4
    rows = jnp.repeat(jnp.arange(SEQ // GRID_W, dtype=F32), GRID_W)
    cols = jnp.tile(jnp.arange(GRID_W, dtype=F32), SEQ // GRID_W)
    inv = ROPE_BASE ** (-jnp.arange(quarter, dtype=F32) / quarter)
    ang = jnp.concatenate([rows[:, None] * inv, cols[:, None] * inv], axis=-1)
    cos, sin = jnp.cos(ang), jnp.sin(ang)
    cos2 = jnp.concatenate([cos, cos], axis=-1)
    sin2 = jnp.concatenate([-sin, sin], axis=-1)
    cos2 = jnp.concatenate([cos2, jnp.ones((CTX_LEN, HEAD_DIM), F32)], axis=0)
    sin2 = jnp.concatenate([sin2, jnp.zeros((CTX_LEN, HEAD_DIM), F32)], axis=0)
    return cos2, sin2


def _permute_w_in(w):
    s5 = w[:, :S5_WIDTH]
    ret = w[:, S5_WIDTH:S5_WIDTH + 4 * RET_WIDTH]
    ml = w[:, S5_WIDTH + 4 * RET_WIDTH:S5_WIDTH + 4 * RET_WIDTH + 4 * MLSTM_WIDTH]
    gates = w[:, S5_WIDTH + 4 * RET_WIDTH + 4 * MLSTM_WIDTH:]
    pad = jnp.zeros((D_MODEL, IN_PAD - IN_WIDTH), w.dtype)
    return jnp.concatenate([ret, ml, s5, gates, pad], axis=1).astype(BF16)


def kernel(x, c, ctx, c_ctx, w_mod, b_mod, norm1_w, norm2_w, w_in, w_out, s5_lam_re, s5_lam_im, s5_log_step, s5_b_re, s5_b_im, s5_c_re, s5_c_im, s5_d, s5_w_glu, s5_b_glu, ret_decay_logit, ret_norm_w, mlstm_igate_b, mlstm_fgate_b, mlstm_norm_w, w_ff1, w_ff2, norm_f_w):
    rope_cos, rope_sin = _rope_tables()
    cvec = jnp.concatenate([c, c_ctx[None, :], jnp.zeros((MOD_ROWS - BATCH - 1, D_MODEL), F32)], axis=0)
    mod = _modulation(cvec, w_mod, b_mod)
    xs = jnp.concatenate([x.reshape(NX, D_MODEL), ctx.reshape(NH, D_MODEL)], axis=0)
    s5w = jax.vmap(_s5_prep)(s5_lam_re, s5_lam_im, s5_log_step, s5_b_re, s5_b_im, s5_c_re, s5_c_im, s5_d)
    for l in range(DEPTH):
        last = l == DEPTH - 1
        mod3 = mod[l].reshape(MOD_ROWS, 1, 6 * D_MODEL)
        a = _in_proj(xs, norm1_w[l], mod3, _permute_w_in(w_in[l]))
        s5y = _s5_glu(_s5_mix(a, l, *s5w), s5_w_glu[l], s5_b_glu[l])
        ret_o = _retention(a, rope_cos, rope_sin, ret_decay_logit[l])
        ml_o = _mlstm(a, mlstm_igate_b[l], mlstm_fgate_b[l])
        rows = NX if last else NT
        xs = _out_proj(rows, s5y, ret_o, ml_o, a, ret_norm_w[l], mlstm_norm_w[l], xs, mod3,
                       w_out[l].astype(BF16))
        xs = _mlp(rows, xs, norm2_w[l], mod3, w_ff1, w_ff2, l, norm_f_w, last)
    return xs.reshape(BATCH, SEQ, D_MODEL)
```

```python
import functools
import math

import jax
import jax.numpy as jnp
from jax import lax
from jax.experimental import pallas as pl
from jax.experimental.pallas import tpu as pltpu

F32 = jnp.float32
BF16 = jnp.bfloat16

D_MODEL = 2048
BATCH = 4
SEQ = 2048
DEPTH = 2
CTX_LEN = 256
GRID_W = 64
HEAD_DIM = 128
S5_WIDTH = 512
S5_GROUP = 16
S5_GROUPS = 32
S5_STATE = 64
RET_WIDTH = 768
MLSTM_WIDTH = 768
HEADS = 6
IN_WIDTH = S5_WIDTH + 4 * RET_WIDTH + 4 * MLSTM_WIDTH + 4 * HEADS
D_FF = 4 * D_MODEL
CHUNK = 128
ROPE_BASE = 10000.0
EPS = 1e-6

NX = BATCH * SEQ
NH = BATCH * CTX_LEN
NT = NX + NH
MOD_ROWS = 8
CTX_GROUP = BATCH

COL_RET = 0
COL_ML = 4 * RET_WIDTH
COL_S5 = COL_ML + 4 * MLSTM_WIDTH
COL_GATE = COL_S5 + S5_WIDTH
IN_PAD = 7168

X_CHUNKS = SEQ // CHUNK
H_CHUNKS = CTX_LEN // CHUNK
MIX_STEPS = X_CHUNKS + H_CHUNKS

S5_T = 8
S5_LANES = 128
S5_LG = S5_LANES // S5_GROUP
S5_TILES = S5_WIDTH // S5_LANES
S5_K = S5_T * S5_LANES
S5_XB = SEQ // S5_T
S5_HB = CTX_LEN // S5_T
S5_NB = S5_XB + S5_HB
S5_ROWS = S5_NB * BATCH
S5_RC = 128

VMEM_LIMIT = 48 * 1024 * 1024
VMEM_LIMIT_BIG = 56 * 1024 * 1024


def _cparams(sem, limit=VMEM_LIMIT):
    return pltpu.CompilerParams(dimension_semantics=sem, vmem_limit_bytes=limit)


def _sigmoid(x):
    return 1.0 / (1.0 + jnp.exp(-x))


def _silu(x):
    return x * _sigmoid(x)


def _log_sigmoid(x):
    return jnp.minimum(x, 0.0) - jnp.log1p(jnp.exp(-jnp.abs(x)))


def _gelu_tanh(x):
    c = math.sqrt(2.0 / math.pi)
    return 0.5 * x * (1.0 + jnp.tanh(c * (x + 0.044715 * (x * x * x))))


def _row_group(i, tm):
    return jnp.minimum((i * tm) // SEQ, CTX_GROUP)


def _mod_kernel(c_ref, w_ref, b_ref, o_ref):
    c = c_ref[...]
    lhs = _silu(c).astype(BF16)
    o_ref[...] = jnp.dot(lhs, w_ref[...].astype(BF16), preferred_element_type=F32) + b_ref[...]


def _modulation(cvec, w_mod, b_mod):
    tn = 512
    n = 6 * D_MODEL
    return pl.pallas_call(
        _mod_kernel,
        out_shape=jax.ShapeDtypeStruct((DEPTH, MOD_ROWS, n), F32),
        grid=(DEPTH, n // tn),
        in_specs=[
            pl.BlockSpec((MOD_ROWS, D_MODEL), lambda l, j: (0, 0)),
            pl.BlockSpec((None, D_MODEL, tn), lambda l, j: (l, 0, j)),
            pl.BlockSpec((None, 1, tn), lambda l, j: (l, 0, j)),
        ],
        out_specs=pl.BlockSpec((None, MOD_ROWS, tn), lambda l, j: (l, 0, j)),
        compiler_params=_cparams(("arbitrary", "arbitrary")),
        name="modulation",
    )(cvec, w_mod, b_mod.reshape(DEPTH, 1, n))


def _rms_modulate(x, nw, shift, scale):
    y = x * lax.rsqrt(jnp.mean(x * x, axis=-1, keepdims=True) + EPS) * nw
    return y * (1.0 + scale) + shift


def _in_proj_kernel(x_ref, nw_ref, shift_ref, scale_ref, w_ref, o_ref, lhs_scr):
    @pl.when(pl.program_id(1) == 0)
    def _():
        lhs_scr[...] = _rms_modulate(x_ref[...], nw_ref[...], shift_ref[...], scale_ref[...]).astype(BF16)

    o_ref[...] = jnp.dot(lhs_scr[...], w_ref[...], preferred_element_type=F32)


def _in_proj(xs, norm_w, mod3, w_in_p):
    tm, tn = 1024, 1024
    return pl.pallas_call(
        _in_proj_kernel,
        out_shape=jax.ShapeDtypeStruct((NT, IN_PAD), F32),
        grid=(NT // tm, IN_PAD // tn),
        in_specs=[
            pl.BlockSpec((tm, D_MODEL), lambda i, j: (i, 0)),
            pl.BlockSpec((1, D_MODEL), lambda i, j: (0, 0)),
            pl.BlockSpec((None, 1, D_MODEL), lambda i, j: (_row_group(i, tm), 0, 0)),
            pl.BlockSpec((None, 1, D_MODEL), lambda i, j: (_row_group(i, tm), 0, 1)),
            pl.BlockSpec((D_MODEL, tn), lambda i, j: (0, j)),
        ],
        out_specs=pl.BlockSpec((tm, tn), lambda i, j: (i, j)),
        scratch_shapes=[pltpu.VMEM((tm, D_MODEL), BF16)],
        compiler_params=_cparams(("arbitrary", "arbitrary")),
        name="in_proj",
    )(xs, norm_w.reshape(1, D_MODEL), mod3, mod3, w_in_p)


def _s5_prep(lam_re, lam_im, log_step, b_re, b_im, c_re, c_im, d_skip):
    hp = lax.Precision.HIGHEST
    lam_re = jnp.minimum(lam_re.astype(F32), -1e-4)
    lam_im = lam_im.astype(F32)
    step = jnp.exp(log_step.astype(F32))[..., None]
    mag = jnp.exp(lam_re * step)
    ab_re, ab_im = mag * jnp.cos(lam_im * step), mag * jnp.sin(lam_im * step)
    den = lam_re * lam_re + lam_im * lam_im
    ir, ii = lam_re / den, -lam_im / den
    nr = (ab_re - 1.0) * ir - ab_im * ii
    ni = (ab_re - 1.0) * ii + ab_im * ir
    bre, bim = b_re.astype(F32), b_im.astype(F32)
    bb_re = nr[..., None] * bre - ni[..., None] * bim
    bb_im = nr[..., None] * bim + ni[..., None] * bre
    lpow = jnp.arange(S5_T + 1, dtype=F32)[:, None, None, None]
    pmag = jnp.exp(lpow * (lam_re * step))
    pr, pi = pmag * jnp.cos(lpow * (lam_im * step)), pmag * jnp.sin(lpow * (lam_im * step))
    cre, cim = c_re.astype(F32), c_im.astype(F32)
    ca_r = cre[None] * pr[:, :, :, None, :] - cim[None] * pi[:, :, :, None, :]
    ca_i = cre[None] * pi[:, :, :, None, :] + cim[None] * pr[:, :, :, None, :]
    kl = (jnp.einsum('ldgnp,dgpm->ldgnm', ca_r, bb_re, precision=hp)
          - jnp.einsum('ldgnp,dgpm->ldgnm', ca_i, bb_im, precision=hp))
    tt = jnp.arange(S5_T)
    lag = tt[None, :] - tt[:, None]
    kf = jnp.where((lag >= 0)[:, :, None, None, None], kl[jnp.clip(lag, 0, S5_T), 0], 0.0)
    kb = jnp.where((lag <= 0)[:, :, None, None, None], kl[jnp.clip(-lag, 0, S5_T), 1], 0.0)
    nt = S5_TILES
    row = jnp.arange(S5_K)
    col = jnp.arange(S5_K)
    lane = jnp.arange(S5_LANES)
    g_blk = (row // S5_GROUP) % S5_LG
    g_state = 2 * (row // (4 * S5_STATE)) + (row // S5_STATE) % 2
    to_blk = ((lane[:, None] // S5_GROUP == col[None, :] // S5_LANES)
              & (lane[:, None] % S5_GROUP == col[None, :] % S5_GROUP)).astype(BF16)
    to_state = ((lane[:, None] // S5_STATE == (col[None, :] // S5_LANES) % 2)
                & (lane[:, None] % S5_STATE == col[None, :] % S5_STATE)).astype(BF16)

    def expand(compact, spread, g_row, g_col):
        wide = jnp.einsum('trk,kc->trc', compact.reshape(nt, S5_K, S5_LANES).astype(BF16), spread,
                          preferred_element_type=BF16)
        return jnp.where(g_row[:, None] == g_col[None, :], wide, jnp.zeros((), BF16))

    mc = (kf + kb).reshape(S5_T, S5_T, nt, S5_LG, S5_GROUP, S5_GROUP).transpose(2, 0, 3, 5, 1, 4)
    m = expand(mc, to_blk, g_blk, g_blk)

    def carry_in(d, powers):
        er = pr[powers, d][..., None] * bb_re[d][None] - pi[powers, d][..., None] * bb_im[d][None]
        ei = pr[powers, d][..., None] * bb_im[d][None] + pi[powers, d][..., None] * bb_re[d][None]
        arrange = lambda t: t.reshape(S5_T, nt, S5_LG, S5_STATE, S5_GROUP).transpose(1, 0, 2, 4, 3)
        es = jnp.stack([arrange(er), arrange(ei)], axis=4)
        return expand(es, to_state, g_blk, g_state)

    def read_out(d, powers):
        arrange = lambda t: t.reshape(S5_T, nt, S5_LG // 2, 2, S5_GROUP, S5_STATE).transpose(1, 2, 3, 5, 0, 4)
        fs = jnp.stack([arrange(ca_r[powers, d]), arrange(-ca_i[powers, d])], axis=2)
        return expand(fs, to_blk, g_state, g_blk)

    ef, eb = carry_in(0, S5_T - 1 - tt), carry_in(1, tt)
    ff, fb = read_out(0, tt + 1), read_out(1, S5_T - tt)
    lanes = lambda t: t.reshape(nt, S5_K // 2)
    a_r = jnp.concatenate([lanes(pr[S5_T, 0]), lanes(pr[S5_T, 1])], axis=-1)
    a_i = jnp.concatenate([lanes(pi[S5_T, 0]), lanes(pi[S5_T, 1])], axis=-1)
    dsk = jnp.pad(d_skip.astype(F32).reshape(nt, S5_LANES), ((0, 0), (0, S5_K - S5_LANES)))
    aux = jnp.stack([a_r, a_i, dsk] + [jnp.zeros_like(a_r)] * 5, axis=1)
    return ef, eb, m, ff, fb, aux


def _s5_kernel(a_ref, ef_ref, eb_ref, m_ref, ff_ref, fb_ref, aux_ref, y_ref, u_scr, wf_scr, wb_scr):
    for b in range(BATCH):
        for t in range(S5_T):
            u_scr.at[t][pl.ds(S5_HB * BATCH + b, S5_XB, stride=BATCH), :] = (
                a_ref[pl.ds(b * SEQ + t, S5_XB, stride=S5_T), :])
            u_scr.at[t][pl.ds(b, S5_HB, stride=BATCH), :] = a_ref[pl.ds(NX + b * CTX_LEN + t, S5_HB, stride=S5_T), :]

    def get_tiles(scr, r0):
        return jnp.concatenate([scr.at[t][pl.ds(r0, S5_RC), :] for t in range(S5_T)], axis=1).astype(BF16)

    def put_tiles(scr, r0, val):
        for t in range(S5_T):
            scr.at[t][pl.ds(r0, S5_RC), :] = val[:, t * S5_LANES:(t + 1) * S5_LANES]

    def block_rows(rc):
        r0 = pl.multiple_of(rc * S5_RC, S5_RC)
        return r0, get_tiles(u_scr, r0)

    def carry_in(rc, _):
        r0, ub = block_rows(rc)
        put_tiles(wf_scr, r0, jnp.dot(ub, ef_ref[...], preferred_element_type=F32))
        put_tiles(wb_scr, r0, jnp.dot(ub, eb_ref[...], preferred_element_type=F32))
        return 0

    lax.fori_loop(0, S5_ROWS // S5_RC, carry_in, 0)

    aux = aux_ref[...]
    npair = S5_LG // 2
    shape = (BATCH, S5_LANES)
    coef = []
    for d in range(2):
        for j in range(npair):
            lo = d * (S5_K // 2) + j * S5_LANES
            coef.append((jnp.broadcast_to(aux[0:1, lo:lo + S5_LANES], shape),
                         jnp.broadcast_to(aux[1:2, lo:lo + S5_LANES], shape)))

    def scan(s, carry):
        rf = pl.multiple_of(s * BATCH, BATCH)
        gb = jnp.where(s < S5_HB, S5_HB - 1 - s, S5_NB + S5_HB - 1 - s)
        rb = pl.multiple_of(gb * BATCH, BATCH)
        out = []
        for d, (w_scr, r) in enumerate(((wf_scr, rf), (wb_scr, rb))):
            for j in range(npair):
                xr, xi = carry[d * npair + j]
                ar, ai = coef[d * npair + j]
                w_re, w_im = w_scr.at[2 * j], w_scr.at[2 * j + 1]
                wr = w_re[pl.ds(r, BATCH), :]
                wi = w_im[pl.ds(r, BATCH), :]
                w_re[pl.ds(r, BATCH), :] = xr
                w_im[pl.ds(r, BATCH), :] = xi
                out.append((ar * xr - ai * xi + wr, ar * xi + ai * xr + wi))
        return tuple(out)

    z = jnp.zeros(shape, F32)
    lax.fori_loop(0, S5_NB, scan, tuple((z, z) for _ in range(2 * npair)))

    def read_out(rc, _):
        r0, ub = block_rows(rc)
        y = jnp.dot(ub, m_ref[...], preferred_element_type=F32)
        y = y + jnp.dot(get_tiles(wf_scr, r0), ff_ref[...], preferred_element_type=F32)
        y = y + jnp.dot(get_tiles(wb_scr, r0), fb_ref[...], preferred_element_type=F32)
        put_tiles(u_scr, r0, y)
        return 0

    lax.fori_loop(0, S5_ROWS // S5_RC, read_out, 0)

    for b in range(BATCH):
        for t in range(S5_T):
            y_ref[pl.ds(b * SEQ + t, S5_XB, stride=S5_T), :] = (
                u_scr.at[t][pl.ds(S5_HB * BATCH + b, S5_XB, stride=BATCH), :])
            y_ref[pl.ds(NX + b * CTX_LEN + t, S5_HB, stride=S5_T), :] = u_scr.at[t][pl.ds(b, S5_HB, stride=BATCH), :]
    y_ref[...] = y_ref[...] + a_ref[...] * aux[2:3, 0:S5_LANES]


def _s5_mix(a, layer, ef, eb, m, ff, fb, aux):
    wspec = lambda: pl.BlockSpec((None, None, S5_K, S5_K), lambda g: (layer, g, 0, 0))
    return pl.pallas_call(
        _s5_kernel,
        out_shape=jax.ShapeDtypeStruct((NT, S5_WIDTH), F32),
        grid=(S5_TILES,),
        in_specs=[
            pl.BlockSpec((NT, S5_LANES), lambda g: (0, COL_S5 // S5_LANES + g)),
            wspec(), wspec(), wspec(), wspec(), wspec(),
            pl.BlockSpec((None, None, 8, S5_K), lambda g: (layer, g, 0, 0)),
        ],
        out_specs=pl.BlockSpec((NT, S5_LANES), lambda g: (0, g)),
        scratch_shapes=[pltpu.VMEM((S5_T, S5_ROWS, S5_LANES), F32)] * 3,
        compiler_params=_cparams(("arbitrary",), VMEM_LIMIT_BIG),
        name="s5_scan",
    )(a, ef, eb, m, ff, fb, aux)


def _glu_kernel(y_ref, w_ref, b_ref, o_ref):
    g = _gelu_tanh(y_ref[...]).astype(BF16)
    z = jnp.dot(g, w_ref[...], preferred_element_type=F32) + b_ref[...]
    o_ref[...] = z[:, :S5_WIDTH] * _sigmoid(z[:, S5_WIDTH:])


def _s5_glu(y, w_glu, b_glu):
    tm = 1024
    return pl.pallas_call(
        _glu_kernel,
        out_shape=jax.ShapeDtypeStruct((NT, S5_WIDTH), F32),
        grid=(NT // tm,),
        in_specs=[
            pl.BlockSpec((tm, S5_WIDTH), lambda i: (i, 0)),
            pl.BlockSpec((S5_WIDTH, 2 * S5_WIDTH), lambda i: (0, 0)),
            pl.BlockSpec((1, 2 * S5_WIDTH), lambda i: (0, 0)),
        ],
        out_specs=pl.BlockSpec((tm, S5_WIDTH), lambda i: (i, 0)),
        compiler_params=_cparams(("arbitrary",)),
        name="s5_glu",
    )(y, w_glu.astype(BF16), b_glu.reshape(1, 2 * S5_WIDTH))


def _chunk_index(d, s):
    is_ctx = s < H_CHUNKS
    fwd = jnp.where(is_ctx, s, s - H_CHUNKS)
    bwd = jnp.where(is_ctx, H_CHUNKS - 1 - s, MIX_STEPS - 1 - s)
    return is_ctx, jnp.where(d == 0, fwd, bwd)


def _token_block(b, d, s):
    is_ctx, c = _chunk_index(d, s)
    return jnp.where(is_ctx, NX // CHUNK + b * H_CHUNKS + c, b * X_CHUNKS + c)


def _rope_block(b, d, s):
    is_ctx, c = _chunk_index(d, s)
    return jnp.where(is_ctx, X_CHUNKS + c, c)


def _dot_nt(a, b):
    return lax.dot_general(a, b, (((1,), (1,)), ((), ())), preferred_element_type=F32)


def _dot_tn(a, b):
    return lax.dot_general(a, b, (((0,), (0,)), ((), ())), preferred_element_type=F32)


def _dot(a, b):
    return jnp.dot(a, b, preferred_element_type=F32)


def _head(ref, h):
    return ref[:, h * HEAD_DIM:(h + 1) * HEAD_DIM]


def _chunk_pos(rev):
    i = lax.broadcasted_iota(jnp.int32, (CHUNK, CHUNK), 0)
    j = lax.broadcasted_iota(jnp.int32, (CHUNK, CHUNK), 1)
    pi = jnp.where(rev, CHUNK - 1 - i, i)
    pj = jnp.where(rev, CHUNK - 1 - j, j)
    return pi, pj


def _ret_kernel(dl_ref, qf_ref, kf_ref, vf_ref, cosf_ref, sinf_ref, qb_ref, kb_ref, vb_ref, cosb_ref, sinb_ref,
                of_ref, ob_ref, r_scr, dec_scr, qd_scr, kd_scr, cd_scr):
    s = pl.program_id(1)

    @pl.when(s == 0)
    def _():
        for d in range(2):
            pi, pj = _chunk_pos(d == 1)
            rel = (pi - pj).astype(F32)
            pif = pi.astype(F32)
            for h in range(HEADS):
                x = d * HEADS + h
                lg = _log_sigmoid(dl_ref[d, h])[0:1, :]
                dec_scr[x] = jnp.where(rel >= 0, jnp.exp(lg * jnp.maximum(rel, 0.0)), 0.0)
                qd_scr[x] = jnp.exp(lg * (pif + 1.0))
                kd_scr[x] = jnp.exp(lg * (CHUNK - 1.0 - pif))
                cd_scr[x] = jnp.exp(jnp.broadcast_to(lg, (8, HEAD_DIM)) * CHUNK)
        r_scr[...] = jnp.zeros_like(r_scr)

    scale = HEAD_DIM ** -0.5
    dirs = ((qf_ref, kf_ref, vf_ref, cosf_ref, sinf_ref, of_ref), (qb_ref, kb_ref, vb_ref, cosb_ref, sinb_ref, ob_ref))
    ch = [(d, h) for d in range(2) for h in range(HEADS)]
    xs = range(len(ch))
    cos = [dirs[d][3][...] for d in range(2)]
    sin = [dirs[d][4][...] for d in range(2)]
    q = [_head(dirs[d][0], h) for d, h in ch]
    k = [_head(dirs[d][1], h) for d, h in ch]
    v = [_head(dirs[d][2], h).astype(BF16) for d, h in ch]
    q = [q[x] * cos[ch[x][0]] + pltpu.roll(q[x], HEAD_DIM // 2, 1) * sin[ch[x][0]] for x in xs]
    k = [(k[x] * cos[ch[x][0]] + pltpu.roll(k[x], HEAD_DIM // 2, 1) * sin[ch[x][0]]) * scale for x in xs]
    r = [r_scr[x] for x in xs]
    sm = [_dot_nt(q[x].astype(BF16), k[x].astype(BF16)) * dec_scr[x] for x in xs]
    inter = [_dot((q[x] * qd_scr[x]).astype(BF16), r[x].astype(BF16)) for x in xs]
    kv = [_dot_tn((k[x] * kd_scr[x]).astype(BF16), v[x]) for x in xs]
    for x in xs:
        d, h = ch[x]
        dirs[d][5][:, h * HEAD_DIM:(h + 1) * HEAD_DIM] = _dot(sm[x].astype(BF16), v[x]) + inter[x]
        r_scr[x] = cd_scr[x][0:1, :] * r[x] + kv[x]


def _retention(a, rope_cos, rope_sin, decay_logit):
    dl = jnp.broadcast_to(decay_logit.astype(F32)[:, :, None, None], (2, HEADS, 8, HEAD_DIM))
    col0 = COL_RET // RET_WIDTH
    specs = [pl.BlockSpec((2, HEADS, 8, HEAD_DIM), lambda b, s: (0, 0, 0, 0))]
    for d in range(2):
        for col in (col0, col0 + 1, col0 + 2):
            specs.append(pl.BlockSpec((CHUNK, RET_WIDTH), lambda b, s, d=d, col=col: (_token_block(b, d, s), col)))
        specs += [pl.BlockSpec((CHUNK, HEAD_DIM), lambda b, s, d=d: (_rope_block(b, d, s), 0))] * 2
    hh = pltpu.VMEM((2 * HEADS, CHUNK, HEAD_DIM), F32)
    out = jax.ShapeDtypeStruct((NT, RET_WIDTH), F32)
    return pl.pallas_call(
        _ret_kernel,
        out_shape=(out, out),
        grid=(BATCH, MIX_STEPS),
        in_specs=specs,
        out_specs=[pl.BlockSpec((CHUNK, RET_WIDTH), lambda b, s, d=d: (_token_block(b, d, s), 0)) for d in range(2)],
        scratch_shapes=[hh, hh, hh, hh, pltpu.VMEM((2 * HEADS, 8, HEAD_DIM), F32)],
        compiler_params=_cparams(("arbitrary", "arbitrary")),
        name="retention",
    )(dl, a, a, a, rope_cos, rope_sin, a, a, a, rope_cos, rope_sin)


def _split3(x):
    h = x.astype(BF16)
    r = x - h.astype(F32)
    m = r.astype(BF16)
    return h, m, (r - m.astype(F32)).astype(BF16)


def _mlstm_kernel(q_ref, k_ref, v_ref, g_ref, gb_ref, o_ref, c_scr, n_scr, m_scr):
    d = pl.program_id(1)
    s = pl.program_id(2)
    rev = d == 1

    @pl.when(s == 0)
    def _():
        c_scr[...] = jnp.zeros_like(c_scr)
        n_scr[...] = jnp.zeros_like(n_scr)
        m_scr[...] = jnp.zeros_like(m_scr)

    pi, pj = _chunk_pos(rev)
    causal = pj <= pi
    tri = jnp.where(causal, 1.0, 0.0).astype(BF16)
    pre = g_ref[...] + gb_ref[...]
    pre_t = pre.T[0:32, :]
    lf_c = _log_sigmoid(pre)
    lf_r = _log_sigmoid(pre_t)
    b_cols = sum(_dot(tri, part) for part in _split3(lf_c))
    b_rows = sum(_dot_nt(part, tri) for part in _split3(lf_r))
    scale = HEAD_DIM ** -0.5
    neg_inf = -jnp.inf
    hs = range(HEADS)
    pick = lambda t, kind, h, rows: jnp.where(rev, rows(t, (2 + kind) * HEADS + h), rows(t, kind * HEADS + h))
    as_col = lambda t, c: t[:, c:c + 1]
    as_row = lambda t, c: t[c:c + 1, :]
    i_col = [pick(pre, 0, h, as_col) for h in hs]
    b_col = [pick(b_cols, 1, h, as_col) for h in hs]
    i_row = [pick(pre_t, 0, h, as_row) for h in hs]
    b_row = [pick(b_rows, 1, h, as_row) for h in hs]
    b_end = [jnp.where(rev, b_cols[0:1, 3 * HEADS + h:3 * HEADS + h + 1],
                       b_cols[CHUNK - 1:CHUNK, HEADS + h:HEADS + h + 1]) for h in hs]
    m_prev = [m_scr[h][0:1, 0:1] for h in hs]
    q = [_head(q_ref, h) for h in hs]
    k = [_head(k_ref, h) * scale for h in hs]
    v = [_head(v_ref, h).astype(BF16) for h in hs]
    qb = [q[h].astype(BF16) for h in hs]
    c_mem = [c_scr[h] for h in hs]
    n_mem = [n_scr[h][0:1, :] for h in hs]
    log_w = [jnp.where(causal, b_col[h] - b_row[h] + i_row[h], neg_inf) for h in hs]
    log_a = [b_col[h] + m_prev[h] for h in hs]
    m_t = [jnp.maximum(log_a[h], jnp.max(log_w[h], axis=-1, keepdims=True)) for h in hs]
    w = [jnp.exp(log_w[h] - m_t[h]) for h in hs]
    a = [jnp.exp(log_a[h] - m_t[h]) for h in hs]
    sm = [_dot_nt(qb[h], k[h].astype(BF16)) * w[h] for h in hs]
    inter = [_dot(qb[h], c_mem[h].astype(BF16)) for h in hs]
    num = [_dot(sm[h].astype(BF16), v[h]) + a[h] * inter[h] for h in hs]
    den = [jnp.sum(sm[h] + a[h] * (q[h] * n_mem[h]), axis=-1, keepdims=True) for h in hs]
    for h in hs:
        o_ref[:, h * HEAD_DIM:(h + 1) * HEAD_DIM] = num[h] / jnp.maximum(jnp.abs(den[h]), jnp.exp(-m_t[h]))
    log_w_end = [b_end[h] - b_col[h] + i_col[h] for h in hs]
    m_new = [jnp.maximum(b_end[h] + m_prev[h], jnp.max(log_w_end[h], axis=0, keepdims=True)) for h in hs]
    a_end = [jnp.exp(b_end[h] + m_prev[h] - m_new[h]) for h in hs]
    kw = [k[h] * jnp.exp(log_w_end[h] - m_new[h]) for h in hs]
    kv = [_dot_tn(kw[h].astype(BF16), v[h]) for h in hs]
    for h in hs:
        c_scr[h] = a_end[h] * c_mem[h] + kv[h]
        n_scr[h] = jnp.broadcast_to(a_end[h] * n_mem[h] + jnp.sum(kw[h], axis=0, keepdims=True), (8, HEAD_DIM))
        m_scr[h] = jnp.broadcast_to(m_new[h], (8, HEAD_DIM))


def _mlstm(a, igate_b, fgate_b):
    gb = jnp.stack([igate_b.astype(F32), fgate_b.astype(F32)], axis=1).reshape(1, 4 * HEADS)
    gb = jnp.pad(gb, ((0, 0), (0, HEAD_DIM - 4 * HEADS)))
    col0 = COL_ML // MLSTM_WIDTH
    tok = lambda col: pl.BlockSpec((CHUNK, MLSTM_WIDTH), lambda b, d, s: (_token_block(b, d, s), col))
    small = pltpu.VMEM((HEADS, 8, HEAD_DIM), F32)
    return pl.pallas_call(
        _mlstm_kernel,
        out_shape=jax.ShapeDtypeStruct((2, NT, MLSTM_WIDTH), F32),
        grid=(BATCH, 2, MIX_STEPS),
        in_specs=[
            tok(col0), tok(col0 + 1), tok(col0 + 2),
            pl.BlockSpec((CHUNK, HEAD_DIM), lambda b, d, s: (_token_block(b, d, s), COL_GATE // HEAD_DIM)),
            pl.BlockSpec((1, HEAD_DIM), lambda b, d, s: (0, 0)),
        ],
        out_specs=pl.BlockSpec((None, CHUNK, MLSTM_WIDTH), lambda b, d, s: (d, _token_block(b, d, s), 0)),
        scratch_shapes=[pltpu.VMEM((HEADS, CHUNK, HEAD_DIM), F32), small, small],
        compiler_params=_cparams(("arbitrary", "arbitrary", "arbitrary")),
        name="mlstm",
    )(a, a, a, a, gb)


def _head_norm(o, center):
    if center:
        o = o - jnp.mean(o, axis=-1, keepdims=True)
    return o * lax.rsqrt(jnp.mean(o * o, axis=-1, keepdims=True) + EPS)


def _out_proj_kernel(s5_ref, rf_ref, rb_ref, rg_ref, rw_ref, mo_ref, mg_ref, mw_ref, x_ref, gate_ref, w_ref,
                     o_ref):
    pair = 2 * HEAD_DIM
    acc = jnp.dot(s5_ref[:, 0:pair].astype(BF16), w_ref[0:pair, :], preferred_element_type=F32)
    acc += jnp.dot(s5_ref[:, pair:2 * pair].astype(BF16), w_ref[pair:2 * pair, :], preferred_element_type=F32)
    for hp in range(HEADS // 2):
        rs, ms = [], []
        for h in (2 * hp, 2 * hp + 1):
            sl = slice(h * HEAD_DIM, (h + 1) * HEAD_DIM)
            r = _head_norm(rf_ref[:, sl] + rb_ref[:, sl], True) * rw_ref[:, sl] * _silu(rg_ref[:, sl])
            m = _head_norm(mo_ref[0, :, sl] + mo_ref[1, :, sl], False) * mw_ref[:, sl] * _sigmoid(mg_ref[:, sl])
            rs.append(r.astype(BF16))
            ms.append(m.astype(BF16))
        r0 = S5_WIDTH + hp * pair
        m0 = S5_WIDTH + RET_WIDTH + hp * pair
        acc += jnp.dot(jnp.concatenate(rs, axis=1), w_ref[r0:r0 + pair, :], preferred_element_type=F32)
        acc += jnp.dot(jnp.concatenate(ms, axis=1), w_ref[m0:m0 + pair, :], preferred_element_type=F32)
    o_ref[...] = x_ref[...] + gate_ref[...] * acc


def _out_proj(rows, s5y, ret_o, ml_o, a, ret_norm_w, mlstm_norm_w, xs, mod3, w_out):
    tm = 256
    return pl.pallas_call(
        _out_proj_kernel,
        out_shape=jax.ShapeDtypeStruct((rows, D_MODEL), F32),
        grid=(rows // tm,),
        in_specs=[
            pl.BlockSpec((tm, S5_WIDTH), lambda i: (i, 0)),
            pl.BlockSpec((tm, RET_WIDTH), lambda i: (i, 0)),
            pl.BlockSpec((tm, RET_WIDTH), lambda i: (i, 0)),
            pl.BlockSpec((tm, RET_WIDTH), lambda i: (i, COL_RET // RET_WIDTH + 3)),
            pl.BlockSpec((1, RET_WIDTH), lambda i: (0, 0)),
            pl.BlockSpec((2, tm, MLSTM_WIDTH), lambda i: (0, i, 0)),
            pl.BlockSpec((tm, MLSTM_WIDTH), lambda i: (i, COL_ML // MLSTM_WIDTH + 3)),
            pl.BlockSpec((1, MLSTM_WIDTH), lambda i: (0, 0)),
            pl.BlockSpec((tm, D_MODEL), lambda i: (i, 0)),
            pl.BlockSpec((None, 1, D_MODEL), lambda i: (_row_group(i, tm), 0, 2)),
            pl.BlockSpec((D_MODEL, D_MODEL), lambda i: (0, 0), pipeline_mode=pl.Buffered(1)),
        ],
        out_specs=pl.BlockSpec((tm, D_MODEL), lambda i: (i, 0)),
        compiler_params=_cparams(("arbitrary",)),
        name="out_proj",
    )(s5y, ret_o[0], ret_o[1], a, ret_norm_w.reshape(1, RET_WIDTH), ml_o, a, mlstm_norm_w.reshape(1, MLSTM_WIDTH),
      xs, mod3, w_out)


def _mlp_kernel(x_ref, nw_ref, shift_ref, scale_ref, gate_ref, w1_ref, w2_ref, nf_ref, o_ref, lhs_scr,
                *, final_norm):
    f = pl.program_id(1)

    @pl.when(f == 0)
    def _():
        lhs_scr[...] = _rms_modulate(x_ref[...], nw_ref[...], shift_ref[...], scale_ref[...]).astype(BF16)
        o_ref[...] = jnp.zeros_like(o_ref)

    hid = jnp.dot(lhs_scr[...], w1_ref[...].astype(BF16), preferred_element_type=F32)
    hid = jnp.square(jnp.maximum(hid, 0.0)).astype(BF16)
    o_ref[...] += jnp.dot(hid, w2_ref[...].astype(BF16), preferred_element_type=F32)

    @pl.when(f == pl.num_programs(1) - 1)
    def _():
        y = x_ref[...] + gate_ref[...] * o_ref[...]
        if final_norm:
            y = y * lax.rsqrt(jnp.mean(y * y, axis=-1, keepdims=True) + EPS) * nf_ref[...]
        o_ref[...] = y


def _mlp(rows, xs, norm_w, mod3, w1, w2, layer, norm_f_w, final_norm):
    tm, tf = 1024, 512
    modspec = lambda k: pl.BlockSpec((None, 1, D_MODEL), lambda i, f: (_row_group(i, tm), 0, k))
    once = pl.Buffered(1)
    return pl.pallas_call(
        functools.partial(_mlp_kernel, final_norm=final_norm),
        out_shape=jax.ShapeDtypeStruct((rows, D_MODEL), F32),
        grid=(rows // tm, D_FF // tf),
        in_specs=[
            pl.BlockSpec((tm, D_MODEL), lambda i, f: (i, 0)),
            pl.BlockSpec((1, D_MODEL), lambda i, f: (0, 0)),
            modspec(3), modspec(4), modspec(5),
            pl.BlockSpec((None, D_MODEL, tf), lambda i, f: (layer, 0, f)),
            pl.BlockSpec((None, tf, D_MODEL), lambda i, f: (layer, f, 0)),
            pl.BlockSpec((1, D_MODEL), lambda i, f: (0, 0)),
        ],
        out_specs=pl.BlockSpec((tm, D_MODEL), lambda i, f: (i, 0), pipeline_mode=once),
        scratch_shapes=[pltpu.VMEM((tm, D_MODEL), BF16)],
        compiler_params=_cparams(("arbitrary", "arbitrary"), VMEM_LIMIT_BIG),
        name="mlp",
    )(xs, norm_w.reshape(1, D_MODEL), mod3, mod3, mod3, w1, w2, norm_f_w.reshape(1, D_MODEL))


def _rope_tables():
    quarter = HEAD_DIM // 4
    rows = jnp.repeat(jnp.arange(SEQ // GRID_W, dtype=F32), GRID_W)
    cols = jnp.tile(jnp.arange(GRID_W, dtype=F32), SEQ // GRID_W)
    inv = ROPE_BASE ** (-jnp.arange(quarter, dtype=F32) / quarter)
    ang = jnp.concatenate([rows[:, None] * inv, cols[:, None] * inv], axis=-1)
    cos, sin = jnp.cos(ang), jnp.sin(ang)
    cos2 = jnp.concatenate([cos, cos], axis=-1)
    sin2 = jnp.concatenate([-sin, sin], axis=-1)
    cos2 = jnp.concatenate([cos2, jnp.ones((CTX_LEN, HEAD_DIM), F32)], axis=0)
    sin2 = jnp.concatenate([sin2, jnp.zeros((CTX_LEN, HEAD_DIM), F32)], axis=0)
    return cos2, sin2


def _permute_w_in(w):
    s5 = w[:, :S5_WIDTH]
    ret = w[:, S5_WIDTH:S5_WIDTH + 4 * RET_WIDTH]
    ml = w[:, S5_WIDTH + 4 * RET_WIDTH:S5_WIDTH + 4 * RET_WIDTH + 4 * MLSTM_WIDTH]
    gates = w[:, S5_WIDTH + 4 * RET_WIDTH + 4 * MLSTM_WIDTH:]
    pad = jnp.zeros((D_MODEL, IN_PAD - IN_WIDTH), w.dtype)
    return jnp.concatenate([ret, ml, s5, gates, pad], axis=1).astype(BF16)


def kernel(x, c, ctx, c_ctx, w_mod, b_mod, norm1_w, norm2_w, w_in, w_out, s5_lam_re, s5_lam_im, s5_log_step, s5_b_re, s5_b_im, s5_c_re, s5_c_im, s5_d, s5_w_glu, s5_b_glu, ret_decay_logit, ret_norm_w, mlstm_igate_b, mlstm_fgate_b, mlstm_norm_w, w_ff1, w_ff2, norm_f_w):
    rope_cos, rope_sin = _rope_tables()
    cvec = jnp.concatenate([c, c_ctx[None, :], jnp.zeros((MOD_ROWS - BATCH - 1, D_MODEL), F32)], axis=0)
    mod = _modulation(cvec, w_mod, b_mod)
    xs = jnp.concatenate([x.reshape(NX, D_MODEL), ctx.reshape(NH, D_MODEL)], axis=0)
    s5w = jax.vmap(_s5_prep)(s5_lam_re, s5_lam_im, s5_log_step, s5_b_re, s5_b_im, s5_c_re, s5_c_im, s5_d)
    for l in range(DEPTH):
        last = l == DEPTH - 1
        mod3 = mod[l].reshape(MOD_ROWS, 1, 6 * D_MODEL)
        a = _in_proj(xs, norm1_w[l], mod3, _permute_w_in(w_in[l]))
        s5y = _s5_glu(_s5_mix(a, l, *s5w), s5_w_glu[l], s5_b_glu[l])
        ret_o = _retention(a, rope_cos, rope_sin, ret_decay_logit[l])
        ml_o = _mlstm(a, mlstm_igate_b[l], mlstm_fgate_b[l])
        rows = NX if last else NT
        xs = _out_proj(rows, s5y, ret_o, ml_o, a, ret_norm_w[l], mlstm_norm_w[l], xs, mod3,
                       w_out[l].astype(BF16))
        xs = _mlp(rows, xs, norm2_w[l], mod3, w_ff1, w_ff2, l, norm_f_w, last)
    return xs.reshape(BATCH, SEQ, D_MODEL)
```

```python
import functools
import math

import jax
import jax.numpy as jnp
from jax import lax
from jax.experimental import pallas as pl
from jax.experimental.pallas import tpu as pltpu

F32 = jnp.float32
BF16 = jnp.bfloat16

D_MODEL = 2048
BATCH = 4
SEQ = 2048
DEPTH = 2
CTX_LEN = 256
GRID_W = 64
HEAD_DIM = 128
S5_WIDTH = 512
S5_GROUP = 16
S5_GROUPS = 32
S5_STATE = 64
RET_WIDTH = 768
MLSTM_WIDTH = 768
HEADS = 6
IN_WIDTH = S5_WIDTH + 4 * RET_WIDTH + 4 * MLSTM_WIDTH + 4 * HEADS
D_FF = 4 * D_MODEL
CHUNK = 128
ROPE_BASE = 10000.0
EPS = 1e-6

NX = BATCH * SEQ
NH = BATCH * CTX_LEN
NT = NX + NH
MOD_ROWS = 8
CTX_GROUP = BATCH

COL_RET = 0
COL_ML = 4 * RET_WIDTH
COL_S5 = COL_ML + 4 * MLSTM_WIDTH
COL_GATE = COL_S5 + S5_WIDTH
IN_PAD = 7168

X_CHUNKS = SEQ // CHUNK
H_CHUNKS = CTX_LEN // CHUNK
MIX_STEPS = X_CHUNKS + H_CHUNKS

S5_T = 8
S5_LANES = 128
S5_LG = S5_LANES // S5_GROUP
S5_TILES = S5_WIDTH // S5_LANES
S5_K = S5_T * S5_LANES
S5_XB = SEQ // S5_T
S5_HB = CTX_LEN // S5_T
S5_NB = S5_XB + S5_HB
S5_ROWS = S5_NB * BATCH
S5_RC = 128

VMEM_LIMIT = 48 * 1024 * 1024
VMEM_LIMIT_BIG = 56 * 1024 * 1024


def _cparams(sem, limit=VMEM_LIMIT):
    return pltpu.CompilerParams(dimension_semantics=sem, vmem_limit_bytes=limit)


def _sigmoid(x):
    return 1.0 / (1.0 + jnp.exp(-x))


def _silu(x):
    return x * _sigmoid(x)


def _log_sigmoid(x):
    return jnp.minimum(x, 0.0) - jnp.log1p(jnp.exp(-jnp.abs(x)))


def _gelu_tanh(x):
    c = math.sqrt(2.0 / math.pi)
    return 0.5 * x * (1.0 + jnp.tanh(c * (x + 0.044715 * (x * x * x))))


def _row_group(i, tm):
    return jnp.minimum((i * tm) // SEQ, CTX_GROUP)


def _mod_kernel(c_ref, w_ref, b_ref, o_ref):
    c = c_ref[...]
    lhs = _silu(c).astype(BF16)
    o_ref[...] = jnp.dot(lhs, w_ref[...].astype(BF16), preferred_element_type=F32) + b_ref[...]


def _modulation(cvec, w_mod, b_mod):
    tn = 512
    n = 6 * D_MODEL
    return pl.pallas_call(
        _mod_kernel,
        out_shape=jax.ShapeDtypeStruct((DEPTH, MOD_ROWS, n), F32),
        grid=(DEPTH, n // tn),
        in_specs=[
            pl.BlockSpec((MOD_ROWS, D_MODEL), lambda l, j: (0, 0)),
            pl.BlockSpec((None, D_MODEL, tn), lambda l, j: (l, 0, j)),
            pl.BlockSpec((None, 1, tn), lambda l, j: (l, 0, j)),
        ],
        out_specs=pl.BlockSpec((None, MOD_ROWS, tn), lambda l, j: (l, 0, j)),
        compiler_params=_cparams(("arbitrary", "arbitrary")),
        name="modulation",
    )(cvec, w_mod, b_mod.reshape(DEPTH, 1, n))


def _rms_modulate(x, nw, shift, scale):
    y = x * lax.rsqrt(jnp.mean(x * x, axis=-1, keepdims=True) + EPS) * nw
    return y * (1.0 + scale) + shift


def _in_proj_kernel(x_ref, nw_ref, shift_ref, scale_ref, w_ref, o_ref, lhs_scr):
    @pl.when(pl.program_id(1) == 0)
    def _():
        lhs_scr[...] = _rms_modulate(x_ref[...], nw_ref[...], shift_ref[...], scale_ref[...]).astype(BF16)

    o_ref[...] = jnp.dot(lhs_scr[...], w_ref[...], preferred_element_type=F32)


def _in_proj(xs, norm_w, mod3, w_in_p):
    tm, tn = 1024, 1024
    return pl.pallas_call(
        _in_proj_kernel,
        out_shape=jax.ShapeDtypeStruct((NT, IN_PAD), F32),
        grid=(NT // tm, IN_PAD // tn),
        in_specs=[
            pl.BlockSpec((tm, D_MODEL), lambda i, j: (i, 0)),
            pl.BlockSpec((1, D_MODEL), lambda i, j: (0, 0)),
            pl.BlockSpec((None, 1, D_MODEL), lambda i, j: (_row_group(i, tm), 0, 0)),
            pl.BlockSpec((None, 1, D_MODEL), lambda i, j: (_row_group(i, tm), 0, 1)),
            pl.BlockSpec((D_MODEL, tn), lambda i, j: (0, j)),
        ],
        out_specs=pl.BlockSpec((tm, tn), lambda i, j: (i, j)),
        scratch_shapes=[pltpu.VMEM((tm, D_MODEL), BF16)],
        compiler_params=_cparams(("arbitrary", "arbitrary")),
        name="in_proj",
    )(xs, norm_w.reshape(1, D_MODEL), mod3, mod3, w_in_p)


def _s5_prep(lam_re, lam_im, log_step, b_re, b_im, c_re, c_im, d_skip):
    hp = lax.Precision.HIGHEST
    lam_re = jnp.minimum(lam_re.astype(F32), -1e-4)
    lam_im = lam_im.astype(F32)
    step = jnp.exp(log_step.astype(F32))[..., None]
    mag = jnp.exp(lam_re * step)
    ab_re, ab_im = mag * jnp.cos(lam_im * step), mag * jnp.sin(lam_im * step)
    den = lam_re * lam_re + lam_im * lam_im
    ir, ii = lam_re / den, -lam_im / den
    nr = (ab_re - 1.0) * ir - ab_im * ii
    ni = (ab_re - 1.0) * ii + ab_im * ir
    bre, bim = b_re.astype(F32), b_im.astype(F32)
    bb_re = nr[..., None] * bre - ni[..., None] * bim
    bb_im = nr[..., None] * bim + ni[..., None] * bre
    lpow = jnp.arange(S5_T + 1, dtype=F32)[:, None, None, None]
    pmag = jnp.exp(lpow * (lam_re * step))
    pr, pi = pmag * jnp.cos(lpow * (lam_im * step)), pmag * jnp.sin(lpow * (lam_im * step))
    cre, cim = c_re.astype(F32), c_im.astype(F32)
    ca_r = cre[None] * pr[:, :, :, None, :] - cim[None] * pi[:, :, :, None, :]
    ca_i = cre[None] * pi[:, :, :, None, :] + cim[None] * pr[:, :, :, None, :]
    kl = (jnp.einsum('ldgnp,dgpm->ldgnm', ca_r, bb_re, precision=hp)
          - jnp.einsum('ldgnp,dgpm->ldgnm', ca_i, bb_im, precision=hp))
    tt = jnp.arange(S5_T)
    lag = tt[None, :] - tt[:, None]
    kf = jnp.where((lag >= 0)[:, :, None, None, None], kl[jnp.clip(lag, 0, S5_T), 0], 0.0)
    kb = jnp.where((lag <= 0)[:, :, None, None, None], kl[jnp.clip(-lag, 0, S5_T), 1], 0.0)
    nt = S5_TILES
    row = jnp.arange(S5_K)
    col = jnp.arange(S5_K)
    lane = jnp.arange(S5_LANES)
    g_blk = (row // S5_GROUP) % S5_LG
    g_state = 2 * (row // (4 * S5_STATE)) + (row // S5_STATE) % 2
    to_blk = ((lane[:, None] // S5_GROUP == col[None, :] // S5_LANES)
              & (lane[:, None] % S5_GROUP == col[None, :] % S5_GROUP)).astype(BF16)
    to_state = ((lane[:, None] // S5_STATE == (col[None, :] // S5_LANES) % 2)
                & (lane[:, None] % S5_STATE == col[None, :] % S5_STATE)).astype(BF16)

    def expand(compact, spread, g_row, g_col):
        wide = jnp.einsum('trk,kc->trc', compact.reshape(nt, S5_K, S5_LANES).astype(BF16), spread,
                          preferred_element_type=BF16)
        return jnp.where(g_row[:, None] == g_col[None, :], wide, jnp.zeros((), BF16))

    mc = (kf + kb).reshape(S5_T, S5_T, nt, S5_LG, S5_GROUP, S5_GROUP).transpose(2, 0, 3, 5, 1, 4)
    m = expand(mc, to_blk, g_blk, g_blk)

    def carry_in(d, powers):
        er = pr[powers, d][..., None] * bb_re[d][None] - pi[powers, d][..., None] * bb_im[d][None]
        ei = pr[powers, d][..., None] * bb_im[d][None] + pi[powers, d][..., None] * bb_re[d][None]
        arrange = lambda t: t.reshape(S5_T, nt, S5_LG, S5_STATE, S5_GROUP).transpose(1, 0, 2, 4, 3)
        es = jnp.stack([arrange(er), arrange(ei)], axis=4)
        return expand(es, to_state, g_blk, g_state)

    def read_out(d, powers):
        arrange = lambda t: t.reshape(S5_T, nt, S5_LG // 2, 2, S5_GROUP, S5_STATE).transpose(1, 2, 3, 5, 0, 4)
        fs = jnp.stack([arrange(ca_r[powers, d]), arrange(-ca_i[powers, d])], axis=2)
        return expand(fs, to_blk, g_state, g_blk)

    ef, eb = carry_in(0, S5_T - 1 - tt), carry_in(1, tt)
    ff, fb = read_out(0, tt + 1), read_out(1, S5_T - tt)
    lanes = lambda t: t.reshape(nt, S5_K // 2)
    a_r = jnp.concatenate([lanes(pr[S5_T, 0]), lanes(pr[S5_T, 1])], axis=-1)
    a_i = jnp.concatenate([lanes(pi[S5_T, 0]), lanes(pi[S5_T, 1])], axis=-1)
    dsk = jnp.pad(d_skip.astype(F32).reshape(nt, S5_LANES), ((0, 0), (0, S5_K - S5_LANES)))
    aux = jnp.stack([a_r, a_i, dsk] + [jnp.zeros_like(a_r)] * 5, axis=1)
    return ef, eb, m, ff, fb, aux


def _s5_kernel(a_ref, ef_ref, eb_ref, m_ref, ff_ref, fb_ref, aux_ref, y_ref, u_scr, wf_scr, wb_scr):
    for b in range(BATCH):
        for t in range(S5_T):
            u_scr.at[t][pl.ds(S5_HB * BATCH + b, S5_XB, stride=BATCH), :] = (
                a_ref[pl.ds(b * SEQ + t, S5_XB, stride=S5_T), :])
            u_scr.at[t][pl.ds(b, S5_HB, stride=BATCH), :] = a_ref[pl.ds(NX + b * CTX_LEN + t, S5_HB, stride=S5_T), :]

    def get_tiles(scr, r0):
        return jnp.concatenate([scr.at[t][pl.ds(r0, S5_RC), :] for t in range(S5_T)], axis=1).astype(BF16)

    def put_tiles(scr, r0, val):
        for t in range(S5_T):
            scr.at[t][pl.ds(r0, S5_RC), :] = val[:, t * S5_LANES:(t + 1) * S5_LANES]

    def block_rows(rc):
        r0 = pl.multiple_of(rc * S5_RC, S5_RC)
        return r0, get_tiles(u_scr, r0)

    def carry_in(rc, _):
        r0, ub = block_rows(rc)
        put_tiles(wf_scr, r0, jnp.dot(ub, ef_ref[...], preferred_element_type=F32))
        put_tiles(wb_scr, r0, jnp.dot(ub, eb_ref[...], preferred_element_type=F32))
        return 0

    lax.fori_loop(0, S5_ROWS // S5_RC, carry_in, 0)

    aux = aux_ref[...]
    npair = S5_LG // 2
    shape = (BATCH, S5_LANES)
    coef = []
    for d in range(2):
        for j in range(npair):
            lo = d * (S5_K // 2) + j * S5_LANES
            coef.append((jnp.broadcast_to(aux[0:1, lo:lo + S5_LANES], shape),
                         jnp.broadcast_to(aux[1:2, lo:lo + S5_LANES], shape)))

    def scan(s, carry):
        rf = pl.multiple_of(s * BATCH, BATCH)
        gb = jnp.where(s < S5_HB, S5_HB - 1 - s, S5_NB + S5_HB - 1 - s)
        rb = pl.multiple_of(gb * BATCH, BATCH)
        out = []
        for d, (w_scr, r) in enumerate(((wf_scr, rf), (wb_scr, rb))):
            for j in range(npair):
                xr, xi = carry[d * npair + j]
                ar, ai = coef[d * npair + j]
                w_re, w_im = w_scr.at[2 * j], w_scr.at[2 * j + 1]
                wr = w_re[pl.ds(r, BATCH), :]
                wi = w_im[pl.ds(r, BATCH), :]
                w_re[pl.ds(r, BATCH), :] = xr
                w_im[pl.ds(r, BATCH), :] = xi
                out.append((ar * xr - ai * xi + wr, ar * xi + ai * xr + wi))
        return tuple(out)

    z = jnp.zeros(shape, F32)
    lax.fori_loop(0, S5_NB, scan, tuple((z, z) for _ in range(2 * npair)))

    def read_out(rc, _):
        r0, ub = block_rows(rc)
        y = jnp.dot(ub, m_ref[...], preferred_element_type=F32)
        y = y + jnp.dot(get_tiles(wf_scr, r0), ff_ref[...], preferred_element_type=F32)
        y = y + jnp.dot(get_tiles(wb_scr, r0), fb_ref[...], preferred_element_type=F32)
        put_tiles(u_scr, r0, y)
        return 0

    lax.fori_loop(0, S5_ROWS // S5_RC, read_out, 0)

    for b in range(BATCH):
        for t in range(S5_T):
            y_ref[pl.ds(b * SEQ + t, S5_XB, stride=S5_T), :] = (
                u_scr.at[t][pl.ds(S5_HB * BATCH + b, S5_XB, stride=BATCH), :])
            y_ref[pl.ds(NX + b * CTX_LEN + t, S5_HB, stride=S5_T), :] = u_scr.at[t][pl.ds(b, S5_HB, stride=BATCH), :]
    y_ref[...] = y_ref[...] + a_ref[...] * aux[2:3, 0:S5_LANES]


def _s5_mix(a, layer, ef, eb, m, ff, fb, aux):
    wspec = lambda: pl.BlockSpec((None, None, S5_K, S5_K), lambda g: (layer, g, 0, 0))
    return pl.pallas_call(
        _s5_kernel,
        out_shape=jax.ShapeDtypeStruct((NT, S5_WIDTH), F32),
        grid=(S5_TILES,),
        in_specs=[
            pl.BlockSpec((NT, S5_LANES), lambda g: (0, COL_S5 // S5_LANES + g)),
            wspec(), wspec(), wspec(), wspec(), wspec(),
            pl.BlockSpec((None, None, 8, S5_K), lambda g: (layer, g, 0, 0)),
        ],
        out_specs=pl.BlockSpec((NT, S5_LANES), lambda g: (0, g)),
        scratch_shapes=[pltpu.VMEM((S5_T, S5_ROWS, S5_LANES), F32)] * 3,
        compiler_params=_cparams(("arbitrary",), VMEM_LIMIT_BIG),
        name="s5_scan",
    )(a, ef, eb, m, ff, fb, aux)


def _glu_kernel(y_ref, w_ref, b_ref, o_ref):
    g = _gelu_tanh(y_ref[...]).astype(BF16)
    z = jnp.dot(g, w_ref[...], preferred_element_type=F32) + b_ref[...]
    o_ref[...] = z[:, :S5_WIDTH] * _sigmoid(z[:, S5_WIDTH:])


def _s5_glu(y, w_glu, b_glu):
    tm = 1024
    return pl.pallas_call(
        _glu_kernel,
        out_shape=jax.ShapeDtypeStruct((NT, S5_WIDTH), F32),
        grid=(NT // tm,),
        in_specs=[
            pl.BlockSpec((tm, S5_WIDTH), lambda i: (i, 0)),
            pl.BlockSpec((S5_WIDTH, 2 * S5_WIDTH), lambda i: (0, 0)),
            pl.BlockSpec((1, 2 * S5_WIDTH), lambda i: (0, 0)),
        ],
        out_specs=pl.BlockSpec((tm, S5_WIDTH), lambda i: (i, 0)),
        compiler_params=_cparams(("arbitrary",)),
        name="s5_glu",
    )(y, w_glu.astype(BF16), b_glu.reshape(1, 2 * S5_WIDTH))


def _chunk_index(d, s):
    is_ctx = s < H_CHUNKS
    fwd = jnp.where(is_ctx, s, s - H_CHUNKS)
    bwd = jnp.where(is_ctx, H_CHUNKS - 1 - s, MIX_STEPS - 1 - s)
    return is_ctx, jnp.where(d == 0, fwd, bwd)


def _token_block(b, d, s):
    is_ctx, c = _chunk_index(d, s)
    return jnp.where(is_ctx, NX // CHUNK + b * H_CHUNKS + c, b * X_CHUNKS + c)


def _rope_block(b, d, s):
    is_ctx, c = _chunk_index(d, s)
    return jnp.where(is_ctx, X_CHUNKS + c, c)


def _dot_nt(a, b):
    return lax.dot_general(a, b, (((1,), (1,)), ((), ())), preferred_element_type=F32)


def _dot_tn(a, b):
    return lax.dot_general(a, b, (((0,), (0,)), ((), ())), preferred_element_type=F32)


def _dot(a, b):
    return jnp.dot(a, b, preferred_element_type=F32)


def _head(ref, h):
    return ref[:, h * HEAD_DIM:(h + 1) * HEAD_DIM]


def _chunk_pos(rev):
    i = lax.broadcasted_iota(jnp.int32, (CHUNK, CHUNK), 0)
    j = lax.broadcasted_iota(jnp.int32, (CHUNK, CHUNK), 1)
    pi = jnp.where(rev, CHUNK - 1 - i, i)
    pj = jnp.where(rev, CHUNK - 1 - j, j)
    return pi, pj


def _ret_kernel(dl_ref, qf_ref, kf_ref, vf_ref, cosf_ref, sinf_ref, qb_ref, kb_ref, vb_ref, cosb_ref, sinb_ref,
                of_ref, ob_ref, r_scr, dec_scr, qd_scr, kd_scr, cd_scr):
    s = pl.program_id(1)

    @pl.when(s == 0)
    def _():
        for d in range(2):
            pi, pj = _chunk_pos(d == 1)
            rel = (pi - pj).astype(F32)
            pif = pi.astype(F32)
            for h in range(HEADS):
                x = d * HEADS + h
                lg = _log_sigmoid(dl_ref[d, h])[0:1, :]
                dec_scr[x] = jnp.where(rel >= 0, jnp.exp(lg * jnp.maximum(rel, 0.0)), 0.0)
                qd_scr[x] = jnp.exp(lg * (pif + 1.0))
                kd_scr[x] = jnp.exp(lg * (CHUNK - 1.0 - pif))
                cd_scr[x] = jnp.exp(jnp.broadcast_to(lg, (8, HEAD_DIM)) * CHUNK)
        r_scr[...] = jnp.zeros_like(r_scr)

    scale = HEAD_DIM ** -0.5
    dirs = ((qf_ref, kf_ref, vf_ref, cosf_ref, sinf_ref, of_ref), (qb_ref, kb_ref, vb_ref, cosb_ref, sinb_ref, ob_ref))
    ch = [(d, h) for d in range(2) for h in range(HEADS)]
    xs = range(len(ch))
    cos = [dirs[d][3][...] for d in range(2)]
    sin = [dirs[d][4][...] for d in range(2)]
    q = [_head(dirs[d][0], h) for d, h in ch]
    k = [_head(dirs[d][1], h) for d, h in ch]
    v = [_head(dirs[d][2], h).astype(BF16) for d, h in ch]
    q = [q[x] * cos[ch[x][0]] + pltpu.roll(q[x], HEAD_DIM // 2, 1) * sin[ch[x][0]] for x in xs]
    k = [(k[x] * cos[ch[x][0]] + pltpu.roll(k[x], HEAD_DIM // 2, 1) * sin[ch[x][0]]) * scale for x in xs]
    r = [r_scr[x] for x in xs]
    sm = [_dot_nt(q[x].astype(BF16), k[x].astype(BF16)) * dec_scr[x] for x in xs]
    inter = [_dot((q[x] * qd_scr[x]).astype(BF16), r[x].astype(BF16)) for x in xs]
    kv = [_dot_tn((k[x] * kd_scr[x]).astype(BF16), v[x]) for x in xs]
    for x in xs:
        d, h = ch[x]
        dirs[d][5][:, h * HEAD_DIM:(h + 1) * HEAD_DIM] = _dot(sm[x].astype(BF16), v[x]) + inter[x]
        r_scr[x] = cd_scr[x][0:1, :] * r[x] + kv[x]


def _retention(a, rope_cos, rope_sin, decay_logit):
    dl = jnp.broadcast_to(decay_logit.astype(F32)[:, :, None, None], (2, HEADS, 8, HEAD_DIM))
    col0 = COL_RET // RET_WIDTH
    specs = [pl.BlockSpec((2, HEADS, 8, HEAD_DIM), lambda b, s: (0, 0, 0, 0))]
    for d in range(2):
        for col in (col0, col0 + 1, col0 + 2):
            specs.append(pl.BlockSpec((CHUNK, RET_WIDTH), lambda b, s, d=d, col=col: (_token_block(b, d, s), col)))
        specs += [pl.BlockSpec((CHUNK, HEAD_DIM), lambda b, s, d=d: (_rope_block(b, d, s), 0))] * 2
    hh = pltpu.VMEM((2 * HEADS, CHUNK, HEAD_DIM), F32)
    out = jax.ShapeDtypeStruct((NT, RET_WIDTH), F32)
    return pl.pallas_call(
        _ret_kernel,
        out_shape=(out, out),
        grid=(BATCH, MIX_STEPS),
        in_specs=specs,
        out_specs=[pl.BlockSpec((CHUNK, RET_WIDTH), lambda b, s, d=d: (_token_block(b, d, s), 0)) for d in range(2)],
        scratch_shapes=[hh, hh, hh, hh, pltpu.VMEM((2 * HEADS, 8, HEAD_DIM), F32)],
        compiler_params=_cparams(("arbitrary", "arbitrary")),
        name="retention",
    )(dl, a, a, a, rope_cos, rope_sin, a, a, a, rope_cos, rope_sin)


def _split3(x):
    h = x.astype(BF16)
    r = x - h.astype(F32)
    m = r.astype(BF16)
    return h, m, (r - m.astype(F32)).astype(BF16)


def _mlstm_kernel(qf_ref, kf_ref, vf_ref, gf_ref, qb_ref, kb_ref, vb_ref, gb_ref, bias_ref, of_ref, ob_ref,
                  c_scr, n_scr, m_scr):
    s = pl.program_id(1)

    @pl.when(s == 0)
    def _():
        c_scr[...] = jnp.zeros_like(c_scr)
        n_scr[...] = jnp.zeros_like(n_scr)
        m_scr[...] = jnp.zeros_like(m_scr)

    scale = HEAD_DIM ** -0.5
    neg_inf = -jnp.inf
    dirs = ((qf_ref, kf_ref, vf_ref, gf_ref, of_ref), (qb_ref, kb_ref, vb_ref, gb_ref, ob_ref))
    causal, pre, pre_t, b_cols, b_rows = [], [], [], [], []
    for d in range(2):
        pi, pj = _chunk_pos(d == 1)
        cz = pj <= pi
        tri = jnp.where(cz, 1.0, 0.0).astype(BF16)
        p = dirs[d][3][...] + bias_ref[...]
        pt = p.T[0:32, :]
        causal.append(cz)
        pre.append(p)
        pre_t.append(pt)
        b_cols.append(sum(_dot(tri, part) for part in _split3(_log_sigmoid(p))))
        b_rows.append(sum(_dot_nt(part, tri) for part in _split3(_log_sigmoid(pt))))
    ch = [(d, h) for d in range(2) for h in range(HEADS)]
    xs = range(len(ch))
    ci = [2 * d * HEADS + h for d, h in ch]
    cf = [(2 * d + 1) * HEADS + h for d, h in ch]
    last = [0 if d == 1 else CHUNK - 1 for d, h in ch]
    i_col = [pre[ch[x][0]][:, ci[x]:ci[x] + 1] for x in xs]
    b_col = [b_cols[ch[x][0]][:, cf[x]:cf[x] + 1] for x in xs]
    i_row = [pre_t[ch[x][0]][ci[x]:ci[x] + 1, :] for x in xs]
    b_row = [b_rows[ch[x][0]][cf[x]:cf[x] + 1, :] for x in xs]
    b_end = [b_cols[ch[x][0]][last[x]:last[x] + 1, cf[x]:cf[x] + 1] for x in xs]
    m_prev = [m_scr[x][0:1, 0:1] for x in xs]
    q = [_head(dirs[d][0], h) for d, h in ch]
    k = [_head(dirs[d][1], h) * scale for d, h in ch]
    v = [_head(dirs[d][2], h).astype(BF16) for d, h in ch]
    qb = [q[x].astype(BF16) for x in xs]
    c_mem = [c_scr[x] for x in xs]
    n_mem = [n_scr[x][0:1, :] for x in xs]
    log_w = [jnp.where(causal[ch[x][0]], b_col[x] - b_row[x] + i_row[x], neg_inf) for x in xs]
    log_a = [b_col[x] + m_prev[x] for x in xs]
    m_t = [jnp.maximum(log_a[x], jnp.max(log_w[x], axis=-1, keepdims=True)) for x in xs]
    w = [jnp.exp(log_w[x] - m_t[x]) for x in xs]
    a = [jnp.exp(log_a[x] - m_t[x]) for x in xs]
    sm = [_dot_nt(qb[x], k[x].astype(BF16)) * w[x] for x in xs]
    inter = [_dot(qb[x], c_mem[x].astype(BF16)) for x in xs]
    num = [_dot(sm[x].astype(BF16), v[x]) + a[x] * inter[x] for x in xs]
    den = [jnp.sum(sm[x] + a[x] * (q[x] * n_mem[x]), axis=-1, keepdims=True) for x in xs]
    for x in xs:
        d, h = ch[x]
        dirs[d][4][:, h * HEAD_DIM:(h + 1) * HEAD_DIM] = num[x] / jnp.maximum(jnp.abs(den[x]), jnp.exp(-m_t[x]))
    log_w_end = [b_end[x] - b_col[x] + i_col[x] for x in xs]
    m_new = [jnp.maximum(b_end[x] + m_prev[x], jnp.max(log_w_end[x], axis=0, keepdims=True)) for x in xs]
    a_end = [jnp.exp(b_end[x] + m_prev[x] - m_new[x]) for x in xs]
    kw = [k[x] * jnp.exp(log_w_end[x] - m_new[x]) for x in xs]
    kv = [_dot_tn(kw[x].astype(BF16), v[x]) for x in xs]
    for x in xs:
        c_scr[x] = a_end[x] * c_mem[x] + kv[x]
        n_scr[x] = jnp.broadcast_to(a_end[x] * n_mem[x] + jnp.sum(kw[x], axis=0, keepdims=True), (8, HEAD_DIM))
        m_scr[x] = jnp.broadcast_to(m_new[x], (8, HEAD_DIM))


def _mlstm(a, igate_b, fgate_b):
    bias = jnp.stack([igate_b.astype(F32), fgate_b.astype(F32)], axis=1).reshape(1, 4 * HEADS)
    bias = jnp.pad(bias, ((0, 0), (0, HEAD_DIM - 4 * HEADS)))
    col0 = COL_ML // MLSTM_WIDTH
    specs = []
    for d in range(2):
        for col in (col0, col0 + 1, col0 + 2):
            specs.append(pl.BlockSpec((CHUNK, MLSTM_WIDTH), lambda b, s, d=d, col=col: (_token_block(b, d, s), col)))
        specs.append(pl.BlockSpec((CHUNK, HEAD_DIM), lambda b, s, d=d: (_token_block(b, d, s), COL_GATE // HEAD_DIM)))
    specs.append(pl.BlockSpec((1, HEAD_DIM), lambda b, s: (0, 0)))
    small = pltpu.VMEM((2 * HEADS, 8, HEAD_DIM), F32)
    out = jax.ShapeDtypeStruct((NT, MLSTM_WIDTH), F32)
    return pl.pallas_call(
        _mlstm_kernel,
        out_shape=(out, out),
        grid=(BATCH, MIX_STEPS),
        in_specs=specs,
        out_specs=[pl.BlockSpec((CHUNK, MLSTM_WIDTH), lambda b, s, d=d: (_token_block(b, d, s), 0))
                   for d in range(2)],
        scratch_shapes=[pltpu.VMEM((2 * HEADS, CHUNK, HEAD_DIM), F32), small, small],
        compiler_params=_cparams(("arbitrary", "arbitrary")),
        name="mlstm",
    )(a, a, a, a, a, a, a, a, bias)


def _head_norm(o, center):
    if center:
        o = o - jnp.mean(o, axis=-1, keepdims=True)
    return o * lax.rsqrt(jnp.mean(o * o, axis=-1, keepdims=True) + EPS)


def _out_proj_kernel(s5_ref, rf_ref, rb_ref, rg_ref, rw_ref, mf_ref, mb_ref, mg_ref, mw_ref, x_ref, gate_ref,
                     w_ref, o_ref):
    pair = 2 * HEAD_DIM
    acc = jnp.dot(s5_ref[:, 0:pair].astype(BF16), w_ref[0:pair, :], preferred_element_type=F32)
    acc += jnp.dot(s5_ref[:, pair:2 * pair].astype(BF16), w_ref[pair:2 * pair, :], preferred_element_type=F32)
    for hp in range(HEADS // 2):
        rs, ms = [], []
        for h in (2 * hp, 2 * hp + 1):
            sl = slice(h * HEAD_DIM, (h + 1) * HEAD_DIM)
            r = _head_norm(rf_ref[:, sl] + rb_ref[:, sl], True) * rw_ref[:, sl] * _silu(rg_ref[:, sl])
            m = _head_norm(mf_ref[:, sl] + mb_ref[:, sl], False) * mw_ref[:, sl] * _sigmoid(mg_ref[:, sl])
            rs.append(r.astype(BF16))
            ms.append(m.astype(BF16))
        r0 = S5_WIDTH + hp * pair
        m0 = S5_WIDTH + RET_WIDTH + hp * pair
        acc += jnp.dot(jnp.concatenate(rs, axis=1), w_ref[r0:r0 + pair, :], preferred_element_type=F32)
        acc += jnp.dot(jnp.concatenate(ms, axis=1), w_ref[m0:m0 + pair, :], preferred_element_type=F32)
    o_ref[...] = x_ref[...] + gate_ref[...] * acc


def _out_proj(rows, s5y, ret_o, ml_o, a, ret_norm_w, mlstm_norm_w, xs, mod3, w_out):
    tm = 256
    return pl.pallas_call(
        _out_proj_kernel,
        out_shape=jax.ShapeDtypeStruct((rows, D_MODEL), F32),
        grid=(rows // tm,),
        in_specs=[
            pl.BlockSpec((tm, S5_WIDTH), lambda i: (i, 0)),
            pl.BlockSpec((tm, RET_WIDTH), lambda i: (i, 0)),
            pl.BlockSpec((tm, RET_WIDTH), lambda i: (i, 0)),
            pl.BlockSpec((tm, RET_WIDTH), lambda i: (i, COL_RET // RET_WIDTH + 3)),
            pl.BlockSpec((1, RET_WIDTH), lambda i: (0, 0)),
            pl.BlockSpec((tm, MLSTM_WIDTH), lambda i: (i, 0)),
            pl.BlockSpec((tm, MLSTM_WIDTH), lambda i: (i, 0)),
            pl.BlockSpec((tm, MLSTM_WIDTH), lambda i: (i, COL_ML // MLSTM_WIDTH + 3)),
            pl.BlockSpec((1, MLSTM_WIDTH), lambda i: (0, 0)),
            pl.BlockSpec((tm, D_MODEL), lambda i: (i, 0)),
            pl.BlockSpec((None, 1, D_MODEL), lambda i: (_row_group(i, tm), 0, 2)),
            pl.BlockSpec((D_MODEL, D_MODEL), lambda i: (0, 0), pipeline_mode=pl.Buffered(1)),
        ],
        out_specs=pl.BlockSpec((tm, D_MODEL), lambda i: (i, 0)),
        compiler_params=_cparams(("arbitrary",)),
        name="out_proj",
    )(s5y, ret_o[0], ret_o[1], a, ret_norm_w.reshape(1, RET_WIDTH), ml_o[0], ml_o[1], a,
      mlstm_norm_w.reshape(1, MLSTM_WIDTH),
      xs, mod3, w_out)


def _mlp_kernel(x_ref, nw_ref, shift_ref, scale_ref, gate_ref, w1_ref, w2_ref, nf_ref, o_ref, lhs_scr,
                *, final_norm):
    f = pl.program_id(1)

    @pl.when(f == 0)
    def _():
        lhs_scr[...] = _rms_modulate(x_ref[...], nw_ref[...], shift_ref[...], scale_ref[...]).astype(BF16)
        o_ref[...] = jnp.zeros_like(o_ref)

    hid = jnp.dot(lhs_scr[...], w1_ref[...].astype(BF16), preferred_element_type=F32)
    hid = jnp.square(jnp.maximum(hid, 0.0)).astype(BF16)
    o_ref[...] += jnp.dot(hid, w2_ref[...].astype(BF16), preferred_element_type=F32)

    @pl.when(f == pl.num_programs(1) - 1)
    def _():
        y = x_ref[...] + gate_ref[...] * o_ref[...]
        if final_norm:
            y = y * lax.rsqrt(jnp.mean(y * y, axis=-1, keepdims=True) + EPS) * nf_ref[...]
        o_ref[...] = y


def _mlp(rows, xs, norm_w, mod3, w1, w2, layer, norm_f_w, final_norm):
    tm, tf = 1024, 512
    modspec = lambda k: pl.BlockSpec((None, 1, D_MODEL), lambda i, f: (_row_group(i, tm), 0, k))
    once = pl.Buffered(1)
    return pl.pallas_call(
        functools.partial(_mlp_kernel, final_norm=final_norm),
        out_shape=jax.ShapeDtypeStruct((rows, D_MODEL), F32),
        grid=(rows // tm, D_FF // tf),
        in_specs=[
            pl.BlockSpec((tm, D_MODEL), lambda i, f: (i, 0)),
            pl.BlockSpec((1, D_MODEL), lambda i, f: (0, 0)),
            modspec(3), modspec(4), modspec(5),
            pl.BlockSpec((None, D_MODEL, tf), lambda i, f: (layer, 0, f)),
            pl.BlockSpec((None, tf, D_MODEL), lambda i, f: (layer, f, 0)),
            pl.BlockSpec((1, D_MODEL), lambda i, f: (0, 0)),
        ],
        out_specs=pl.BlockSpec((tm, D_MODEL), lambda i, f: (i, 0), pipeline_mode=once),
        scratch_shapes=[pltpu.VMEM((tm, D_MODEL), BF16)],
        compiler_params=_cparams(("arbitrary", "arbitrary"), VMEM_LIMIT_BIG),
        name="mlp",
    )(xs, norm_w.reshape(1, D_MODEL), mod3, mod3, mod3, w1, w2, norm_f_w.reshape(1, D_MODEL))


def _rope_tables():
    quarter = HEAD_DIM // 4
    rows = jnp.repeat(jnp.arange(SEQ // GRID_W, dtype=F32), GRID_W)
    cols = jnp.tile(jnp.arange(GRID_W, dtype=F32), SEQ // GRID_W)
    inv = ROPE_BASE ** (-jnp.arange(quarter, dtype=F32) / quarter)
    ang = jnp.concatenate([rows[:, None] * inv, cols[:, None] * inv], axis=-1)
    cos, sin = jnp.cos(ang), jnp.sin(ang)
    cos2 = jnp.concatenate([cos, cos], axis=-1)
    sin2 = jnp.concatenate([-sin, sin], axis=-1)
    cos2 = jnp.concatenate([cos2, jnp.ones((CTX_LEN, HEAD_DIM), F32)], axis=0)
    sin2 = jnp.concatenate([sin2, jnp.zeros((CTX_LEN, HEAD_DIM), F32)], axis=0)
    return cos2, sin2


def _permute_w_in(w):
    s5 = w[:, :S5_WIDTH]
    ret = w[:, S5_WIDTH:S5_WIDTH + 4 * RET_WIDTH]
    ml = w[:, S5_WIDTH + 4 * RET_WIDTH:S5_WIDTH + 4 * RET_WIDTH + 4 * MLSTM_WIDTH]
    gates = w[:, S5_WIDTH + 4 * RET_WIDTH + 4 * MLSTM_WIDTH:]
    pad = jnp.zeros((D_MODEL, IN_PAD - IN_WIDTH), w.dtype)
    return jnp.concatenate([ret, ml, s5, gates, pad], axis=1).astype(BF16)


def kernel(x, c, ctx, c_ctx, w_mod, b_mod, norm1_w, norm2_w, w_in, w_out, s5_lam_re, s5_lam_im, s5_log_step, s5_b_re, s5_b_im, s5_c_re, s5_c_im, s5_d, s5_w_glu, s5_b_glu, ret_decay_logit, ret_norm_w, mlstm_igate_b, mlstm_fgate_b, mlstm_norm_w, w_ff1, w_ff2, norm_f_w):
    rope_cos, rope_sin = _rope_tables()
    cvec = jnp.concatenate([c, c_ctx[None, :], jnp.zeros((MOD_ROWS - BATCH - 1, D_MODEL), F32)], axis=0)
    mod = _modulation(cvec, w_mod, b_mod)
    xs = jnp.concatenate([x.reshape(NX, D_MODEL), ctx.reshape(NH, D_MODEL)], axis=0)
    s5w = jax.vmap(_s5_prep)(s5_lam_re, s5_lam_im, s5_log_step, s5_b_re, s5_b_im, s5_c_re, s5_c_im, s5_d)
    for l in range(DEPTH):
        last = l == DEPTH - 1
        mod3 = mod[l].reshape(MOD_ROWS, 1, 6 * D_MODEL)
        a = _in_proj(xs, norm1_w[l], mod3, _permute_w_in(w_in[l]))
        s5y = _s5_glu(_s5_mix(a, l, *s5w), s5_w_glu[l], s5_b_glu[l])
        ret_o = _retention(a, rope_cos, rope_sin, ret_decay_logit[l])
        ml_o = _mlstm(a, mlstm_igate_b[l], mlstm_fgate_b[l])
        rows = NX if last else NT
        xs = _out_proj(rows, s5y, ret_o, ml_o, a, ret_norm_w[l], mlstm_norm_w[l], xs, mod3,
                       w_out[l].astype(BF16))
        xs = _mlp(rows, xs, norm2_w[l], mod3, w_ff1, w_ff2, l, norm_f_w, last)
    return xs.reshape(BATCH, SEQ, D_MODEL)
```

```python
import functools
import math

import jax
import jax.numpy as jnp
from jax import lax
from jax.experimental import pallas as pl
from jax.experimental.pallas import tpu as pltpu

F32 = jnp.float32
BF16 = jnp.bfloat16

D_MODEL = 2048
BATCH = 4
SEQ = 2048
DEPTH = 2
CTX_LEN = 256
GRID_W = 64
HEAD_DIM = 128
S5_WIDTH = 512
S5_GROUP = 16
S5_GROUPS = 32
S5_STATE = 64
RET_WIDTH = 768
MLSTM_WIDTH = 768
HEADS = 6
IN_WIDTH = S5_WIDTH + 4 * RET_WIDTH + 4 * MLSTM_WIDTH + 4 * HEADS
D_FF = 4 * D_MODEL
CHUNK = 128
ROPE_BASE = 10000.0
EPS = 1e-6

NX = BATCH * SEQ
NH = BATCH * CTX_LEN
NT = NX + NH
MOD_ROWS = 8
CTX_GROUP = BATCH

COL_RET = 0
COL_ML = 4 * RET_WIDTH
COL_S5 = COL_ML + 4 * MLSTM_WIDTH
COL_GATE = COL_S5 + S5_WIDTH
IN_PAD = 7168

X_CHUNKS = SEQ // CHUNK
H_CHUNKS = CTX_LEN // CHUNK
MIX_STEPS = X_CHUNKS + H_CHUNKS

S5_T = 8
S5_LANES = 128
S5_LG = S5_LANES // S5_GROUP
S5_TILES = S5_WIDTH // S5_LANES
S5_K = S5_T * S5_LANES
S5_XB = SEQ // S5_T
S5_HB = CTX_LEN // S5_T
S5_NB = S5_XB + S5_HB
S5_ROWS = S5_NB * BATCH
S5_RC = 128

VMEM_LIMIT = 48 * 1024 * 1024
VMEM_LIMIT_BIG = 56 * 1024 * 1024


def _cparams(sem, limit=VMEM_LIMIT):
    return pltpu.CompilerParams(dimension_semantics=sem, vmem_limit_bytes=limit)


def _sigmoid(x):
    return 1.0 / (1.0 + jnp.exp(-x))


def _silu(x):
    return x * _sigmoid(x)


def _log_sigmoid(x):
    return jnp.minimum(x, 0.0) - jnp.log1p(jnp.exp(-jnp.abs(x)))


def _gelu_tanh(x):
    c = math.sqrt(2.0 / math.pi)
    return 0.5 * x * (1.0 + jnp.tanh(c * (x + 0.044715 * (x * x * x))))


def _row_group(i, tm):
    return jnp.minimum((i * tm) // SEQ, CTX_GROUP)


def _mod_kernel(c_ref, w_ref, b_ref, o_ref):
    c = c_ref[...]
    lhs = _silu(c).astype(BF16)
    o_ref[...] = jnp.dot(lhs, w_ref[...].astype(BF16), preferred_element_type=F32) + b_ref[...]


def _modulation(cvec, w_mod, b_mod):
    tn = 512
    n = 6 * D_MODEL
    return pl.pallas_call(
        _mod_kernel,
        out_shape=jax.ShapeDtypeStruct((DEPTH, MOD_ROWS, n), F32),
        grid=(DEPTH, n // tn),
        in_specs=[
            pl.BlockSpec((MOD_ROWS, D_MODEL), lambda l, j: (0, 0)),
            pl.BlockSpec((None, D_MODEL, tn), lambda l, j: (l, 0, j)),
            pl.BlockSpec((None, 1, tn), lambda l, j: (l, 0, j)),
        ],
        out_specs=pl.BlockSpec((None, MOD_ROWS, tn), lambda l, j: (l, 0, j)),
        compiler_params=_cparams(("arbitrary", "arbitrary")),
        name="modulation",
    )(cvec, w_mod, b_mod.reshape(DEPTH, 1, n))


def _rms_modulate(x, nw, shift, scale):
    y = x * lax.rsqrt(jnp.mean(x * x, axis=-1, keepdims=True) + EPS) * nw
    return y * (1.0 + scale) + shift


def _in_proj_kernel(x_ref, nw_ref, shift_ref, scale_ref, w_ref, o_ref, lhs_scr):
    @pl.when(pl.program_id(1) == 0)
    def _():
        lhs_scr[...] = _rms_modulate(x_ref[...], nw_ref[...], shift_ref[...], scale_ref[...]).astype(BF16)

    o_ref[...] = jnp.dot(lhs_scr[...], w_ref[...], preferred_element_type=F32)


def _in_proj(xs, norm_w, mod3, w_in_p):
    tm, tn = 1024, 1024
    return pl.pallas_call(
        _in_proj_kernel,
        out_shape=jax.ShapeDtypeStruct((NT, IN_PAD), F32),
        grid=(NT // tm, IN_PAD // tn),
        in_specs=[
            pl.BlockSpec((tm, D_MODEL), lambda i, j: (i, 0)),
            pl.BlockSpec((1, D_MODEL), lambda i, j: (0, 0)),
            pl.BlockSpec((None, 1, D_MODEL), lambda i, j: (_row_group(i, tm), 0, 0)),
            pl.BlockSpec((None, 1, D_MODEL), lambda i, j: (_row_group(i, tm), 0, 1)),
            pl.BlockSpec((D_MODEL, tn), lambda i, j: (0, j)),
        ],
        out_specs=pl.BlockSpec((tm, tn), lambda i, j: (i, j)),
        scratch_shapes=[pltpu.VMEM((tm, D_MODEL), BF16)],
        compiler_params=_cparams(("arbitrary", "arbitrary")),
        name="in_proj",
    )(xs, norm_w.reshape(1, D_MODEL), mod3, mod3, w_in_p)


def _s5_prep(lam_re, lam_im, log_step, b_re, b_im, c_re, c_im, d_skip):
    hp = lax.Precision.HIGHEST
    lam_re = jnp.minimum(lam_re.astype(F32), -1e-4)
    lam_im = lam_im.astype(F32)
    step = jnp.exp(log_step.astype(F32))[..., None]
    mag = jnp.exp(lam_re * step)
    ab_re, ab_im = mag * jnp.cos(lam_im * step), mag * jnp.sin(lam_im * step)
    den = lam_re * lam_re + lam_im * lam_im
    ir, ii = lam_re / den, -lam_im / den
    nr = (ab_re - 1.0) * ir - ab_im * ii
    ni = (ab_re - 1.0) * ii + ab_im * ir
    bre, bim = b_re.astype(F32), b_im.astype(F32)
    bb_re = nr[..., None] * bre - ni[..., None] * bim
    bb_im = nr[..., None] * bim + ni[..., None] * bre
    lpow = jnp.arange(S5_T + 1, dtype=F32)[:, None, None, None]
    pmag = jnp.exp(lpow * (lam_re * step))
    pr, pi = pmag * jnp.cos(lpow * (lam_im * step)), pmag * jnp.sin(lpow * (lam_im * step))
    cre, cim = c_re.astype(F32), c_im.astype(F32)
    ca_r = cre[None] * pr[:, :, :, None, :] - cim[None] * pi[:, :, :, None, :]
    ca_i = cre[None] * pi[:, :, :, None, :] + cim[None] * pr[:, :, :, None, :]
    kl = (jnp.einsum('ldgnp,dgpm->ldgnm', ca_r, bb_re, precision=hp)
          - jnp.einsum('ldgnp,dgpm->ldgnm', ca_i, bb_im, precision=hp))
    tt = jnp.arange(S5_T)
    lag = tt[None, :] - tt[:, None]
    kf = jnp.where((lag >= 0)[:, :, None, None, None], kl[jnp.clip(lag, 0, S5_T), 0], 0.0)
    kb = jnp.where((lag <= 0)[:, :, None, None, None], kl[jnp.clip(-lag, 0, S5_T), 1], 0.0)
    nt = S5_TILES
    row = jnp.arange(S5_K)
    col = jnp.arange(S5_K)
    lane = jnp.arange(S5_LANES)
    g_blk = (row // S5_GROUP) % S5_LG
    g_state = 2 * (row // (4 * S5_STATE)) + (row // S5_STATE) % 2
    to_blk = ((lane[:, None] // S5_GROUP == col[None, :] // S5_LANES)
              & (lane[:, None] % S5_GROUP == col[None, :] % S5_GROUP)).astype(BF16)
    to_state = ((lane[:, None] // S5_STATE == (col[None, :] // S5_LANES) % 2)
                & (lane[:, None] % S5_STATE == col[None, :] % S5_STATE)).astype(BF16)

    def expand(compact, spread, g_row, g_col):
        wide = jnp.einsum('trk,kc->trc', compact.reshape(nt, S5_K, S5_LANES).astype(BF16), spread,
                          preferred_element_type=BF16)
        return jnp.where(g_row[:, None] == g_col[None, :], wide, jnp.zeros((), BF16))

    mc = (kf + kb).reshape(S5_T, S5_T, nt, S5_LG, S5_GROUP, S5_GROUP).transpose(2, 0, 3, 5, 1, 4)
    m = expand(mc, to_blk, g_blk, g_blk)

    def carry_in(d, powers):
        er = pr[powers, d][..., None] * bb_re[d][None] - pi[powers, d][..., None] * bb_im[d][None]
        ei = pr[powers, d][..., None] * bb_im[d][None] + pi[powers, d][..., None] * bb_re[d][None]
        arrange = lambda t: t.reshape(S5_T, nt, S5_LG, S5_STATE, S5_GROUP).transpose(1, 0, 2, 4, 3)
        es = jnp.stack([arrange(er), arrange(ei)], axis=4)
        return expand(es, to_state, g_blk, g_state)

    def read_out(d, powers):
        arrange = lambda t: t.reshape(S5_T, nt, S5_LG // 2, 2, S5_GROUP, S5_STATE).transpose(1, 2, 3, 5, 0, 4)
        fs = jnp.stack([arrange(ca_r[powers, d]), arrange(-ca_i[powers, d])], axis=2)
        return expand(fs, to_blk, g_state, g_blk)

    ef, eb = carry_in(0, S5_T - 1 - tt), carry_in(1, tt)
    ff, fb = read_out(0, tt + 1), read_out(1, S5_T - tt)
    lanes = lambda t: t.reshape(nt, S5_K // 2)
    a_r = jnp.concatenate([lanes(pr[S5_T, 0]), lanes(pr[S5_T, 1])], axis=-1)
    a_i = jnp.concatenate([lanes(pi[S5_T, 0]), lanes(pi[S5_T, 1])], axis=-1)
    dsk = jnp.pad(d_skip.astype(F32).reshape(nt, S5_LANES), ((0, 0), (0, S5_K - S5_LANES)))
    aux = jnp.stack([a_r, a_i, dsk] + [jnp.zeros_like(a_r)] * 5, axis=1)
    return ef, eb, m, ff, fb, aux


def _s5_kernel(a_ref, ef_ref, eb_ref, m_ref, ff_ref, fb_ref, aux_ref, y_ref, u_scr, wf_scr, wb_scr):
    for b in range(BATCH):
        for t in range(S5_T):
            u_scr.at[t][pl.ds(S5_HB * BATCH + b, S5_XB, stride=BATCH), :] = (
                a_ref[pl.ds(b * SEQ + t, S5_XB, stride=S5_T), :])
            u_scr.at[t][pl.ds(b, S5_HB, stride=BATCH), :] = a_ref[pl.ds(NX + b * CTX_LEN + t, S5_HB, stride=S5_T), :]

    def get_tiles(scr, r0):
        return jnp.concatenate([scr.at[t][pl.ds(r0, S5_RC), :] for t in range(S5_T)], axis=1).astype(BF16)

    def put_tiles(scr, r0, val):
        for t in range(S5_T):
            scr.at[t][pl.ds(r0, S5_RC), :] = val[:, t * S5_LANES:(t + 1) * S5_LANES]

    def block_rows(rc):
        r0 = pl.multiple_of(rc * S5_RC, S5_RC)
        return r0, get_tiles(u_scr, r0)

    def carry_in(rc, _):
        r0, ub = block_rows(rc)
        put_tiles(wf_scr, r0, jnp.dot(ub, ef_ref[...], preferred_element_type=F32))
        put_tiles(wb_scr, r0, jnp.dot(ub, eb_ref[...], preferred_element_type=F32))
        return 0

    lax.fori_loop(0, S5_ROWS // S5_RC, carry_in, 0)

    aux = aux_ref[...]
    npair = S5_LG // 2
    shape = (BATCH, S5_LANES)
    coef = []
    for d in range(2):
        for j in range(npair):
            lo = d * (S5_K // 2) + j * S5_LANES
            coef.append((jnp.broadcast_to(aux[0:1, lo:lo + S5_LANES], shape),
                         jnp.broadcast_to(aux[1:2, lo:lo + S5_LANES], shape)))

    def scan(s, carry):
        rf = pl.multiple_of(s * BATCH, BATCH)
        gb = jnp.where(s < S5_HB, S5_HB - 1 - s, S5_NB + S5_HB - 1 - s)
        rb = pl.multiple_of(gb * BATCH, BATCH)
        out = []
        for d, (w_scr, r) in enumerate(((wf_scr, rf), (wb_scr, rb))):
            for j in range(npair):
                xr, xi = carry[d * npair + j]
                ar, ai = coef[d * npair + j]
                w_re, w_im = w_scr.at[2 * j], w_scr.at[2 * j + 1]
                wr = w_re[pl.ds(r, BATCH), :]
                wi = w_im[pl.ds(r, BATCH), :]
                w_re[pl.ds(r, BATCH), :] = xr
                w_im[pl.ds(r, BATCH), :] = xi
                out.append((ar * xr - ai * xi + wr, ar * xi + ai * xr + wi))
        return tuple(out)

    z = jnp.zeros(shape, F32)
    lax.fori_loop(0, S5_NB, scan, tuple((z, z) for _ in range(2 * npair)))

    def read_out(rc, _):
        r0, ub = block_rows(rc)
        y = jnp.dot(ub, m_ref[...], preferred_element_type=F32)
        y = y + jnp.dot(get_tiles(wf_scr, r0), ff_ref[...], preferred_element_type=F32)
        y = y + jnp.dot(get_tiles(wb_scr, r0), fb_ref[...], preferred_element_type=F32)
        put_tiles(u_scr, r0, y)
        return 0

    lax.fori_loop(0, S5_ROWS // S5_RC, read_out, 0)

    for b in range(BATCH):
        for t in range(S5_T):
            y_ref[pl.ds(b * SEQ + t, S5_XB, stride=S5_T), :] = (
                u_scr.at[t][pl.ds(S5_HB * BATCH + b, S5_XB, stride=BATCH), :])
            y_ref[pl.ds(NX + b * CTX_LEN + t, S5_HB, stride=S5_T), :] = u_scr.at[t][pl.ds(b, S5_HB, stride=BATCH), :]
    y_ref[...] = y_ref[...] + a_ref[...] * aux[2:3, 0:S5_LANES]


def _s5_mix(a, layer, ef, eb, m, ff, fb, aux):
    wspec = lambda: pl.BlockSpec((None, None, S5_K, S5_K), lambda g: (layer, g, 0, 0))
    return pl.pallas_call(
        _s5_kernel,
        out_shape=jax.ShapeDtypeStruct((NT, S5_WIDTH), F32),
        grid=(S5_TILES,),
        in_specs=[
            pl.BlockSpec((NT, S5_LANES), lambda g: (0, COL_S5 // S5_LANES + g)),
            wspec(), wspec(), wspec(), wspec(), wspec(),
            pl.BlockSpec((None, None, 8, S5_K), lambda g: (layer, g, 0, 0)),
        ],
        out_specs=pl.BlockSpec((NT, S5_LANES), lambda g: (0, g)),
        scratch_shapes=[pltpu.VMEM((S5_T, S5_ROWS, S5_LANES), F32)] * 3,
        compiler_params=_cparams(("arbitrary",), VMEM_LIMIT_BIG),
        name="s5_scan",
    )(a, ef, eb, m, ff, fb, aux)


def _glu_kernel(y_ref, w_ref, b_ref, o_ref):
    g = _gelu_tanh(y_ref[...]).astype(BF16)
    z = jnp.dot(g, w_ref[...], preferred_element_type=F32) + b_ref[...]
    o_ref[...] = z[:, :S5_WIDTH] * _sigmoid(z[:, S5_WIDTH:])


def _s5_glu(y, w_glu, b_glu):
    tm = 1024
    return pl.pallas_call(
        _glu_kernel,
        out_shape=jax.ShapeDtypeStruct((NT, S5_WIDTH), F32),
        grid=(NT // tm,),
        in_specs=[
            pl.BlockSpec((tm, S5_WIDTH), lambda i: (i, 0)),
            pl.BlockSpec((S5_WIDTH, 2 * S5_WIDTH), lambda i: (0, 0)),
            pl.BlockSpec((1, 2 * S5_WIDTH), lambda i: (0, 0)),
        ],
        out_specs=pl.BlockSpec((tm, S5_WIDTH), lambda i: (i, 0)),
        compiler_params=_cparams(("arbitrary",)),
        name="s5_glu",
    )(y, w_glu.astype(BF16), b_glu.reshape(1, 2 * S5_WIDTH))


def _chunk_index(d, s):
    is_ctx = s < H_CHUNKS
    fwd = jnp.where(is_ctx, s, s - H_CHUNKS)
    bwd = jnp.where(is_ctx, H_CHUNKS - 1 - s, MIX_STEPS - 1 - s)
    return is_ctx, jnp.where(d == 0, fwd, bwd)


def _token_block(b, d, s):
    is_ctx, c = _chunk_index(d, s)
    return jnp.where(is_ctx, NX // CHUNK + b * H_CHUNKS + c, b * X_CHUNKS + c)


def _rope_block(b, d, s):
    is_ctx, c = _chunk_index(d, s)
    return jnp.where(is_ctx, X_CHUNKS + c, c)


def _dot_nt(a, b):
    return lax.dot_general(a, b, (((1,), (1,)), ((), ())), preferred_element_type=F32)


def _dot_tn(a, b):
    return lax.dot_general(a, b, (((0,), (0,)), ((), ())), preferred_element_type=F32)


def _dot(a, b):
    return jnp.dot(a, b, preferred_element_type=F32)


def _head(ref, h):
    return ref[:, h * HEAD_DIM:(h + 1) * HEAD_DIM]


def _chunk_pos(rev):
    i = lax.broadcasted_iota(jnp.int32, (CHUNK, CHUNK), 0)
    j = lax.broadcasted_iota(jnp.int32, (CHUNK, CHUNK), 1)
    pi = jnp.where(rev, CHUNK - 1 - i, i)
    pj = jnp.where(rev, CHUNK - 1 - j, j)
    return pi, pj


def _ret_kernel(dl_ref, qf_ref, kf_ref, vf_ref, cosf_ref, sinf_ref, qb_ref, kb_ref, vb_ref, cosb_ref, sinb_ref,
                of_ref, ob_ref, r_scr, dec_scr, qd_scr, kd_scr, cd_scr):
    s = pl.program_id(1)

    @pl.when(s == 0)
    def _():
        for d in range(2):
            pi, pj = _chunk_pos(d == 1)
            rel = (pi - pj).astype(F32)
            pif = pi.astype(F32)
            for h in range(HEADS):
                x = d * HEADS + h
                lg = _log_sigmoid(dl_ref[d, h])[0:1, :]
                dec_scr[x] = jnp.where(rel >= 0, jnp.exp(lg * jnp.maximum(rel, 0.0)), 0.0)
                qd_scr[x] = jnp.exp(lg * (pif + 1.0))
                kd_scr[x] = jnp.exp(lg * (CHUNK - 1.0 - pif))
                cd_scr[x] = jnp.exp(jnp.broadcast_to(lg, (8, HEAD_DIM)) * CHUNK)
        r_scr[...] = jnp.zeros_like(r_scr)

    scale = HEAD_DIM ** -0.5
    dirs = ((qf_ref, kf_ref, vf_ref, cosf_ref, sinf_ref, of_ref), (qb_ref, kb_ref, vb_ref, cosb_ref, sinb_ref, ob_ref))
    ch = [(d, h) for d in range(2) for h in range(HEADS)]
    xs = range(len(ch))
    cos = [dirs[d][3][...] for d in range(2)]
    sin = [dirs[d][4][...] for d in range(2)]
    q = [_head(dirs[d][0], h) for d, h in ch]
    k = [_head(dirs[d][1], h) for d, h in ch]
    v = [_head(dirs[d][2], h).astype(BF16) for d, h in ch]
    q = [q[x] * cos[ch[x][0]] + pltpu.roll(q[x], HEAD_DIM // 2, 1) * sin[ch[x][0]] for x in xs]
    k = [(k[x] * cos[ch[x][0]] + pltpu.roll(k[x], HEAD_DIM // 2, 1) * sin[ch[x][0]]) * scale for x in xs]
    r = [r_scr[x] for x in xs]
    sm = [_dot_nt(q[x].astype(BF16), k[x].astype(BF16)) * dec_scr[x] for x in xs]
    inter = [_dot((q[x] * qd_scr[x]).astype(BF16), r[x].astype(BF16)) for x in xs]
    kv = [_dot_tn((k[x] * kd_scr[x]).astype(BF16), v[x]) for x in xs]
    for x in xs:
        d, h = ch[x]
        dirs[d][5][:, h * HEAD_DIM:(h + 1) * HEAD_DIM] = _dot(sm[x].astype(BF16), v[x]) + inter[x]
        r_scr[x] = cd_scr[x][0:1, :] * r[x] + kv[x]


def _retention(a, rope_cos, rope_sin, decay_logit):
    dl = jnp.broadcast_to(decay_logit.astype(F32)[:, :, None, None], (2, HEADS, 8, HEAD_DIM))
    col0 = COL_RET // RET_WIDTH
    specs = [pl.BlockSpec((2, HEADS, 8, HEAD_DIM), lambda b, s: (0, 0, 0, 0))]
    for d in range(2):
        for col in (col0, col0 + 1, col0 + 2):
            specs.append(pl.BlockSpec((CHUNK, RET_WIDTH), lambda b, s, d=d, col=col: (_token_block(b, d, s), col)))
        specs += [pl.BlockSpec((CHUNK, HEAD_DIM), lambda b, s, d=d: (_rope_block(b, d, s), 0))] * 2
    hh = pltpu.VMEM((2 * HEADS, CHUNK, HEAD_DIM), F32)
    out = jax.ShapeDtypeStruct((NT, RET_WIDTH), F32)
    return pl.pallas_call(
        _ret_kernel,
        out_shape=(out, out),
        grid=(BATCH, MIX_STEPS),
        in_specs=specs,
        out_specs=[pl.BlockSpec((CHUNK, RET_WIDTH), lambda b, s, d=d: (_token_block(b, d, s), 0)) for d in range(2)],
        scratch_shapes=[hh, hh, hh, hh, pltpu.VMEM((2 * HEADS, 8, HEAD_DIM), F32)],
        compiler_params=_cparams(("arbitrary", "arbitrary")),
        name="retention",
    )(dl, a, a, a, rope_cos, rope_sin, a, a, a, rope_cos, rope_sin)


def _split3(x):
    h = x.astype(BF16)
    r = x - h.astype(F32)
    m = r.astype(BF16)
    return h, m, (r - m.astype(F32)).astype(BF16)


def _mlstm_kernel(qf_ref, kf_ref, vf_ref, gf_ref, qb_ref, kb_ref, vb_ref, gb_ref, bias_ref, of_ref, ob_ref,
                  c_scr, n_scr, m_scr):
    s = pl.program_id(1)

    @pl.when(s == 0)
    def _():
        c_scr[...] = jnp.zeros_like(c_scr)
        n_scr[...] = jnp.zeros_like(n_scr)
        m_scr[...] = jnp.zeros_like(m_scr)

    scale = HEAD_DIM ** -0.5
    neg_inf = -jnp.inf
    dirs = ((qf_ref, kf_ref, vf_ref, gf_ref, of_ref), (qb_ref, kb_ref, vb_ref, gb_ref, ob_ref))
    causal, pre, pre_t, b_cols, b_rows = [], [], [], [], []
    for d in range(2):
        pi, pj = _chunk_pos(d == 1)
        cz = pj <= pi
        tri = jnp.where(cz, 1.0, 0.0).astype(BF16)
        p = dirs[d][3][...] + bias_ref[...]
        pt = p.T[0:32, :]
        causal.append(cz)
        pre.append(p)
        pre_t.append(pt)
        b_cols.append(sum(_dot(tri, part) for part in _split3(_log_sigmoid(p))))
        b_rows.append(sum(_dot_nt(part, tri) for part in _split3(_log_sigmoid(pt))))
    ch = [(d, h) for d in range(2) for h in range(HEADS)]
    xs = range(len(ch))
    ci = [2 * d * HEADS + h for d, h in ch]
    cf = [(2 * d + 1) * HEADS + h for d, h in ch]
    last = [0 if d == 1 else CHUNK - 1 for d, h in ch]
    i_col = [pre[ch[x][0]][:, ci[x]:ci[x] + 1] for x in xs]
    b_col = [b_cols[ch[x][0]][:, cf[x]:cf[x] + 1] for x in xs]
    i_row = [pre_t[ch[x][0]][ci[x]:ci[x] + 1, :] for x in xs]
    b_row = [b_rows[ch[x][0]][cf[x]:cf[x] + 1, :] for x in xs]
    b_end = [b_cols[ch[x][0]][last[x]:last[x] + 1, cf[x]:cf[x] + 1] for x in xs]
    m_prev = [m_scr[x][0:1, 0:1] for x in xs]
    q = [_head(dirs[d][0], h) for d, h in ch]
    k = [_head(dirs[d][1], h) * scale for d, h in ch]
    v = [_head(dirs[d][2], h).astype(BF16) for d, h in ch]
    qb = [q[x].astype(BF16) for x in xs]
    c_mem = [c_scr[x] for x in xs]
    n_mem = [n_scr[x][0:1, :] for x in xs]
    log_w = [jnp.where(causal[ch[x][0]], b_col[x] - b_row[x] + i_row[x], neg_inf) for x in xs]
    log_a = [b_col[x] + m_prev[x] for x in xs]
    m_t = [jnp.maximum(log_a[x], jnp.max(log_w[x], axis=-1, keepdims=True)) for x in xs]
    w = [jnp.exp(log_w[x] - m_t[x]) for x in xs]
    a = [jnp.exp(log_a[x] - m_t[x]) for x in xs]
    sm = [_dot_nt(qb[x], k[x].astype(BF16)) * w[x] for x in xs]
    inter = [_dot(qb[x], c_mem[x].astype(BF16)) for x in xs]
    num = [_dot(sm[x].astype(BF16), v[x]) + a[x] * inter[x] for x in xs]
    den = [jnp.sum(sm[x] + a[x] * (q[x] * n_mem[x]), axis=-1, keepdims=True) for x in xs]
    for x in xs:
        d, h = ch[x]
        dirs[d][4][:, h * HEAD_DIM:(h + 1) * HEAD_DIM] = num[x] / jnp.maximum(jnp.abs(den[x]), jnp.exp(-m_t[x]))
    log_w_end = [b_end[x] - b_col[x] + i_col[x] for x in xs]
    m_new = [jnp.maximum(b_end[x] + m_prev[x], jnp.max(log_w_end[x], axis=0, keepdims=True)) for x in xs]
    a_end = [jnp.exp(b_end[x] + m_prev[x] - m_new[x]) for x in xs]
    kw = [k[x] * jnp.exp(log_w_end[x] - m_new[x]) for x in xs]
    kv = [_dot_tn(kw[x].astype(BF16), v[x]) for x in xs]
    for x in xs:
        c_scr[x] = a_end[x] * c_mem[x] + kv[x]
        n_scr[x] = jnp.broadcast_to(a_end[x] * n_mem[x] + jnp.sum(kw[x], axis=0, keepdims=True), (8, HEAD_DIM))
        m_scr[x] = jnp.broadcast_to(m_new[x], (8, HEAD_DIM))


def _mlstm(a, igate_b, fgate_b):
    bias = jnp.stack([igate_b.astype(F32), fgate_b.astype(F32)], axis=1).reshape(1, 4 * HEADS)
    bias = jnp.pad(bias, ((0, 0), (0, HEAD_DIM - 4 * HEADS)))
    col0 = COL_ML // MLSTM_WIDTH
    specs = []
    for d in range(2):
        for col in (col0, col0 + 1, col0 + 2):
            specs.append(pl.BlockSpec((CHUNK, MLSTM_WIDTH), lambda b, s, d=d, col=col: (_token_block(b, d, s), col)))
        specs.append(pl.BlockSpec((CHUNK, HEAD_DIM), lambda b, s, d=d: (_token_block(b, d, s), COL_GATE // HEAD_DIM)))
    specs.append(pl.BlockSpec((1, HEAD_DIM), lambda b, s: (0, 0)))
    small = pltpu.VMEM((2 * HEADS, 8, HEAD_DIM), F32)
    out = jax.ShapeDtypeStruct((NT, MLSTM_WIDTH), F32)
    return pl.pallas_call(
        _mlstm_kernel,
        out_shape=(out, out),
        grid=(BATCH, MIX_STEPS),
        in_specs=specs,
        out_specs=[pl.BlockSpec((CHUNK, MLSTM_WIDTH), lambda b, s, d=d: (_token_block(b, d, s), 0))
                   for d in range(2)],
        scratch_shapes=[pltpu.VMEM((2 * HEADS, CHUNK, HEAD_DIM), F32), small, small],
        compiler_params=_cparams(("arbitrary", "arbitrary")),
        name="mlstm",
    )(a, a, a, a, a, a, a, a, bias)


def _head_norm(o, center):
    if center:
        o = o - jnp.mean(o, axis=-1, keepdims=True)
    return o * lax.rsqrt(jnp.mean(o * o, axis=-1, keepdims=True) + EPS)


def _out_proj_kernel(s5_ref, rf_ref, rb_ref, rg_ref, rw_ref, mf_ref, mb_ref, mg_ref, mw_ref, x_ref, gate_ref,
                     w_ref, o_ref):
    pair = 2 * HEAD_DIM
    acc = jnp.dot(s5_ref[:, 0:pair].astype(BF16), w_ref[0:pair, :], preferred_element_type=F32)
    acc += jnp.dot(s5_ref[:, pair:2 * pair].astype(BF16), w_ref[pair:2 * pair, :], preferred_element_type=F32)
    for hp in range(HEADS // 2):
        rs, ms = [], []
        for h in (2 * hp, 2 * hp + 1):
            sl = slice(h * HEAD_DIM, (h + 1) * HEAD_DIM)
            r = _head_norm(rf_ref[:, sl] + rb_ref[:, sl], True) * rw_ref[:, sl] * _silu(rg_ref[:, sl])
            m = _head_norm(mf_ref[:, sl] + mb_ref[:, sl], False) * mw_ref[:, sl] * _sigmoid(mg_ref[:, sl])
            rs.append(r.astype(BF16))
            ms.append(m.astype(BF16))
        r0 = S5_WIDTH + hp * pair
        m0 = S5_WIDTH + RET_WIDTH + hp * pair
        acc += jnp.dot(jnp.concatenate(rs, axis=1), w_ref[r0:r0 + pair, :], preferred_element_type=F32)
        acc += jnp.dot(jnp.concatenate(ms, axis=1), w_ref[m0:m0 + pair, :], preferred_element_type=F32)
    o_ref[...] = x_ref[...] + gate_ref[...] * acc


def _out_proj(rows, s5y, ret_o, ml_o, a, ret_norm_w, mlstm_norm_w, xs, mod3, w_out, layer):
    tm = 256
    return pl.pallas_call(
        _out_proj_kernel,
        out_shape=jax.ShapeDtypeStruct((rows, D_MODEL), F32),
        grid=(rows // tm,),
        in_specs=[
            pl.BlockSpec((tm, S5_WIDTH), lambda i: (i, 0)),
            pl.BlockSpec((tm, RET_WIDTH), lambda i: (i, 0)),
            pl.BlockSpec((tm, RET_WIDTH), lambda i: (i, 0)),
            pl.BlockSpec((tm, RET_WIDTH), lambda i: (i, COL_RET // RET_WIDTH + 3)),
            pl.BlockSpec((1, RET_WIDTH), lambda i: (0, 0)),
            pl.BlockSpec((tm, MLSTM_WIDTH), lambda i: (i, 0)),
            pl.BlockSpec((tm, MLSTM_WIDTH), lambda i: (i, 0)),
            pl.BlockSpec((tm, MLSTM_WIDTH), lambda i: (i, COL_ML // MLSTM_WIDTH + 3)),
            pl.BlockSpec((1, MLSTM_WIDTH), lambda i: (0, 0)),
            pl.BlockSpec((tm, D_MODEL), lambda i: (i, 0)),
            pl.BlockSpec((None, 1, D_MODEL), lambda i: (_row_group(i, tm), 0, 2)),
            pl.BlockSpec((None, D_MODEL, D_MODEL), lambda i: (layer, 0, 0), pipeline_mode=pl.Buffered(1)),
        ],
        out_specs=pl.BlockSpec((tm, D_MODEL), lambda i: (i, 0)),
        compiler_params=_cparams(("arbitrary",)),
        name="out_proj",
    )(s5y, ret_o[0], ret_o[1], a, ret_norm_w.reshape(1, RET_WIDTH), ml_o[0], ml_o[1], a,
      mlstm_norm_w.reshape(1, MLSTM_WIDTH),
      xs, mod3, w_out)


def _mlp_kernel(x_ref, nw_ref, shift_ref, scale_ref, gate_ref, w1_ref, w2_ref, nf_ref, o_ref, lhs_scr,
                *, final_norm):
    f = pl.program_id(1)

    @pl.when(f == 0)
    def _():
        lhs_scr[...] = _rms_modulate(x_ref[...], nw_ref[...], shift_ref[...], scale_ref[...]).astype(BF16)
        o_ref[...] = jnp.zeros_like(o_ref)

    hid = jnp.dot(lhs_scr[...], w1_ref[...].astype(BF16), preferred_element_type=F32)
    hid = jnp.square(jnp.maximum(hid, 0.0)).astype(BF16)
    o_ref[...] += jnp.dot(hid, w2_ref[...].astype(BF16), preferred_element_type=F32)

    @pl.when(f == pl.num_programs(1) - 1)
    def _():
        y = x_ref[...] + gate_ref[...] * o_ref[...]
        if final_norm:
            y = y * lax.rsqrt(jnp.mean(y * y, axis=-1, keepdims=True) + EPS) * nf_ref[...]
        o_ref[...] = y


def _mlp(rows, xs, norm_w, mod3, w1, w2, layer, norm_f_w, final_norm):
    tm, tf = 1024, 512
    modspec = lambda k: pl.BlockSpec((None, 1, D_MODEL), lambda i, f: (_row_group(i, tm), 0, k))
    once = pl.Buffered(1)
    return pl.pallas_call(
        functools.partial(_mlp_kernel, final_norm=final_norm),
        out_shape=jax.ShapeDtypeStruct((rows, D_MODEL), F32),
        grid=(rows // tm, D_FF // tf),
        in_specs=[
            pl.BlockSpec((tm, D_MODEL), lambda i, f: (i, 0)),
            pl.BlockSpec((1, D_MODEL), lambda i, f: (0, 0)),
            modspec(3), modspec(4), modspec(5),
            pl.BlockSpec((None, D_MODEL, tf), lambda i, f: (layer, 0, f)),
            pl.BlockSpec((None, tf, D_MODEL), lambda i, f: (layer, f, 0)),
            pl.BlockSpec((1, D_MODEL), lambda i, f: (0, 0)),
        ],
        out_specs=pl.BlockSpec((tm, D_MODEL), lambda i, f: (i, 0), pipeline_mode=once),
        scratch_shapes=[pltpu.VMEM((tm, D_MODEL), BF16)],
        compiler_params=_cparams(("arbitrary", "arbitrary"), VMEM_LIMIT_BIG),
        name="mlp",
    )(xs, norm_w.reshape(1, D_MODEL), mod3, mod3, mod3, w1, w2, norm_f_w.reshape(1, D_MODEL))


def _rope_tables():
    quarter = HEAD_DIM // 4
    rows = jnp.repeat(jnp.arange(SEQ // GRID_W, dtype=F32), GRID_W)
    cols = jnp.tile(jnp.arange(GRID_W, dtype=F32), SEQ // GRID_W)
    inv = ROPE_BASE ** (-jnp.arange(quarter, dtype=F32) / quarter)
    ang = jnp.concatenate([rows[:, None] * inv, cols[:, None] * inv], axis=-1)
    cos, sin = jnp.cos(ang), jnp.sin(ang)
    cos2 = jnp.concatenate([cos, cos], axis=-1)
    sin2 = jnp.concatenate([-sin, sin], axis=-1)
    cos2 = jnp.concatenate([cos2, jnp.ones((CTX_LEN, HEAD_DIM), F32)], axis=0)
    sin2 = jnp.concatenate([sin2, jnp.zeros((CTX_LEN, HEAD_DIM), F32)], axis=0)
    return cos2, sin2


def _permute_w_in(w):
    s5 = w[:, :S5_WIDTH]
    ret = w[:, S5_WIDTH:S5_WIDTH + 4 * RET_WIDTH]
    ml = w[:, S5_WIDTH + 4 * RET_WIDTH:S5_WIDTH + 4 * RET_WIDTH + 4 * MLSTM_WIDTH]
    gates = w[:, S5_WIDTH + 4 * RET_WIDTH + 4 * MLSTM_WIDTH:]
    pad = jnp.zeros((D_MODEL, IN_PAD - IN_WIDTH), w.dtype)
    return jnp.concatenate([ret, ml, s5, gates, pad], axis=1).astype(BF16)


def kernel(x, c, ctx, c_ctx, w_mod, b_mod, norm1_w, norm2_w, w_in, w_out, s5_lam_re, s5_lam_im, s5_log_step, s5_b_re, s5_b_im, s5_c_re, s5_c_im, s5_d, s5_w_glu, s5_b_glu, ret_decay_logit, ret_norm_w, mlstm_igate_b, mlstm_fgate_b, mlstm_norm_w, w_ff1, w_ff2, norm_f_w):
    rope_cos, rope_sin = _rope_tables()
    cvec = jnp.concatenate([c, c_ctx[None, :], jnp.zeros((MOD_ROWS - BATCH - 1, D_MODEL), F32)], axis=0)
    mod = _modulation(cvec, w_mod, b_mod)
    xs = jnp.concatenate([x.reshape(NX, D_MODEL), ctx.reshape(NH, D_MODEL)], axis=0)
    s5w = jax.vmap(_s5_prep)(s5_lam_re, s5_lam_im, s5_log_step, s5_b_re, s5_b_im, s5_c_re, s5_c_im, s5_d)
    w_out_b = w_out.astype(BF16)
    for l in range(DEPTH):
        last = l == DEPTH - 1
        mod3 = mod[l].reshape(MOD_ROWS, 1, 6 * D_MODEL)
        a = _in_proj(xs, norm1_w[l], mod3, _permute_w_in(w_in[l]))
        s5y = _s5_glu(_s5_mix(a, l, *s5w), s5_w_glu[l], s5_b_glu[l])
        ret_o = _retention(a, rope_cos, rope_sin, ret_decay_logit[l])
        ml_o = _mlstm(a, mlstm_igate_b[l], mlstm_fgate_b[l])
        rows = NX if last else NT
        xs = _out_proj(rows, s5y, ret_o, ml_o, a, ret_norm_w[l], mlstm_norm_w[l], xs, mod3, w_out_b, l)
        xs = _mlp(rows, xs, norm2_w[l], mod3, w_ff1, w_ff2, l, norm_f_w, last)
    return xs.reshape(BATCH, SEQ, D_MODEL)
```

```python
import functools
import math

import jax
import jax.numpy as jnp
from jax import lax
from jax.experimental import pallas as pl
from jax.experimental.pallas import tpu as pltpu

F32 = jnp.float32
BF16 = jnp.bfloat16

D_MODEL = 2048
BATCH = 4
SEQ = 2048
DEPTH = 2
CTX_LEN = 256
GRID_W = 64
HEAD_DIM = 128
S5_WIDTH = 512
S5_GROUP = 16
S5_GROUPS = 32
S5_STATE = 64
RET_WIDTH = 768
MLSTM_WIDTH = 768
HEADS = 6
IN_WIDTH = S5_WIDTH + 4 * RET_WIDTH + 4 * MLSTM_WIDTH + 4 * HEADS
D_FF = 4 * D_MODEL
CHUNK = 128
ROPE_BASE = 10000.0
EPS = 1e-6

NX = BATCH * SEQ
NH = BATCH * CTX_LEN
NT = NX + NH
MOD_ROWS = 8
CTX_GROUP = BATCH

COL_RET = 0
COL_ML = 4 * RET_WIDTH
COL_S5 = COL_ML + 4 * MLSTM_WIDTH
COL_GATE = COL_S5 + S5_WIDTH
IN_PAD = 7168

X_CHUNKS = SEQ // CHUNK
H_CHUNKS = CTX_LEN // CHUNK
MIX_STEPS = X_CHUNKS + H_CHUNKS

S5_T = 8
S5_LANES = 128
S5_LG = S5_LANES // S5_GROUP
S5_TILES = S5_WIDTH // S5_LANES
S5_K = S5_T * S5_LANES
S5_XB = SEQ // S5_T
S5_HB = CTX_LEN // S5_T
S5_NB = S5_XB + S5_HB
S5_ROWS = S5_NB * BATCH
S5_RC = 128

VMEM_LIMIT = 48 * 1024 * 1024
VMEM_LIMIT_BIG = 56 * 1024 * 1024


def _cparams(sem, limit=VMEM_LIMIT):
    return pltpu.CompilerParams(dimension_semantics=sem, vmem_limit_bytes=limit)


def _sigmoid(x):
    return 1.0 / (1.0 + jnp.exp(-x))


def _silu(x):
    return x * _sigmoid(x)


def _log_sigmoid(x):
    return jnp.minimum(x, 0.0) - jnp.log1p(jnp.exp(-jnp.abs(x)))


def _gelu_tanh(x):
    c = math.sqrt(2.0 / math.pi)
    return 0.5 * x * (1.0 + jnp.tanh(c * (x + 0.044715 * (x * x * x))))


def _row_group(i, tm):
    return jnp.minimum((i * tm) // SEQ, CTX_GROUP)


def _mod_kernel(c_ref, w_ref, b_ref, o_ref):
    c = c_ref[...]
    lhs = _silu(c).astype(BF16)
    o_ref[...] = jnp.dot(lhs, w_ref[...].astype(BF16), preferred_element_type=F32) + b_ref[...]


def _modulation(cvec, w_mod, b_mod):
    tn = 512
    n = 6 * D_MODEL
    return pl.pallas_call(
        _mod_kernel,
        out_shape=jax.ShapeDtypeStruct((DEPTH, MOD_ROWS, n), F32),
        grid=(DEPTH, n // tn),
        in_specs=[
            pl.BlockSpec((MOD_ROWS, D_MODEL), lambda l, j: (0, 0)),
            pl.BlockSpec((None, D_MODEL, tn), lambda l, j: (l, 0, j)),
            pl.BlockSpec((None, 1, tn), lambda l, j: (l, 0, j)),
        ],
        out_specs=pl.BlockSpec((None, MOD_ROWS, tn), lambda l, j: (l, 0, j)),
        compiler_params=_cparams(("arbitrary", "arbitrary")),
        name="modulation",
    )(cvec, w_mod, b_mod.reshape(DEPTH, 1, n))


def _rms_modulate(x, nw, shift, scale):
    y = x * lax.rsqrt(jnp.mean(x * x, axis=-1, keepdims=True) + EPS) * nw
    return y * (1.0 + scale) + shift


def _in_proj_kernel(x_ref, nw_ref, shift_ref, scale_ref, w_ref, o_ref, lhs_scr):
    @pl.when(pl.program_id(1) == 0)
    def _():
        lhs_scr[...] = _rms_modulate(x_ref[...], nw_ref[...], shift_ref[...], scale_ref[...]).astype(BF16)

    o_ref[...] = jnp.dot(lhs_scr[...], w_ref[...], preferred_element_type=F32)


def _in_proj(xs, norm_w, mod3, w_in_p):
    tm, tn = 1024, 1024
    return pl.pallas_call(
        _in_proj_kernel,
        out_shape=jax.ShapeDtypeStruct((NT, IN_PAD), F32),
        grid=(NT // tm, IN_PAD // tn),
        in_specs=[
            pl.BlockSpec((tm, D_MODEL), lambda i, j: (i, 0)),
            pl.BlockSpec((1, D_MODEL), lambda i, j: (0, 0)),
            pl.BlockSpec((None, 1, D_MODEL), lambda i, j: (_row_group(i, tm), 0, 0)),
            pl.BlockSpec((None, 1, D_MODEL), lambda i, j: (_row_group(i, tm), 0, 1)),
            pl.BlockSpec((D_MODEL, tn), lambda i, j: (0, j)),
        ],
        out_specs=pl.BlockSpec((tm, tn), lambda i, j: (i, j)),
        scratch_shapes=[pltpu.VMEM((tm, D_MODEL), BF16)],
        compiler_params=_cparams(("arbitrary", "arbitrary")),
        name="in_proj",
    )(xs, norm_w.reshape(1, D_MODEL), mod3, mod3, w_in_p)


def _s5_prep(lam_re, lam_im, log_step, b_re, b_im, c_re, c_im, d_skip):
    hp = lax.Precision.HIGHEST
    lam_re = jnp.minimum(lam_re.astype(F32), -1e-4)
    lam_im = lam_im.astype(F32)
    step = jnp.exp(log_step.astype(F32))[..., None]
    mag = jnp.exp(lam_re * step)
    ab_re, ab_im = mag * jnp.cos(lam_im * step), mag * jnp.sin(lam_im * step)
    den = lam_re * lam_re + lam_im * lam_im
    ir, ii = lam_re / den, -lam_im / den
    nr = (ab_re - 1.0) * ir - ab_im * ii
    ni = (ab_re - 1.0) * ii + ab_im * ir
    bre, bim = b_re.astype(F32), b_im.astype(F32)
    bb_re = nr[..., None] * bre - ni[..., None] * bim
    bb_im = nr[..., None] * bim + ni[..., None] * bre
    lpow = jnp.arange(S5_T + 1, dtype=F32)[:, None, None, None]
    pmag = jnp.exp(lpow * (lam_re * step))
    pr, pi = pmag * jnp.cos(lpow * (lam_im * step)), pmag * jnp.sin(lpow * (lam_im * step))
    cre, cim = c_re.astype(F32), c_im.astype(F32)
    ca_r = cre[None] * pr[:, :, :, None, :] - cim[None] * pi[:, :, :, None, :]
    ca_i = cre[None] * pi[:, :, :, None, :] + cim[None] * pr[:, :, :, None, :]
    kl = (jnp.einsum('ldgnp,dgpm->ldgnm', ca_r, bb_re, precision=hp)
          - jnp.einsum('ldgnp,dgpm->ldgnm', ca_i, bb_im, precision=hp))
    tt = jnp.arange(S5_T)
    lag = tt[None, :] - tt[:, None]
    kf = jnp.where((lag >= 0)[:, :, None, None, None], kl[jnp.clip(lag, 0, S5_T), 0], 0.0)
    kb = jnp.where((lag <= 0)[:, :, None, None, None], kl[jnp.clip(-lag, 0, S5_T), 1], 0.0)
    nt = S5_TILES
    row = jnp.arange(S5_K)
    col = jnp.arange(S5_K)
    lane = jnp.arange(S5_LANES)
    g_blk = (row // S5_GROUP) % S5_LG
    g_state = 2 * (row // (4 * S5_STATE)) + (row // S5_STATE) % 2
    to_blk = ((lane[:, None] // S5_GROUP == col[None, :] // S5_LANES)
              & (lane[:, None] % S5_GROUP == col[None, :] % S5_GROUP)).astype(BF16)
    to_state = ((lane[:, None] // S5_STATE == (col[None, :] // S5_LANES) % 2)
                & (lane[:, None] % S5_STATE == col[None, :] % S5_STATE)).astype(BF16)

    def expand(compact, spread, g_row, g_col):
        wide = jnp.einsum('trk,kc->trc', compact.reshape(nt, S5_K, S5_LANES).astype(BF16), spread,
                          preferred_element_type=BF16)
        return jnp.where(g_row[:, None] == g_col[None, :], wide, jnp.zeros((), BF16))

    mc = (kf + kb).reshape(S5_T, S5_T, nt, S5_LG, S5_GROUP, S5_GROUP).transpose(2, 0, 3, 5, 1, 4)
    m = expand(mc, to_blk, g_blk, g_blk)

    def carry_in(d, powers):
        er = pr[powers, d][..., None] * bb_re[d][None] - pi[powers, d][..., None] * bb_im[d][None]
        ei = pr[powers, d][..., None] * bb_im[d][None] + pi[powers, d][..., None] * bb_re[d][None]
        arrange = lambda t: t.reshape(S5_T, nt, S5_LG, S5_STATE, S5_GROUP).transpose(1, 0, 2, 4, 3)
        es = jnp.stack([arrange(er), arrange(ei)], axis=4)
        return expand(es, to_state, g_blk, g_state)

    def read_out(d, powers):
        arrange = lambda t: t.reshape(S5_T, nt, S5_LG // 2, 2, S5_GROUP, S5_STATE).transpose(1, 2, 3, 5, 0, 4)
        fs = jnp.stack([arrange(ca_r[powers, d]), arrange(-ca_i[powers, d])], axis=2)
        return expand(fs, to_blk, g_state, g_blk)

    ef, eb = carry_in(0, S5_T - 1 - tt), carry_in(1, tt)
    ff, fb = read_out(0, tt + 1), read_out(1, S5_T - tt)
    lanes = lambda t: t.reshape(nt, S5_K // 2)
    a_r = jnp.concatenate([lanes(pr[S5_T, 0]), lanes(pr[S5_T, 1])], axis=-1)
    a_i = jnp.concatenate([lanes(pi[S5_T, 0]), lanes(pi[S5_T, 1])], axis=-1)
    dsk = jnp.pad(d_skip.astype(F32).reshape(nt, S5_LANES), ((0, 0), (0, S5_K - S5_LANES)))
    aux = jnp.stack([a_r, a_i, dsk] + [jnp.zeros_like(a_r)] * 5, axis=1)
    return ef, eb, m, ff, fb, aux


def _s5_kernel(a_ref, ef_ref, eb_ref, m_ref, ff_ref, fb_ref, aux_ref, y_ref, u_scr, wf_scr, wb_scr):
    for b in range(BATCH):
        for t in range(S5_T):
            u_scr.at[t][pl.ds(S5_HB * BATCH + b, S5_XB, stride=BATCH), :] = (
                a_ref[pl.ds(b * SEQ + t, S5_XB, stride=S5_T), :])
            u_scr.at[t][pl.ds(b, S5_HB, stride=BATCH), :] = a_ref[pl.ds(NX + b * CTX_LEN + t, S5_HB, stride=S5_T), :]

    def get_tiles(scr, r0):
        return jnp.concatenate([scr.at[t][pl.ds(r0, S5_RC), :] for t in range(S5_T)], axis=1).astype(BF16)

    def put_tiles(scr, r0, val):
        for t in range(S5_T):
            scr.at[t][pl.ds(r0, S5_RC), :] = val[:, t * S5_LANES:(t + 1) * S5_LANES]

    def block_rows(rc):
        r0 = pl.multiple_of(rc * S5_RC, S5_RC)
        return r0, get_tiles(u_scr, r0)

    def carry_in(rc, _):
        r0, ub = block_rows(rc)
        put_tiles(wf_scr, r0, jnp.dot(ub, ef_ref[...], preferred_element_type=F32))
        put_tiles(wb_scr, r0, jnp.dot(ub, eb_ref[...], preferred_element_type=F32))
        return 0

    lax.fori_loop(0, S5_ROWS // S5_RC, carry_in, 0)

    aux = aux_ref[...]
    npair = S5_LG // 2
    shape = (BATCH, S5_LANES)
    coef = []
    for d in range(2):
        for j in range(npair):
            lo = d * (S5_K // 2) + j * S5_LANES
            coef.append((jnp.broadcast_to(aux[0:1, lo:lo + S5_LANES], shape),
                         jnp.broadcast_to(aux[1:2, lo:lo + S5_LANES], shape)))

    def scan(s, carry):
        rf = pl.multiple_of(s * BATCH, BATCH)
        gb = jnp.where(s < S5_HB, S5_HB - 1 - s, S5_NB + S5_HB - 1 - s)
        rb = pl.multiple_of(gb * BATCH, BATCH)
        out = []
        for d, (w_scr, r) in enumerate(((wf_scr, rf), (wb_scr, rb))):
            for j in range(npair):
                xr, xi = carry[d * npair + j]
                ar, ai = coef[d * npair + j]
                w_re, w_im = w_scr.at[2 * j], w_scr.at[2 * j + 1]
                wr = w_re[pl.ds(r, BATCH), :]
                wi = w_im[pl.ds(r, BATCH), :]
                w_re[pl.ds(r, BATCH), :] = xr
                w_im[pl.ds(r, BATCH), :] = xi
                out.append((ar * xr - ai * xi + wr, ar * xi + ai * xr + wi))
        return tuple(out)

    z = jnp.zeros(shape, F32)
    lax.fori_loop(0, S5_NB, scan, tuple((z, z) for _ in range(2 * npair)))

    def read_out(rc, _):
        r0, ub = block_rows(rc)
        y = jnp.dot(ub, m_ref[...], preferred_element_type=F32)
        y = y + jnp.dot(get_tiles(wf_scr, r0), ff_ref[...], preferred_element_type=F32)
        y = y + jnp.dot(get_tiles(wb_scr, r0), fb_ref[...], preferred_element_type=F32)
        put_tiles(u_scr, r0, y)
        return 0

    lax.fori_loop(0, S5_ROWS // S5_RC, read_out, 0)

    for b in range(BATCH):
        for t in range(S5_T):
            y_ref[pl.ds(b * SEQ + t, S5_XB, stride=S5_T), :] = (
                u_scr.at[t][pl.ds(S5_HB * BATCH + b, S5_XB, stride=BATCH), :])
            y_ref[pl.ds(NX + b * CTX_LEN + t, S5_HB, stride=S5_T), :] = u_scr.at[t][pl.ds(b, S5_HB, stride=BATCH), :]
    y_ref[...] = y_ref[...] + a_ref[...] * aux[2:3, 0:S5_LANES]


def _s5_mix(a, layer, ef, eb, m, ff, fb, aux):
    wspec = lambda: pl.BlockSpec((None, None, S5_K, S5_K), lambda g: (layer, g, 0, 0))
    return pl.pallas_call(
        _s5_kernel,
        out_shape=jax.ShapeDtypeStruct((NT, S5_WIDTH), F32),
        grid=(S5_TILES,),
        in_specs=[
            pl.BlockSpec((NT, S5_LANES), lambda g: (0, COL_S5 // S5_LANES + g)),
            wspec(), wspec(), wspec(), wspec(), wspec(),
            pl.BlockSpec((None, None, 8, S5_K), lambda g: (layer, g, 0, 0)),
        ],
        out_specs=pl.BlockSpec((NT, S5_LANES), lambda g: (0, g)),
        scratch_shapes=[pltpu.VMEM((S5_T, S5_ROWS, S5_LANES), F32)] * 3,
        compiler_params=_cparams(("arbitrary",), VMEM_LIMIT_BIG),
        name="s5_scan",
    )(a, ef, eb, m, ff, fb, aux)


def _chunk_index(d, s):
    is_ctx = s < H_CHUNKS
    fwd = jnp.where(is_ctx, s, s - H_CHUNKS)
    bwd = jnp.where(is_ctx, H_CHUNKS - 1 - s, MIX_STEPS - 1 - s)
    return is_ctx, jnp.where(d == 0, fwd, bwd)


def _token_block(b, d, s):
    is_ctx, c = _chunk_index(d, s)
    return jnp.where(is_ctx, NX // CHUNK + b * H_CHUNKS + c, b * X_CHUNKS + c)


def _rope_block(b, d, s):
    is_ctx, c = _chunk_index(d, s)
    return jnp.where(is_ctx, X_CHUNKS + c, c)


def _dot_nt(a, b):
    return lax.dot_general(a, b, (((1,), (1,)), ((), ())), preferred_element_type=F32)


def _dot_tn(a, b):
    return lax.dot_general(a, b, (((0,), (0,)), ((), ())), preferred_element_type=F32)


def _dot(a, b):
    return jnp.dot(a, b, preferred_element_type=F32)


def _head(ref, h):
    return ref[:, h * HEAD_DIM:(h + 1) * HEAD_DIM]


def _chunk_pos(rev):
    i = lax.broadcasted_iota(jnp.int32, (CHUNK, CHUNK), 0)
    j = lax.broadcasted_iota(jnp.int32, (CHUNK, CHUNK), 1)
    pi = jnp.where(rev, CHUNK - 1 - i, i)
    pj = jnp.where(rev, CHUNK - 1 - j, j)
    return pi, pj


def _ret_kernel(dl_ref, qf_ref, kf_ref, vf_ref, cosf_ref, sinf_ref, qb_ref, kb_ref, vb_ref, cosb_ref, sinb_ref,
                of_ref, ob_ref, r_scr, dec_scr, qd_scr, kd_scr, cd_scr):
    s = pl.program_id(1)

    @pl.when(s == 0)
    def _():
        for d in range(2):
            pi, pj = _chunk_pos(d == 1)
            rel = (pi - pj).astype(F32)
            pif = pi.astype(F32)
            for h in range(HEADS):
                x = d * HEADS + h
                lg = _log_sigmoid(dl_ref[d, h])[0:1, :]
                dec_scr[x] = jnp.where(rel >= 0, jnp.exp(lg * jnp.maximum(rel, 0.0)), 0.0)
                qd_scr[x] = jnp.exp(lg * (pif + 1.0))
                kd_scr[x] = jnp.exp(lg * (CHUNK - 1.0 - pif))
                cd_scr[x] = jnp.exp(jnp.broadcast_to(lg, (8, HEAD_DIM)) * CHUNK)
        r_scr[...] = jnp.zeros_like(r_scr)

    scale = HEAD_DIM ** -0.5
    dirs = ((qf_ref, kf_ref, vf_ref, cosf_ref, sinf_ref, of_ref), (qb_ref, kb_ref, vb_ref, cosb_ref, sinb_ref, ob_ref))
    ch = [(d, h) for d in range(2) for h in range(HEADS)]
    xs = range(len(ch))
    cos = [dirs[d][3][...] for d in range(2)]
    sin = [dirs[d][4][...] for d in range(2)]
    q = [_head(dirs[d][0], h) for d, h in ch]
    k = [_head(dirs[d][1], h) for d, h in ch]
    v = [_head(dirs[d][2], h).astype(BF16) for d, h in ch]
    q = [q[x] * cos[ch[x][0]] + pltpu.roll(q[x], HEAD_DIM // 2, 1) * sin[ch[x][0]] for x in xs]
    k = [(k[x] * cos[ch[x][0]] + pltpu.roll(k[x], HEAD_DIM // 2, 1) * sin[ch[x][0]]) * scale for x in xs]
    r = [r_scr[x] for x in xs]
    sm = [_dot_nt(q[x].astype(BF16), k[x].astype(BF16)) * dec_scr[x] for x in xs]
    inter = [_dot((q[x] * qd_scr[x]).astype(BF16), r[x].astype(BF16)) for x in xs]
    kv = [_dot_tn((k[x] * kd_scr[x]).astype(BF16), v[x]) for x in xs]
    for x in xs:
        d, h = ch[x]
        dirs[d][5][:, h * HEAD_DIM:(h + 1) * HEAD_DIM] = _dot(sm[x].astype(BF16), v[x]) + inter[x]
        r_scr[x] = cd_scr[x][0:1, :] * r[x] + kv[x]


def _retention(a, rope_cos, rope_sin, decay_logit):
    dl = jnp.broadcast_to(decay_logit.astype(F32)[:, :, None, None], (2, HEADS, 8, HEAD_DIM))
    col0 = COL_RET // RET_WIDTH
    specs = [pl.BlockSpec((2, HEADS, 8, HEAD_DIM), lambda b, s: (0, 0, 0, 0))]
    for d in range(2):
        for col in (col0, col0 + 1, col0 + 2):
            specs.append(pl.BlockSpec((CHUNK, RET_WIDTH), lambda b, s, d=d, col=col: (_token_block(b, d, s), col)))
        specs += [pl.BlockSpec((CHUNK, HEAD_DIM), lambda b, s, d=d: (_rope_block(b, d, s), 0))] * 2
    hh = pltpu.VMEM((2 * HEADS, CHUNK, HEAD_DIM), F32)
    out = jax.ShapeDtypeStruct((NT, RET_WIDTH), F32)
    return pl.pallas_call(
        _ret_kernel,
        out_shape=(out, out),
        grid=(BATCH, MIX_STEPS),
        in_specs=specs,
        out_specs=[pl.BlockSpec((CHUNK, RET_WIDTH), lambda b, s, d=d: (_token_block(b, d, s), 0)) for d in range(2)],
        scratch_shapes=[hh, hh, hh, hh, pltpu.VMEM((2 * HEADS, 8, HEAD_DIM), F32)],
        compiler_params=_cparams(("arbitrary", "arbitrary")),
        name="retention",
    )(dl, a, a, a, rope_cos, rope_sin, a, a, a, rope_cos, rope_sin)


def _split3(x):
    h = x.astype(BF16)
    r = x - h.astype(F32)
    m = r.astype(BF16)
    return h, m, (r - m.astype(F32)).astype(BF16)


def _mlstm_kernel(qf_ref, kf_ref, vf_ref, gf_ref, qb_ref, kb_ref, vb_ref, gb_ref, bias_ref, of_ref, ob_ref,
                  c_scr, n_scr, m_scr):
    s = pl.program_id(1)

    @pl.when(s == 0)
    def _():
        c_scr[...] = jnp.zeros_like(c_scr)
        n_scr[...] = jnp.zeros_like(n_scr)
        m_scr[...] = jnp.zeros_like(m_scr)

    scale = HEAD_DIM ** -0.5
    neg_inf = -jnp.inf
    dirs = ((qf_ref, kf_ref, vf_ref, gf_ref, of_ref), (qb_ref, kb_ref, vb_ref, gb_ref, ob_ref))
    causal, pre, pre_t, b_cols, b_rows = [], [], [], [], []
    for d in range(2):
        pi, pj = _chunk_pos(d == 1)
        cz = pj <= pi
        tri = jnp.where(cz, 1.0, 0.0).astype(BF16)
        p = dirs[d][3][...] + bias_ref[...]
        pt = p.T[0:32, :]
        causal.append(cz)
        pre.append(p)
        pre_t.append(pt)
        b_cols.append(sum(_dot(tri, part) for part in _split3(_log_sigmoid(p))))
        b_rows.append(sum(_dot_nt(part, tri) for part in _split3(_log_sigmoid(pt))))
    ch = [(d, h) for d in range(2) for h in range(HEADS)]
    xs = range(len(ch))
    ci = [2 * d * HEADS + h for d, h in ch]
    cf = [(2 * d + 1) * HEADS + h for d, h in ch]
    last = [0 if d == 1 else CHUNK - 1 for d, h in ch]
    i_col = [pre[ch[x][0]][:, ci[x]:ci[x] + 1] for x in xs]
    b_col = [b_cols[ch[x][0]][:, cf[x]:cf[x] + 1] for x in xs]
    i_row = [pre_t[ch[x][0]][ci[x]:ci[x] + 1, :] for x in xs]
    b_row = [b_rows[ch[x][0]][cf[x]:cf[x] + 1, :] for x in xs]
    b_end = [b_cols[ch[x][0]][last[x]:last[x] + 1, cf[x]:cf[x] + 1] for x in xs]
    m_prev = [m_scr[x][0:1, 0:1] for x in xs]
    q = [_head(dirs[d][0], h) for d, h in ch]
    k = [_head(dirs[d][1], h) * scale for d, h in ch]
    v = [_head(dirs[d][2], h).astype(BF16) for d, h in ch]
    qb = [q[x].astype(BF16) for x in xs]
    c_mem = [c_scr[x] for x in xs]
    n_mem = [n_scr[x][0:1, :] for x in xs]
    log_w = [jnp.where(causal[ch[x][0]], b_col[x] - b_row[x] + i_row[x], neg_inf) for x in xs]
    log_a = [b_col[x] + m_prev[x] for x in xs]
    m_t = [jnp.maximum(log_a[x], jnp.max(log_w[x], axis=-1, keepdims=True)) for x in xs]
    w = [jnp.exp(log_w[x] - m_t[x]) for x in xs]
    a = [jnp.exp(log_a[x] - m_t[x]) for x in xs]
    sm = [_dot_nt(qb[x], k[x].astype(BF16)) * w[x] for x in xs]
    inter = [_dot(qb[x], c_mem[x].astype(BF16)) for x in xs]
    num = [_dot(sm[x].astype(BF16), v[x]) + a[x] * inter[x] for x in xs]
    den = [jnp.sum(sm[x] + a[x] * (q[x] * n_mem[x]), axis=-1, keepdims=True) for x in xs]
    for x in xs:
        d, h = ch[x]
        dirs[d][4][:, h * HEAD_DIM:(h + 1) * HEAD_DIM] = num[x] / jnp.maximum(jnp.abs(den[x]), jnp.exp(-m_t[x]))
    log_w_end = [b_end[x] - b_col[x] + i_col[x] for x in xs]
    m_new = [jnp.maximum(b_end[x] + m_prev[x], jnp.max(log_w_end[x], axis=0, keepdims=True)) for x in xs]
    a_end = [jnp.exp(b_end[x] + m_prev[x] - m_new[x]) for x in xs]
    kw = [k[x] * jnp.exp(log_w_end[x] - m_new[x]) for x in xs]
    kv = [_dot_tn(kw[x].astype(BF16), v[x]) for x in xs]
    for x in xs:
        c_scr[x] = a_end[x] * c_mem[x] + kv[x]
        n_scr[x] = jnp.broadcast_to(a_end[x] * n_mem[x] + jnp.sum(kw[x], axis=0, keepdims=True), (8, HEAD_DIM))
        m_scr[x] = jnp.broadcast_to(m_new[x], (8, HEAD_DIM))


def _mlstm(a, igate_b, fgate_b):
    bias = jnp.stack([igate_b.astype(F32), fgate_b.astype(F32)], axis=1).reshape(1, 4 * HEADS)
    bias = jnp.pad(bias, ((0, 0), (0, HEAD_DIM - 4 * HEADS)))
    col0 = COL_ML // MLSTM_WIDTH
    specs = []
    for d in range(2):
        for col in (col0, col0 + 1, col0 + 2):
            specs.append(pl.BlockSpec((CHUNK, MLSTM_WIDTH), lambda b, s, d=d, col=col: (_token_block(b, d, s), col)))
        specs.append(pl.BlockSpec((CHUNK, HEAD_DIM), lambda b, s, d=d: (_token_block(b, d, s), COL_GATE // HEAD_DIM)))
    specs.append(pl.BlockSpec((1, HEAD_DIM), lambda b, s: (0, 0)))
    small = pltpu.VMEM((2 * HEADS, 8, HEAD_DIM), F32)
    out = jax.ShapeDtypeStruct((NT, MLSTM_WIDTH), F32)
    return pl.pallas_call(
        _mlstm_kernel,
        out_shape=(out, out),
        grid=(BATCH, MIX_STEPS),
        in_specs=specs,
        out_specs=[pl.BlockSpec((CHUNK, MLSTM_WIDTH), lambda b, s, d=d: (_token_block(b, d, s), 0))
                   for d in range(2)],
        scratch_shapes=[pltpu.VMEM((2 * HEADS, CHUNK, HEAD_DIM), F32), small, small],
        compiler_params=_cparams(("arbitrary", "arbitrary")),
        name="mlstm",
    )(a, a, a, a, a, a, a, a, bias)


def _head_norm(o, center):
    if center:
        o = o - jnp.mean(o, axis=-1, keepdims=True)
    return o * lax.rsqrt(jnp.mean(o * o, axis=-1, keepdims=True) + EPS)


def _out_proj_kernel(s5_ref, wg_ref, bg_ref, rf_ref, rb_ref, rg_ref, rw_ref, mf_ref, mb_ref, mg_ref, mw_ref, x_ref,
                     gate_ref, w_ref, o_ref):
    pair = 2 * HEAD_DIM
    z = jnp.dot(_gelu_tanh(s5_ref[...]).astype(BF16), wg_ref[...], preferred_element_type=F32) + bg_ref[...]
    s5 = (z[:, :S5_WIDTH] * _sigmoid(z[:, S5_WIDTH:])).astype(BF16)
    acc = jnp.dot(s5[:, 0:pair], w_ref[0:pair, :], preferred_element_type=F32)
    acc += jnp.dot(s5[:, pair:2 * pair], w_ref[pair:2 * pair, :], preferred_element_type=F32)
    for hp in range(HEADS // 2):
        rs, ms = [], []
        for h in (2 * hp, 2 * hp + 1):
            sl = slice(h * HEAD_DIM, (h + 1) * HEAD_DIM)
            r = _head_norm(rf_ref[:, sl] + rb_ref[:, sl], True) * rw_ref[:, sl] * _silu(rg_ref[:, sl])
            m = _head_norm(mf_ref[:, sl] + mb_ref[:, sl], False) * mw_ref[:, sl] * _sigmoid(mg_ref[:, sl])
            rs.append(r.astype(BF16))
            ms.append(m.astype(BF16))
        r0 = S5_WIDTH + hp * pair
        m0 = S5_WIDTH + RET_WIDTH + hp * pair
        acc += jnp.dot(jnp.concatenate(rs, axis=1), w_ref[r0:r0 + pair, :], preferred_element_type=F32)
        acc += jnp.dot(jnp.concatenate(ms, axis=1), w_ref[m0:m0 + pair, :], preferred_element_type=F32)
    o_ref[...] = x_ref[...] + gate_ref[...] * acc


def _out_proj(rows, s5y, w_glu, b_glu, ret_o, ml_o, a, ret_norm_w, mlstm_norm_w, xs, mod3, w_out, layer):
    tm = 256
    return pl.pallas_call(
        _out_proj_kernel,
        out_shape=jax.ShapeDtypeStruct((rows, D_MODEL), F32),
        grid=(rows // tm,),
        in_specs=[
            pl.BlockSpec((tm, S5_WIDTH), lambda i: (i, 0)),
            pl.BlockSpec((S5_WIDTH, 2 * S5_WIDTH), lambda i: (0, 0)),
            pl.BlockSpec((1, 2 * S5_WIDTH), lambda i: (0, 0)),
            pl.BlockSpec((tm, RET_WIDTH), lambda i: (i, 0)),
            pl.BlockSpec((tm, RET_WIDTH), lambda i: (i, 0)),
            pl.BlockSpec((tm, RET_WIDTH), lambda i: (i, COL_RET // RET_WIDTH + 3)),
            pl.BlockSpec((1, RET_WIDTH), lambda i: (0, 0)),
            pl.BlockSpec((tm, MLSTM_WIDTH), lambda i: (i, 0)),
            pl.BlockSpec((tm, MLSTM_WIDTH), lambda i: (i, 0)),
            pl.BlockSpec((tm, MLSTM_WIDTH), lambda i: (i, COL_ML // MLSTM_WIDTH + 3)),
            pl.BlockSpec((1, MLSTM_WIDTH), lambda i: (0, 0)),
            pl.BlockSpec((tm, D_MODEL), lambda i: (i, 0)),
            pl.BlockSpec((None, 1, D_MODEL), lambda i: (_row_group(i, tm), 0, 2)),
            pl.BlockSpec((None, D_MODEL, D_MODEL), lambda i: (layer, 0, 0), pipeline_mode=pl.Buffered(1)),
        ],
        out_specs=pl.BlockSpec((tm, D_MODEL), lambda i: (i, 0)),
        compiler_params=_cparams(("arbitrary",)),
        name="out_proj",
    )(s5y, w_glu.astype(BF16), b_glu.reshape(1, 2 * S5_WIDTH),
      ret_o[0], ret_o[1], a, ret_norm_w.reshape(1, RET_WIDTH), ml_o[0], ml_o[1], a,
      mlstm_norm_w.reshape(1, MLSTM_WIDTH),
      xs, mod3, w_out)


def _mlp_kernel(x_ref, nw_ref, shift_ref, scale_ref, gate_ref, w1_ref, w2_ref, nf_ref, o_ref, lhs_scr,
                *, final_norm):
    f = pl.program_id(1)

    @pl.when(f == 0)
    def _():
        lhs_scr[...] = _rms_modulate(x_ref[...], nw_ref[...], shift_ref[...], scale_ref[...]).astype(BF16)
        o_ref[...] = jnp.zeros_like(o_ref)

    hid = jnp.dot(lhs_scr[...], w1_ref[...].astype(BF16), preferred_element_type=F32)
    hid = jnp.square(jnp.maximum(hid, 0.0)).astype(BF16)
    o_ref[...] += jnp.dot(hid, w2_ref[...].astype(BF16), preferred_element_type=F32)

    @pl.when(f == pl.num_programs(1) - 1)
    def _():
        y = x_ref[...] + gate_ref[...] * o_ref[...]
        if final_norm:
            y = y * lax.rsqrt(jnp.mean(y * y, axis=-1, keepdims=True) + EPS) * nf_ref[...]
        o_ref[...] = y


def _mlp(rows, xs, norm_w, mod3, w1, w2, layer, norm_f_w, final_norm):
    tm, tf = 1024, 512
    modspec = lambda k: pl.BlockSpec((None, 1, D_MODEL), lambda i, f: (_row_group(i, tm), 0, k))
    once = pl.Buffered(1)
    return pl.pallas_call(
        functools.partial(_mlp_kernel, final_norm=final_norm),
        out_shape=jax.ShapeDtypeStruct((rows, D_MODEL), F32),
        grid=(rows // tm, D_FF // tf),
        in_specs=[
            pl.BlockSpec((tm, D_MODEL), lambda i, f: (i, 0)),
            pl.BlockSpec((1, D_MODEL), lambda i, f: (0, 0)),
            modspec(3), modspec(4), modspec(5),
            pl.BlockSpec((None, D_MODEL, tf), lambda i, f: (layer, 0, f)),
            pl.BlockSpec((None, tf, D_MODEL), lambda i, f: (layer, f, 0)),
            pl.BlockSpec((1, D_MODEL), lambda i, f: (0, 0)),
        ],
        out_specs=pl.BlockSpec((tm, D_MODEL), lambda i, f: (i, 0), pipeline_mode=once),
        scratch_shapes=[pltpu.VMEM((tm, D_MODEL), BF16)],
        compiler_params=_cparams(("arbitrary", "arbitrary"), VMEM_LIMIT_BIG),
        name="mlp",
    )(xs, norm_w.reshape(1, D_MODEL), mod3, mod3, mod3, w1, w2, norm_f_w.reshape(1, D_MODEL))


def _rope_tables():
    quarter = HEAD_DIM // 4
    rows = jnp.repeat(jnp.arange(SEQ // GRID_W, dtype=F32), GRID_W)
    cols = jnp.tile(jnp.arange(GRID_W, dtype=F32), SEQ // GRID_W)
    inv = ROPE_BASE ** (-jnp.arange(quarter, dtype=F32) / quarter)
    ang = jnp.concatenate([rows[:, None] * inv, cols[:, None] * inv], axis=-1)
    cos, sin = jnp.cos(ang), jnp.sin(ang)
    cos2 = jnp.concatenate([cos, cos], axis=-1)
    sin2 = jnp.concatenate([-sin, sin], axis=-1)
    cos2 = jnp.concatenate([cos2, jnp.ones((CTX_LEN, HEAD_DIM), F32)], axis=0)
    sin2 = jnp.concatenate([sin2, jnp.zeros((CTX_LEN, HEAD_DIM), F32)], axis=0)
    return cos2, sin2


def _permute_w_in(w):
    s5 = w[:, :S5_WIDTH]
    ret = w[:, S5_WIDTH:S5_WIDTH + 4 * RET_WIDTH]
    ml = w[:, S5_WIDTH + 4 * RET_WIDTH:S5_WIDTH + 4 * RET_WIDTH + 4 * MLSTM_WIDTH]
    gates = w[:, S5_WIDTH + 4 * RET_WIDTH + 4 * MLSTM_WIDTH:]
    pad = jnp.zeros((D_MODEL, IN_PAD - IN_WIDTH), w.dtype)
    return jnp.concatenate([ret, ml, s5, gates, pad], axis=1).astype(BF16)


def kernel(x, c, ctx, c_ctx, w_mod, b_mod, norm1_w, norm2_w, w_in, w_out, s5_lam_re, s5_lam_im, s5_log_step, s5_b_re, s5_b_im, s5_c_re, s5_c_im, s5_d, s5_w_glu, s5_b_glu, ret_decay_logit, ret_norm_w, mlstm_igate_b, mlstm_fgate_b, mlstm_norm_w, w_ff1, w_ff2, norm_f_w):
    rope_cos, rope_sin = _rope_tables()
    cvec = jnp.concatenate([c, c_ctx[None, :], jnp.zeros((MOD_ROWS - BATCH - 1, D_MODEL), F32)], axis=0)
    mod = _modulation(cvec, w_mod, b_mod)
    xs = jnp.concatenate([x.reshape(NX, D_MODEL), ctx.reshape(NH, D_MODEL)], axis=0)
    s5w = jax.vmap(_s5_prep)(s5_lam_re, s5_lam_im, s5_log_step, s5_b_re, s5_b_im, s5_c_re, s5_c_im, s5_d)
    w_out_b = w_out.astype(BF16)
    for l in range(DEPTH):
        last = l == DEPTH - 1
        mod3 = mod[l].reshape(MOD_ROWS, 1, 6 * D_MODEL)
        a = _in_proj(xs, norm1_w[l], mod3, _permute_w_in(w_in[l]))
        s5y = _s5_mix(a, l, *s5w)
        ret_o = _retention(a, rope_cos, rope_sin, ret_decay_logit[l])
        ml_o = _mlstm(a, mlstm_igate_b[l], mlstm_fgate_b[l])
        rows = NX if last else NT
        xs = _out_proj(rows, s5y, s5_w_glu[l], s5_b_glu[l], ret_o, ml_o, a, ret_norm_w[l], mlstm_norm_w[l], xs, mod3,
                       w_out_b, l)
        xs = _mlp(rows, xs, norm2_w[l], mod3, w_ff1, w_ff2, l, norm_f_w, last)
    return xs.reshape(BATCH, SEQ, D_MODEL)
```
